```python
import jax, jax.numpy as jnp
from jax import lax
import numpy as np

D_MODEL = 2048
BATCH = 4
SEQ = 2048
DEPTH = 2
DEC_BATCH = 128
DEC_SEQ = 1
PAST_LEN = 8192
PAGE_SIZE = 128

N_EVEN = (DEPTH + 1) // 2
N_ODD = DEPTH // 2
H_A = 16
KVH_A = 4
HD_A = 64
WINDOW = 128
SWA_BLOCK = 128
H_B = 4
DK_B = 128
DV_B = 256
GLA_RANK = 16
GLA_GATE_NORM = 16.0
GLA_CHUNK = 64
C_WIDTH = 1024
C_BLOCKS = 8
C_BLOCK = C_WIDTH // C_BLOCKS
CONV_W = 4
RG_C = 8.0
H_D = 8
KVH_D = 4
HD_D = 128
SB_BLOCK = 128
SB_BIAS_INIT = -6.0
D_FF = -(-(8 * D_MODEL) // (3 * 256)) * 256
PLE_DIM = 256
EPS = 1e-6
E_WIDTHS = (H_A * HD_A, KVH_A * HD_A, KVH_A * HD_A, H_B * DK_B, H_B * DK_B, H_B * DV_B, GLA_RANK, H_B * DV_B)
E_IN = sum(E_WIDTHS)
E_MIX = H_A * HD_A + H_B * DV_B
O_WIDTHS = (C_WIDTH, C_WIDTH, H_D * HD_D, KVH_D * HD_D, KVH_D * HD_D)
O_IN = sum(O_WIDTHS)
O_MIX = C_WIDTH + H_D * HD_D

kernel_name = 'hybrid_swa_gla_rglru_stickbreak_step'

F32 = jnp.float32


def rmsnorm(x, g):
    xf = x.astype(F32)
    y = xf * lax.rsqrt(jnp.mean(xf * xf, axis=-1, keepdims=True) + EPS)
    return (y * g.astype(F32)).astype(x.dtype)


def split_cols(z, widths):
    idx = [int(c) for c in np.cumsum(widths)[:-1]]
    return jnp.split(z, idx, axis=-1)


def sink_probs(s, mask, sink):
    s = jnp.where(mask, s, -jnp.inf)
    m = jnp.maximum(jnp.max(s, axis=-1), sink[:, :, None])
    e = jnp.exp(s - m[..., None])
    den = jnp.sum(e, axis=-1) + jnp.exp(sink[:, :, None] - m)
    return e / den[..., None]


def swa_prompt(q, k, v, sinks):
    bsz, t = q.shape[:2]
    nb = t // SWA_BLOCK
    g = H_A // KVH_A
    qb = q.reshape(bsz, nb, SWA_BLOCK, KVH_A, g, HD_A).astype(F32)
    kb = k.reshape(bsz, nb, SWA_BLOCK, KVH_A, HD_A).astype(F32)
    vb = v.reshape(bsz, nb, SWA_BLOCK, KVH_A, HD_A).astype(F32)
    pad = ((0, 0), (1, 0), (0, 0), (0, 0), (0, 0))
    kk = jnp.concatenate([jnp.pad(kb[:, :-1], pad), kb], axis=2)
    vv = jnp.concatenate([jnp.pad(vb[:, :-1], pad), vb], axis=2)
    s = jnp.einsum('bnqkgd,bnskd->bnkgqs', qb, kk) * HD_A ** -0.5
    qi = jnp.arange(SWA_BLOCK)[:, None]
    kj = jnp.arange(2 * SWA_BLOCK)[None, :]
    diff = qi - kj + SWA_BLOCK
    valid = (jnp.arange(nb)[:, None, None] * SWA_BLOCK - SWA_BLOCK + kj) >= 0
    mask = (diff >= 0) & (diff <= WINDOW) & valid
    p = sink_probs(s, mask[None, :, None, None], sinks.reshape(KVH_A, g).astype(F32))
    o = jnp.einsum('bnkgqs,bnskd->bnqkgd', p, vv)
    return o.reshape(bsz, t, H_A * HD_A)


def swa_decode(q, k, v, buf_k, buf_v, sinks):
    bsz, t = q.shape[:2]
    buf_len = buf_k.shape[1]
    g = H_A // KVH_A
    kk = jnp.concatenate([buf_k, k.astype(buf_k.dtype)], axis=1)
    vv = jnp.concatenate([buf_v, v.astype(buf_v.dtype)], axis=1)
    qg = q.reshape(bsz, t, KVH_A, g, HD_A).astype(F32)
    s = jnp.einsum('bqkgd,bskd->bkgqs', qg, kk.astype(F32)) * HD_A ** -0.5
    diff = jnp.arange(t)[:, None] + buf_len - jnp.arange(buf_len + t)[None, :]
    mask = (diff >= 0) & (diff <= WINDOW)
    p = sink_probs(s, mask, sinks.reshape(KVH_A, g).astype(F32))
    o = jnp.einsum('bkgqs,bskd->bqkgd', p, vv.astype(F32))
    return o.reshape(bsz, t, H_A * HD_A), kk[:, -buf_len:], vv[:, -buf_len:]


def gla_prompt(q, k, v, logdec):
    bsz, t = q.shape[:2]
    nc = t // GLA_CHUNK
    shp = lambda a: a.reshape((bsz, nc, GLA_CHUNK) + a.shape[2:])
    q, k, v, logdec = shp(q), shp(k), shp(v), shp(logdec)
    b = jnp.cumsum(logdec, axis=2)
    qt = q * jnp.exp(b)
    kt = k * jnp.exp(-b)
    att = jnp.einsum('bnthk,bnshk->bnhts', qt, kt)
    causal = jnp.tril(jnp.ones((GLA_CHUNK, GLA_CHUNK), dtype=bool))
    o_intra = jnp.einsum('bnhts,bnshv->bnthv', jnp.where(causal, att, 0.0), v)
    b_last = b[:, :, -1]
    d_state = jnp.einsum('bnshk,bnshv->bnhkv', k * jnp.exp(b_last[:, :, None] - b), v)
    decay = jnp.exp(b_last)

    def step(state, inp):
        dec, ds = inp
        return dec[..., None] * state + ds, state

    s0 = jnp.zeros((bsz, H_B, DK_B, DV_B), F32)
    s_final, s_starts = lax.scan(step, s0, (jnp.moveaxis(decay, 1, 0), jnp.moveaxis(d_state, 1, 0)))
    o_inter = jnp.einsum('bnthk,bnhkv->bnthv', qt, jnp.moveaxis(s_starts, 0, 1))
    o = (o_intra + o_inter).reshape(bsz, t, H_B, DV_B)
    return o, s_final


def gla_decode(q, k, v, logdec, s0):
    def step(state, inp):
        qt, kt, vt, gt = inp
        state = jnp.exp(gt)[..., None] * state + kt[..., :, None] * vt[..., None, :]
        return state, jnp.einsum('bhk,bhkv->bhv', qt, state)

    xs = tuple(jnp.moveaxis(a, 1, 0) for a in (q, k, v, logdec))
    s_final, o = lax.scan(step, s0.astype(F32), xs)
    return jnp.moveaxis(o, 0, 1), s_final


def causal_conv(x, buf, w, b):
    t = x.shape[1]
    xp = jnp.concatenate([buf.astype(x.dtype), x], axis=1)
    y = b + sum(xp[:, j:j + t] * w[j] for j in range(CONV_W))
    return y, xp[:, t:]


def block_diag(x, w, b):
    xb = x.reshape(x.shape[:-1] + (C_BLOCKS, C_BLOCK))
    return jnp.einsum('btni,nij->btnj', xb, w.astype(x.dtype)).reshape(x.shape) + b.astype(x.dtype)


def rglru(x, h0, w_a, b_a, w_x, b_x, lam):
    xf = x.astype(F32)
    r_gate = jax.nn.sigmoid(block_diag(xf, w_a, b_a))
    i_gate = jax.nn.sigmoid(block_diag(xf, w_x, b_x))
    log_a = -RG_C * r_gate * jax.nn.softplus(-lam.astype(F32))
    a = jnp.exp(log_a)
    u = jnp.sqrt(-jnp.expm1(2.0 * log_a)) * (i_gate * xf)
    u = u.at[:, 0].add(a[:, 0] * h0.astype(F32))

    def combine(left, right):
        al, ul = left
        ar, ur = right
        return al * ar, ar * ul + ur

    _, h = lax.associative_scan(combine, (a, u), axis=1)
    return h, h[:, -1]


def sb_prompt(q, k, v, bias):
    bsz, t = q.shape[:2]
    g = H_D // KVH_D
    nb = t // SB_BLOCK
    qb = jnp.moveaxis(q.reshape(bsz, nb, SB_BLOCK, KVH_D, g, HD_D).astype(F32), 1, 0)
    kf = k.astype(F32)
    vf = v.astype(F32)
    bb = bias.reshape(KVH_D, g).astype(F32)[None, :, :, None, None]
    s_pos = jnp.arange(t)

    def block(args):
        qi, n = args
        z = jnp.einsum('bqkgd,bskd->bkgqs', qi, kf) * HD_D ** -0.5 + bb
        t_pos = n * SB_BLOCK + jnp.arange(SB_BLOCK)
        mask = s_pos[None, :] < t_pos[:, None]
        log1m = jnp.where(mask, jax.nn.log_sigmoid(-z), 0.0)
        suffix = lax.cumsum(log1m, axis=4, reverse=True) - log1m
        w = jnp.where(mask, jnp.exp(jax.nn.log_sigmoid(z) + suffix), 0.0)
        return jnp.einsum('bkgqs,bskd->bqkgd', w, vf)

    o = lax.map(block, (qb, jnp.arange(nb)))
    return jnp.moveaxis(o, 0, 1).reshape(bsz, t, H_D * HD_D)


def sb_decode(q, k, v, bias, cache_k, cache_v, li, page_table):
    bsz, t = q.shape[:2]
    g = H_D // KVH_D
    qg = q.reshape(bsz, t, KVH_D, g, HD_D).astype(F32)
    bb = bias.reshape(KVH_D, g).astype(F32)[None, :, :, None, None]
    z = jnp.einsum('bqkgd,bskd->bkgqs', qg, k.astype(F32)) * HD_D ** -0.5 + bb
    mask = jnp.arange(t)[None, :] < jnp.arange(t)[:, None]
    log1m = jnp.where(mask, jax.nn.log_sigmoid(-z), 0.0)
    suffix = lax.cumsum(log1m, axis=4, reverse=True) - log1m
    w = jnp.where(mask, jnp.exp(jax.nn.log_sigmoid(z) + suffix), 0.0)
    o0 = jnp.einsum('bkgqs,bskd->bqkgd', w, v.astype(F32))
    surv0 = jnp.sum(log1m, axis=-1)

    def page_step(carry, phys):
        o, surv = carry
        kp = cache_k[li, phys].astype(F32)
        vp = cache_v[li, phys].astype(F32)
        zp = jnp.einsum('bqkgd,bskd->bkgqs', qg, kp) * HD_D ** -0.5 + bb
        l1m = jax.nn.log_sigmoid(-zp)
        suf = lax.cumsum(l1m, axis=4, reverse=True) - l1m + surv[..., None]
        wp = jnp.exp(jax.nn.log_sigmoid(zp) + suf)
        o = o + jnp.einsum('bkgqs,bskd->bqkgd', wp, vp)
        return (o, surv + jnp.sum(l1m, axis=-1)), None

    (o, _), _ = lax.scan(page_step, (o0, surv0), page_table.T, reverse=True)
    return o.reshape(bsz, t, H_D * HD_D)


def even_mixer(xn, w_in, g_qn, g_kn, sinks, w_g2, b_g2, g_go, w_out, st):
    bsz, t = xn.shape[:2]
    qa, ka, va, qb, kb, vb, glr, gout = split_cols(xn @ w_in, E_WIDTHS)
    qa = rmsnorm(qa.reshape(bsz, t, H_A, HD_A), g_qn)
    ka = rmsnorm(ka.reshape(bsz, t, KVH_A, HD_A), g_kn)
    va = va.reshape(bsz, t, KVH_A, HD_A)
    qb = qb.reshape(bsz, t, H_B, DK_B).astype(F32) * DK_B ** -0.5
    kb = kb.reshape(bsz, t, H_B, DK_B).astype(F32)
    vb = vb.reshape(bsz, t, H_B, DV_B).astype(F32)
    logdec = (jax.nn.log_sigmoid((glr @ w_g2 + b_g2).astype(F32)) / GLA_GATE_NORM).reshape(bsz, t, H_B, DK_B)
    if st is None:
        oa = swa_prompt(qa, ka, va, sinks)
        n_keep = min(WINDOW, t)
        new_k, new_v = ka[:, -n_keep:], va[:, -n_keep:]
        ob, s_new = gla_prompt(qb, kb, vb, logdec)
    else:
        buf_k, buf_v, s0 = st
        oa, new_k, new_v = swa_decode(qa, ka, va, buf_k, buf_v, sinks)
        ob, s_new = gla_decode(qb, kb, vb, logdec, s0)
    ob = rmsnorm(ob, g_go) * jax.nn.silu(gout.reshape(bsz, t, H_B, DV_B).astype(F32))
    mix = jnp.concatenate([oa.astype(xn.dtype), ob.reshape(bsz, t, H_B * DV_B).astype(xn.dtype)], axis=-1)
    return mix @ w_out, (new_k, new_v, s_new.astype(xn.dtype))


def odd_mixer(xn, w_in, conv_w, conv_b, w_ra, b_ra, w_rx, b_rx, lam, g_qn, g_kn, sb_bias, w_out, st):
    bsz, t = xn.shape[:2]
    cx, cy, qd, kd, vd = split_cols(xn @ w_in, O_WIDTHS)
    if st is None:
        buf = jnp.zeros((bsz, CONV_W - 1, C_WIDTH), xn.dtype)
        h0 = jnp.zeros((bsz, C_WIDTH), F32)
    else:
        buf, h0, cache_k, cache_v, li, page_table = st
    xc, new_buf = causal_conv(cx, buf, conv_w, conv_b)
    hseq, h_last = rglru(xc, h0, w_ra, b_ra, w_rx, b_rx, lam)
    oc = hseq * jax.nn.gelu(cy.astype(F32))
    qd = rmsnorm(qd.reshape(bsz, t, H_D, HD_D), g_qn)
    kd = rmsnorm(kd.reshape(bsz, t, KVH_D, HD_D), g_kn)
    vd = vd.reshape(bsz, t, KVH_D, HD_D)
    if st is None:
        od = sb_prompt(qd, kd, vd, sb_bias)
    else:
        od = sb_decode(qd, kd, vd, sb_bias, cache_k, cache_v, li, page_table)
    mix = jnp.concatenate([oc.astype(xn.dtype), od.astype(xn.dtype)], axis=-1)
    return mix @ w_out, (new_buf, h_last.astype(xn.dtype), kd, vd)


def trunk(x, p, w, st):
    h = x
    new = {n: [] for n in ('swa_k', 'swa_v', 'gla', 'conv', 'lru', 'sb_k', 'sb_v')}
    for i in range(DEPTH):
        li = i // 2
        xn = rmsnorm(h, w['g_mix'][i])
        if i % 2 == 0:
            s = None if st is None else (st['swa_k'][li], st['swa_v'][li], st['gla'][li])
            o, (nk, nv, ns) = even_mixer(xn, w['w_in_even'][li], w['g_qnorm_a'][li], w['g_knorm_a'][li], w['sinks_a'][li], w['w_gla_gate2'][li], w['b_gla_gate2'][li], w['g_gla_out'][li], w['w_out_even'][li], s)
            new['swa_k'].append(nk)
            new['swa_v'].append(nv)
            new['gla'].append(ns)
        else:
            s = None if st is None else (st['conv'][li], st['lru'][li], st['cache_k'], st['cache_v'], li, st['page_table'])
            o, (nc, nh, nk, nv) = odd_mixer(xn, w['w_in_odd'][li], w['conv_w'][li], w['conv_b'][li], w['w_rg_a'][li], w['b_rg_a'][li], w['w_rg_x'][li], w['b_rg_x'][li], w['lru_lambda'][li], w['g_qnorm_d'][li], w['g_knorm_d'][li], w['sb_bias'][li], w['w_out_odd'][li], s)
            new['conv'].append(nc)
            new['lru'].append(nh)
            new['sb_k'].append(nk)
            new['sb_v'].append(nv)
        h = h + o
        hn = rmsnorm(h, w['g_ffn'][i])
        h = h + (jax.nn.silu(hn @ w['w_ffn_gate'][i]) * (hn @ w['w_ffn_up'][i])) @ w['w_ffn_down'][i]
        gate = jax.nn.sigmoid(rmsnorm(h, w['g_ple'][i]) @ w['w_ple_gate'][i])
        h = h + gate * (p[i] @ w['w_ple_proj'][i])
    return h, {n: jnp.stack(v) for n, v in new.items()}


def setup_inputs(seed: int = 0) -> dict:
    key = jax.random.key(seed)
    ks = iter(jax.random.split(key, 64))

    def nrm(shape, scale=1.0):
        return jax.random.normal(next(ks), shape, F32) * scale

    def gain(shape):
        return 1.0 + nrm(shape, 0.02)

    n_pages = PAST_LEN // PAGE_SIZE
    n_used = DEC_BATCH * n_pages
    n_phys = n_used + n_used // 4
    buf_len = min(WINDOW, PAST_LEN)
    u = jax.random.uniform(next(ks), (N_ODD, C_WIDTH), F32, 0.9, 0.999)
    a_base = u ** (1.0 / RG_C)
    lru_lambda = jnp.log(a_base) - jnp.log1p(-a_base)
    page_table = jax.random.permutation(next(ks), n_phys)[:n_used].reshape(DEC_BATCH, n_pages).astype(jnp.int32)
    return {
        'x_prompt': nrm((BATCH, SEQ, D_MODEL)),
        'x_sample': nrm((DEC_BATCH, DEC_SEQ, D_MODEL)),
        'state_swa_k': nrm((N_EVEN, DEC_BATCH, buf_len, KVH_A, HD_A)),
        'state_swa_v': nrm((N_EVEN, DEC_BATCH, buf_len, KVH_A, HD_A)),
        'state_gla': nrm((N_EVEN, DEC_BATCH, H_B, DK_B, DV_B)),
        'state_conv': nrm((N_ODD, DEC_BATCH, CONV_W - 1, C_WIDTH)),
        'state_lru': nrm((N_ODD, DEC_BATCH, C_WIDTH), 0.5),
        'cache_sb_k': nrm((N_ODD, n_phys, PAGE_SIZE, KVH_D, HD_D)),
        'cache_sb_v': nrm((N_ODD, n_phys, PAGE_SIZE, KVH_D, HD_D)),
        'page_table': page_table,
        'p_prompt': nrm((DEPTH, BATCH, SEQ, PLE_DIM)),
        'p_sample': nrm((DEPTH, DEC_BATCH, DEC_SEQ, PLE_DIM)),
        'g_mix': gain((DEPTH, D_MODEL)),
        'g_ffn': gain((DEPTH, D_MODEL)),
        'w_ffn_gate': nrm((DEPTH, D_MODEL, D_FF), D_MODEL ** -0.5),
        'w_ffn_up': nrm((DEPTH, D_MODEL, D_FF), D_MODEL ** -0.5),
        'w_ffn_down': nrm((DEPTH, D_FF, D_MODEL), D_FF ** -0.5),
        'g_ple': gain((DEPTH, D_MODEL)),
        'w_ple_gate': nrm((DEPTH, D_MODEL, D_MODEL), D_MODEL ** -0.5),
        'w_ple_proj': nrm((DEPTH, PLE_DIM, D_MODEL), PLE_DIM ** -0.5),
        'w_in_even': nrm((N_EVEN, D_MODEL, E_IN), D_MODEL ** -0.5),
        'g_qnorm_a': gain((N_EVEN, HD_A)),
        'g_knorm_a': gain((N_EVEN, HD_A)),
        'sinks_a': nrm((N_EVEN, H_A)),
        'w_gla_gate2': nrm((N_EVEN, GLA_RANK, H_B * DK_B), GLA_RANK ** -0.5),
        'b_gla_gate2': nrm((N_EVEN, H_B * DK_B), 0.01),
        'g_gla_out': gain((N_EVEN, DV_B)),
        'w_out_even': nrm((N_EVEN, E_MIX, D_MODEL), E_MIX ** -0.5),
        'w_in_odd': nrm((N_ODD, D_MODEL, O_IN), D_MODEL ** -0.5),
        'conv_w': nrm((N_ODD, CONV_W, C_WIDTH), CONV_W ** -0.5),
        'conv_b': nrm((N_ODD, C_WIDTH), 0.01),
        'w_rg_a': nrm((N_ODD, C_BLOCKS, C_BLOCK, C_BLOCK), C_BLOCK ** -0.5),
        'b_rg_a': nrm((N_ODD, C_WIDTH), 0.01),
        'w_rg_x': nrm((N_ODD, C_BLOCKS, C_BLOCK, C_BLOCK), C_BLOCK ** -0.5),
        'b_rg_x': nrm((N_ODD, C_WIDTH), 0.01),
        'lru_lambda': lru_lambda,
        'g_qnorm_d': gain((N_ODD, HD_D)),
        'g_knorm_d': gain((N_ODD, HD_D)),
        'sb_bias': SB_BIAS_INIT + nrm((N_ODD, H_D), 0.5),
        'w_out_odd': nrm((N_ODD, O_MIX, D_MODEL), O_MIX ** -0.5),
    }


def reference(x_prompt, x_sample, state_swa_k, state_swa_v, state_gla, state_conv, state_lru, cache_sb_k, cache_sb_v, page_table, p_prompt, p_sample, g_mix, g_ffn, w_ffn_gate, w_ffn_up, w_ffn_down, g_ple, w_ple_gate, w_ple_proj, w_in_even, g_qnorm_a, g_knorm_a, sinks_a, w_gla_gate2, b_gla_gate2, g_gla_out, w_out_even, w_in_odd, conv_w, conv_b, w_rg_a, b_rg_a, w_rg_x, b_rg_x, lru_lambda, g_qnorm_d, g_knorm_d, sb_bias, w_out_odd):
    w = dict(g_mix=g_mix, g_ffn=g_ffn, w_ffn_gate=w_ffn_gate, w_ffn_up=w_ffn_up, w_ffn_down=w_ffn_down,
             g_ple=g_ple, w_ple_gate=w_ple_gate, w_ple_proj=w_ple_proj,
             w_in_even=w_in_even, g_qnorm_a=g_qnorm_a, g_knorm_a=g_knorm_a, sinks_a=sinks_a,
             w_gla_gate2=w_gla_gate2, b_gla_gate2=b_gla_gate2, g_gla_out=g_gla_out, w_out_even=w_out_even,
             w_in_odd=w_in_odd, conv_w=conv_w, conv_b=conv_b, w_rg_a=w_rg_a, b_rg_a=b_rg_a,
             w_rg_x=w_rg_x, b_rg_x=b_rg_x, lru_lambda=lru_lambda, g_qnorm_d=g_qnorm_d,
             g_knorm_d=g_knorm_d, sb_bias=sb_bias, w_out_odd=w_out_odd)
    y_prompt, sp = trunk(x_prompt, p_prompt, w, None)
    st = dict(swa_k=state_swa_k, swa_v=state_swa_v, gla=state_gla, conv=state_conv, lru=state_lru,
              cache_k=cache_sb_k, cache_v=cache_sb_v, page_table=page_table)
    y_sample, ss = trunk(x_sample, p_sample, w, st)
    return (y_prompt, y_sample, sp['swa_k'], sp['swa_v'], sp['gla'], sp['conv'], sp['lru'], sp['sb_k'], sp['sb_v'], ss['swa_k'], ss['swa_v'], ss['gla'], ss['conv'], ss['lru'], ss['sb_k'], ss['sb_v'])
```

```python
import functools

import jax
import jax.numpy as jnp
from jax import lax
from jax.experimental import pallas as pl
from jax.experimental.pallas import tpu as pltpu

F32 = jnp.float32
BF16 = jnp.bfloat16
EPS = 1e-6

V7X_VMEM_LIMIT_BYTES = 56 * 1024 * 1024

H_A, KVH_A, HD_A, WINDOW = 16, 4, 64, 128
H_B, DK_B, DV_B, GLA_RANK, GLA_GATE_NORM, GLA_CHUNK = 4, 128, 256, 16, 16.0, 64
C_WIDTH, C_BLOCKS, CONV_W, RG_C = 1024, 8, 4, 8.0
C_BLOCK = C_WIDTH // C_BLOCKS
H_D, KVH_D, HD_D = 8, 4, 128
PAGE = 128

ROW_TILE = 640
SB_PAGES_PER_STEP = 8


def _cp(sem):
    return pltpu.CompilerParams(dimension_semantics=sem, vmem_limit_bytes=V7X_VMEM_LIMIT_BYTES)


def _rms_rows(x, g):
    r = lax.rsqrt(jnp.mean(x * x, axis=-1, keepdims=True) + EPS)
    return (x * r) * g


def _softplus(x):
    return jnp.maximum(x, 0.0) + jnp.log1p(jnp.exp(-jnp.abs(x)))


def _log_sigmoid(x):
    return jnp.minimum(x, 0.0) - jnp.log1p(jnp.exp(-jnp.abs(x)))


def _split_bf16(x):
    hi = x.astype(BF16)
    lo = (x - hi.astype(F32)).astype(BF16)
    return hi, lo


def _dot(a, b):
    return jnp.dot(a, b, preferred_element_type=F32)


def _dot_nt(a, b):
    return lax.dot_general(a, b, (((1,), (1,)), ((), ())), preferred_element_type=F32)


def _dot_tn(a, b):
    return lax.dot_general(a, b, (((0,), (0,)), ((), ())), preferred_element_type=F32)


def _norm_matmul_kernel(x_ref, g_ref, w_ref, o_ref, xn_ref):
    @pl.when(pl.program_id(1) == 0)
    def _():
        xn_ref[...] = _rms_rows(x_ref[...], g_ref[...]).astype(BF16)

    o_ref[...] = _dot(xn_ref[...], w_ref[...]).astype(o_ref.dtype)


def norm_matmul(x, g, w, tn, name):
    n, d = x.shape
    nout = w.shape[1]
    tm = ROW_TILE
    return pl.pallas_call(
        _norm_matmul_kernel,
        out_shape=jax.ShapeDtypeStruct((n, nout), F32),
        grid=(n // tm, nout // tn),
        in_specs=[
            pl.BlockSpec((tm, d), lambda i, j: (i, 0)),
            pl.BlockSpec((1, d), lambda i, j: (0, 0)),
            pl.BlockSpec((d, tn), lambda i, j: (0, j)),
        ],
        out_specs=pl.BlockSpec((tm, tn), lambda i, j: (i, j)),
        scratch_shapes=[pltpu.VMEM((tm, d), BF16)],
        compiler_params=_cp(("parallel", "arbitrary")),
        name=name,
    )(x, g.reshape(1, d), w)


def _proj_res_kernel(h_ref, a_ref, b_ref, wa_ref, wb_ref, o_ref):
    acc = _dot(a_ref[...], wa_ref[...])
    acc = acc + _dot(b_ref[...], wb_ref[...])
    o_ref[...] = h_ref[...] + acc


def proj_residual(h, a, b, wa, wb, name):
    n, d = h.shape
    ka, kb = a.shape[1], b.shape[1]
    tm = ROW_TILE
    return pl.pallas_call(
        _proj_res_kernel,
        out_shape=jax.ShapeDtypeStruct((n, d), F32),
        grid=(n // tm,),
        in_specs=[
            pl.BlockSpec((tm, d), lambda i: (i, 0)),
            pl.BlockSpec((tm, ka), lambda i: (i, 0)),
            pl.BlockSpec((tm, kb), lambda i: (i, 0)),
            pl.BlockSpec((ka, d), lambda i: (0, 0)),
            pl.BlockSpec((kb, d), lambda i: (0, 0)),
        ],
        out_specs=pl.BlockSpec((tm, d), lambda i: (i, 0)),
        compiler_params=_cp(("parallel",)),
        name=name,
    )(h, a, b, wa, wb)


def _ffn_kernel(h_ref, g_ref, wg_ref, wu_ref, wd_ref, o_ref, hn_ref, acc_ref):
    f = pl.program_id(1)

    @pl.when(f == 0)
    def _():
        hn_ref[...] = _rms_rows(h_ref[...], g_ref[...]).astype(BF16)
        acc_ref[...] = jnp.zeros_like(acc_ref)

    hn = hn_ref[...]
    gate = _dot(hn, wg_ref[...])
    up = _dot(hn, wu_ref[...])
    act = (gate * jax.nn.sigmoid(gate)) * up
    acc_ref[...] += _dot(act.astype(BF16), wd_ref[...])

    @pl.when(f == pl.num_programs(1) - 1)
    def _():
        o_ref[...] = h_ref[...] + acc_ref[...]


def ffn_residual(h, g, wg, wu, wd, tf, name):
    n, d = h.shape
    dff = wg.shape[1]
    tm = ROW_TILE
    return pl.pallas_call(
        _ffn_kernel,
        out_shape=jax.ShapeDtypeStruct((n, d), F32),
        grid=(n // tm, dff // tf),
        in_specs=[
            pl.BlockSpec((tm, d), lambda i, f: (i, 0)),
            pl.BlockSpec((1, d), lambda i, f: (0, 0)),
            pl.BlockSpec((d, tf), lambda i, f: (0, f)),
            pl.BlockSpec((d, tf), lambda i, f: (0, f)),
            pl.BlockSpec((tf, d), lambda i, f: (f, 0)),
        ],
        out_specs=pl.BlockSpec((tm, d), lambda i, f: (i, 0)),
        scratch_shapes=[pltpu.VMEM((tm, d), BF16), pltpu.VMEM((tm, d), F32)],
        compiler_params=_cp(("parallel", "arbitrary")),
        name=name,
    )(h, g.reshape(1, d), wg, wu, wd)


def _ple_kernel(h_ref, g_ref, p_ref, wg_ref, wp_ref, o_ref):
    h = h_ref[...]
    hn = _rms_rows(h, g_ref[...]).astype(BF16)
    gate = jax.nn.sigmoid(_dot(hn, wg_ref[...]))
    proj = _dot(p_ref[...].astype(BF16), wp_ref[...])
    o_ref[...] = h + gate * proj


def ple_residual(h, g, p, wg, wp, name):
    n, d = h.shape
    dp = p.shape[1]
    tm = ROW_TILE
    return pl.pallas_call(
        _ple_kernel,
        out_shape=jax.ShapeDtypeStruct((n, d), F32),
        grid=(n // tm,),
        in_specs=[
            pl.BlockSpec((tm, d), lambda i: (i, 0)),
            pl.BlockSpec((1, d), lambda i: (0, 0)),
            pl.BlockSpec((tm, dp), lambda i: (i, 0)),
            pl.BlockSpec((d, d), lambda i: (0, 0)),
            pl.BlockSpec((dp, d), lambda i: (0, 0)),
        ],
        out_specs=pl.BlockSpec((tm, d), lambda i: (i, 0)),
        compiler_params=_cp(("parallel",)),
        name=name,
    )(h, g.reshape(1, d), p, wg, wp)


Z0_QA, Z0_VB, Z0_GOUT, Z0_KA, Z0_VA, Z0_QB, Z0_KB = 0, 1024, 2048, 3072, 3328, 3584, 4096
Z0_WIDTH = 4608


def _swa_prompt_kernel(sink_ref, q_ref, kvp_ref, kvc_ref, gq_ref, gk_ref, o_ref, ko_ref, vo_ref):
    n = pl.program_id(1)
    blk = q_ref.shape[0]
    grp = H_A // KVH_A
    kw = KVH_A * HD_A
    gq = gq_ref[...]
    gk = gk_ref[...]
    kv = jnp.concatenate([kvp_ref[...], kvc_ref[...]], axis=0)
    rows = lax.broadcasted_iota(jnp.int32, (grp * blk, 2 * blk), 0)
    cols = lax.broadcasted_iota(jnp.int32, (grp * blk, 2 * blk), 1)
    diff = (rows & (blk - 1)) - cols + blk
    lo = jnp.where(n == 0, blk, 0)
    mask = (diff >= 0) & (diff <= WINDOW) & (cols >= lo)
    for kh in range(KVH_A):
        kn = _rms_rows(kv[:, kh * HD_A:(kh + 1) * HD_A], gk)
        ko_ref[0, :, kh * HD_A:(kh + 1) * HD_A] = kn[blk:]
        v = kv[:, kw + kh * HD_A: kw + (kh + 1) * HD_A]
        qs = jnp.concatenate(
            [_rms_rows(q_ref[:, (kh * grp + g) * HD_A:(kh * grp + g + 1) * HD_A], gq) for g in range(grp)], axis=0)
        s = _dot_nt(qs.astype(BF16), kn.astype(BF16)) * (HD_A ** -0.5)
        s = jnp.where(mask, s, -jnp.inf)
        sink = jnp.concatenate(
            [jnp.full((blk, 1), sink_ref[kh * grp + g], F32) for g in range(grp)], axis=0)
        m = jnp.maximum(jnp.max(s, axis=-1, keepdims=True), sink)
        e = jnp.exp(s - m)
        den = jnp.sum(e, axis=-1, keepdims=True) + jnp.exp(sink - m)
        p = e / den
        o = _dot(p.astype(BF16), v.astype(BF16))
        for g in range(grp):
            o_ref[:, (kh * grp + g) * HD_A:(kh * grp + g + 1) * HD_A] = o[g * blk:(g + 1) * blk].astype(BF16)
    vo_ref[0] = kvc_ref[:, kw:]


def swa_prompt(z0, gq, gk, sinks, bsz, t):
    blk = 128
    nb = t // blk
    kvw = 2 * KVH_A * HD_A
    kv_col = Z0_KA // kvw
    grid_spec = pltpu.PrefetchScalarGridSpec(
        num_scalar_prefetch=1,
        grid=(bsz, nb),
        in_specs=[
            pl.BlockSpec((blk, H_A * HD_A), lambda b, n, s: (b * nb + n, 0)),
            pl.BlockSpec((blk, kvw), lambda b, n, s: (b * nb + jnp.maximum(n - 1, 0), kv_col)),
            pl.BlockSpec((blk, kvw), lambda b, n, s: (b * nb + n, kv_col)),
            pl.BlockSpec((1, HD_A), lambda b, n, s: (0, 0)),
            pl.BlockSpec((1, HD_A), lambda b, n, s: (0, 0)),
        ],
        out_specs=[
            pl.BlockSpec((blk, H_A * HD_A), lambda b, n, s: (b * nb + n, 0)),
            pl.BlockSpec((1, blk, KVH_A * HD_A), lambda b, n, s: (b, 0, 0)),
            pl.BlockSpec((1, blk, KVH_A * HD_A), lambda b, n, s: (b, 0, 0)),
        ],
    )
    return pl.pallas_call(
        _swa_prompt_kernel,
        out_shape=[
            jax.ShapeDtypeStruct((bsz * t, H_A * HD_A), BF16),
            jax.ShapeDtypeStruct((bsz, blk, KVH_A * HD_A), F32),
            jax.ShapeDtypeStruct((bsz, blk, KVH_A * HD_A), F32),
        ],
        grid_spec=grid_spec,
        compiler_params=_cp(("parallel", "arbitrary")),
        name="swa_prompt",
    )(sinks, z0, z0, z0, gq.reshape(1, HD_A), gk.reshape(1, HD_A))


def _swa_decode_kernel(sink_ref, q_ref, k_ref, v_ref, bk_ref, bv_ref, gq_ref, gk_ref, o_ref, ko_ref, vo_ref):
    grp = H_A // KVH_A
    buf_len = bk_ref.shape[1]
    gq = gq_ref[...]
    gk = gk_ref[...]
    bk = bk_ref[0]
    bv = bv_ref[0]
    k_new = k_ref[0]
    v_new = v_ref[0]
    kn_parts = []
    for kh in range(KVH_A):
        sl = slice(kh * HD_A, (kh + 1) * HD_A)
        kn = _rms_rows(k_new[:, sl], gk)
        kn_parts.append(kn)
        for g in range(grp):
            hd = kh * grp + g
            qh = _rms_rows(q_ref[0, :, hd * HD_A:(hd + 1) * HD_A], gq)
            s = jnp.sum(bk[:, sl] * qh, axis=-1, keepdims=True) * (HD_A ** -0.5)
            s_new = jnp.sum(kn * qh, axis=-1, keepdims=True) * (HD_A ** -0.5)
            sink = sink_ref[hd]
            m = jnp.maximum(jnp.maximum(jnp.max(s, axis=0, keepdims=True), s_new), sink)
            e = jnp.exp(s - m)
            e_new = jnp.exp(s_new - m)
            den = jnp.sum(e, axis=0, keepdims=True) + e_new + jnp.exp(sink - m)
            o = jnp.sum(e * bv[:, sl], axis=0, keepdims=True) + e_new * v_new[:, sl]
            o_ref[0, :, hd * HD_A:(hd + 1) * HD_A] = (o / den).astype(BF16)
    kn_row = jnp.concatenate(kn_parts, axis=-1)
    last = lax.broadcasted_iota(jnp.int32, bk.shape, 0) == buf_len - 1
    ko_ref[0] = jnp.where(last, kn_row, pltpu.roll(bk, buf_len - 1, 0))
    vo_ref[0] = jnp.where(last, v_new, pltpu.roll(bv, buf_len - 1, 0))


def swa_decode(q, k, v, buf_k, buf_v, gq, gk, sinks):
    bsz, buf_len, kw = buf_k.shape
    row = lambda w: pl.BlockSpec((1, 1, w), lambda b, s: (b, 0, 0))
    buf = pl.BlockSpec((1, buf_len, kw), lambda b, s: (b, 0, 0))
    vec = pl.BlockSpec((1, HD_A), lambda b, s: (0, 0))
    grid_spec = pltpu.PrefetchScalarGridSpec(
        num_scalar_prefetch=1,
        grid=(bsz,),
        in_specs=[row(H_A * HD_A), row(kw), row(kw), buf, buf, vec, vec],
        out_specs=[row(H_A * HD_A), buf, buf],
    )
    return pl.pallas_call(
        _swa_decode_kernel,
        out_shape=[
            jax.ShapeDtypeStruct((bsz, 1, H_A * HD_A), BF16),
            jax.ShapeDtypeStruct(buf_k.shape, F32),
            jax.ShapeDtypeStruct(buf_v.shape, F32),
        ],
        grid_spec=grid_spec,
        compiler_params=_cp(("parallel",)),
        name="swa_decode",
    )(sinks, q, k, v, buf_k, buf_v, gq.reshape(1, HD_A), gk.reshape(1, HD_A))


def _gla_logdec(glr, wg2_ref, bg2_ref):
    gl = _dot(glr.astype(BF16), wg2_ref[...]) + bg2_ref[...]
    return _log_sigmoid(gl) * (1.0 / GLA_GATE_NORM)


def _col(row8, i):
    return row8.T[:, i:i + 1]


def _gla_out(o, g_go, gout):
    return (_rms_rows(o, g_go) * (gout * jax.nn.sigmoid(gout))).astype(BF16)


def _gla_prompt_kernel(q_ref, k_ref, v_ref, gout_ref, glr_ref, wg2_ref, bg2_ref, ggo_ref, o_ref, so_ref, s_ref):
    c_idx = pl.program_id(1)
    tb = q_ref.shape[0]
    ck = GLA_CHUNK

    @pl.when(c_idx == 0)
    def _():
        s_ref[...] = jnp.zeros_like(s_ref)

    ld_all = _gla_logdec(glr_ref[...], wg2_ref, bg2_ref)
    ti = lax.broadcasted_iota(jnp.int32, (ck, ck), 0)
    si = lax.broadcasted_iota(jnp.int32, (ck, ck), 1)
    causal = si <= ti
    tri = jnp.where(causal, 1.0, 0.0).astype(BF16)
    g_go = ggo_ref[...]
    for c in range(tb // ck):
        rs = slice(c * ck, (c + 1) * ck)
        for h in range(H_B):
            ks = slice(h * DK_B, (h + 1) * DK_B)
            vs = slice(h * DV_B, (h + 1) * DV_B)
            hi, lo = _split_bf16(ld_all[rs, ks])
            b = _dot(tri, hi) + _dot(tri, lo)
            k = k_ref[rs, ks]
            v = v_ref[rs, vs].astype(BF16)
            qt = ((q_ref[rs, ks] * (DK_B ** -0.5)) * jnp.exp(b)).astype(BF16)
            kt = (k * jnp.exp(-b)).astype(BF16)
            att = jnp.where(causal, _dot_nt(qt, kt), 0.0)
            state = s_ref[h]
            o = _dot(att.astype(BF16), v) + _dot(qt, state.astype(BF16))
            b_last = b[ck - 1:ck, :]
            kd = (k * jnp.exp(b_last - b)).astype(BF16)
            d_state = _dot_tn(kd, v)
            decay = _col(jnp.broadcast_to(jnp.exp(b_last), (8, DK_B)), 0)
            s_ref[h] = decay * state + d_state
            o_ref[rs, vs] = _gla_out(o, g_go, gout_ref[rs, vs])
    so_ref[0] = s_ref[...]


def gla_prompt(z0, glr, wg2, bg2, g_go, bsz, t):
    tb = 256
    nt = t // tb
    qk_w = H_B * DK_B
    v_w = H_B * DV_B
    return pl.pallas_call(
        _gla_prompt_kernel,
        out_shape=[
            jax.ShapeDtypeStruct((bsz * t, v_w), BF16),
            jax.ShapeDtypeStruct((bsz, H_B, DK_B, DV_B), F32),
        ],
        grid=(bsz, nt),
        in_specs=[
            pl.BlockSpec((tb, qk_w), lambda b, c: (b * nt + c, Z0_QB // qk_w)),
            pl.BlockSpec((tb, qk_w), lambda b, c: (b * nt + c, Z0_KB // qk_w)),
            pl.BlockSpec((tb, v_w), lambda b, c: (b * nt + c, Z0_VB // v_w)),
            pl.BlockSpec((tb, v_w), lambda b, c: (b * nt + c, Z0_GOUT // v_w)),
            pl.BlockSpec((tb, 128), lambda b, c: (b * nt + c, 0)),
            pl.BlockSpec((128, qk_w), lambda b, c: (0, 0)),
            pl.BlockSpec((1, qk_w), lambda b, c: (0, 0)),
            pl.BlockSpec((1, DV_B), lambda b, c: (0, 0)),
        ],
        out_specs=[
            pl.BlockSpec((tb, v_w), lambda b, c: (b * nt + c, 0)),
            pl.BlockSpec((1, H_B, DK_B, DV_B), lambda b, c: (b, 0, 0, 0)),
        ],
        scratch_shapes=[pltpu.VMEM((H_B, DK_B, DV_B), F32)],
        compiler_params=_cp(("parallel", "arbitrary")),
        name="gla_prompt",
    )(z0, z0, z0, z0, glr, wg2, bg2.reshape(1, qk_w), g_go.reshape(1, DV_B))


def _gla_decode_kernel(q_ref, k_ref, v_ref, gout_ref, glr_ref, wg2_ref, bg2_ref, ggo_ref, s_ref, o_ref, so_ref):
    nb = q_ref.shape[0]
    ld = _gla_logdec(glr_ref[...], wg2_ref, bg2_ref)
    g_go = ggo_ref[...]
    for h in range(H_B):
        ks = slice(h * DK_B, (h + 1) * DK_B)
        vs = slice(h * DV_B, (h + 1) * DV_B)
        eg_t = jnp.exp(ld[:, ks]).T
        k_t = k_ref[:, ks].T
        q_t = (q_ref[:, ks] * (DK_B ** -0.5)).T
        for i in range(nb):
            state = eg_t[:, i:i + 1] * s_ref[i, h] + k_t[:, i:i + 1] * v_ref[i:i + 1, vs]
            so_ref[i, h] = state
            o = jnp.sum(q_t[:, i:i + 1] * state, axis=0, keepdims=True)
            o_ref[i:i + 1, vs] = _gla_out(o, g_go, gout_ref[i:i + 1, vs])


def gla_decode(q, k, v, gout, glr, wg2, bg2, g_go, state):
    bsz = q.shape[0]
    nb = 8
    qk_w = H_B * DK_B
    v_w = H_B * DV_B
    rows = lambda w: pl.BlockSpec((nb, w), lambda i: (i, 0))
    st = pl.BlockSpec((nb, H_B, DK_B, DV_B), lambda i: (i, 0, 0, 0))
    return pl.pallas_call(
        _gla_decode_kernel,
        out_shape=[
            jax.ShapeDtypeStruct((bsz, v_w), BF16),
            jax.ShapeDtypeStruct(state.shape, F32),
        ],
        grid=(bsz // nb,),
        in_specs=[
            rows(qk_w), rows(qk_w), rows(v_w), rows(v_w), rows(128),
            pl.BlockSpec((128, qk_w), lambda i: (0, 0)),
            pl.BlockSpec((1, qk_w), lambda i: (0, 0)),
            pl.BlockSpec((1, DV_B), lambda i: (0, 0)),
            st,
        ],
        out_specs=[rows(v_w), st],
        compiler_params=_cp(("parallel",)),
        name="gla_decode",
    )(q, k, v, gout, glr, wg2, bg2.reshape(1, qk_w), g_go.reshape(1, DV_B), state)


Z1_CX, Z1_CY, Z1_QD, Z1_KD, Z1_VD = 0, 1024, 2048, 3072, 3584


def _rg_gates(xc, wa_ref, ba_ref, wx_ref, bx_ref, lam_ref):
    xb = xc.astype(BF16)
    ra = jnp.concatenate(
        [_dot(xb[:, n * C_BLOCK:(n + 1) * C_BLOCK], wa_ref[n]) for n in range(C_BLOCKS)], axis=-1) + ba_ref[...]
    rx = jnp.concatenate(
        [_dot(xb[:, n * C_BLOCK:(n + 1) * C_BLOCK], wx_ref[n]) for n in range(C_BLOCKS)], axis=-1) + bx_ref[...]
    r_gate = jax.nn.sigmoid(ra)
    i_gate = jax.nn.sigmoid(rx)
    log_a = (-RG_C * r_gate) * _softplus(-lam_ref[...])
    a = jnp.exp(log_a)
    one_minus_a2 = -jnp.tanh(log_a) * (a * a + 1.0)
    u = jnp.sqrt(one_minus_a2) * (i_gate * xc)
    return a, u


def _shift_rows(x, d, fill):
    rows = lax.broadcasted_iota(jnp.int32, x.shape, 0)
    return jnp.where(rows < d, fill, pltpu.roll(x, d, 0))


def _rglru_prompt_kernel(cx_ref, cy_ref, cw_ref, cb_ref, wa_ref, ba_ref, wx_ref, bx_ref, lam_ref,
                         o_ref, tail_ref, hl_ref, prev_ref, hc_ref):
    t_idx = pl.program_id(1)
    tt = cx_ref.shape[0]

    @pl.when(t_idx == 0)
    def _():
        prev_ref[...] = jnp.zeros_like(prev_ref)
        hc_ref[...] = jnp.zeros_like(hc_ref)

    x = cx_ref[...]
    xp = jnp.concatenate([prev_ref[...], x], axis=0)
    xc = cb_ref[...] + cw_ref[CONV_W - 1:CONV_W, :] * x
    for j in range(CONV_W - 1):
        d = CONV_W - 1 - j
        xc = xc + cw_ref[j:j + 1, :] * pltpu.roll(xp, d, 0)[8:]
    prev_ref[...] = x[tt - 8:]
    tail_ref[0] = x[tt - 8:]

    a, u = _rg_gates(xc, wa_ref, ba_ref, wx_ref, bx_ref, lam_ref)
    d = 1
    while d < tt:
        u = a * _shift_rows(u, d, 0.0) + u
        a = a * _shift_rows(a, d, 1.0)
        d *= 2
    h = a * hc_ref[...] + u
    hc_ref[...] = h[tt - 1:]
    hl_ref[0] = h[tt - 1:]
    o_ref[...] = (h * jax.nn.gelu(cy_ref[...])).astype(BF16)


def rglru_prompt(z1, conv_w, conv_b, wa, ba, wx, bx, lam, bsz, t):
    tt = 256
    nt = t // tt
    c = C_WIDTH
    vec = pl.BlockSpec((1, c), lambda b, i: (0, 0))
    wblk = pl.BlockSpec((C_BLOCKS, C_BLOCK, C_BLOCK), lambda b, i: (0, 0, 0))
    return pl.pallas_call(
        _rglru_prompt_kernel,
        out_shape=[
            jax.ShapeDtypeStruct((bsz * t, c), BF16),
            jax.ShapeDtypeStruct((bsz, 8, c), F32),
            jax.ShapeDtypeStruct((bsz, 1, c), F32),
        ],
        grid=(bsz, nt),
        in_specs=[
            pl.BlockSpec((tt, c), lambda b, i: (b * nt + i, Z1_CX // c)),
            pl.BlockSpec((tt, c), lambda b, i: (b * nt + i, Z1_CY // c)),
            pl.BlockSpec((CONV_W, c), lambda b, i: (0, 0)),
            vec, wblk, vec, wblk, vec, vec,
        ],
        out_specs=[
            pl.BlockSpec((tt, c), lambda b, i: (b * nt + i, 0)),
            pl.BlockSpec((1, 8, c), lambda b, i: (b, 0, 0)),
            pl.BlockSpec((1, 1, c), lambda b, i: (b, 0, 0)),
        ],
        scratch_shapes=[pltpu.VMEM((8, c), F32), pltpu.VMEM((1, c), F32)],
        compiler_params=_cp(("parallel", "arbitrary")),
        name="rglru_prompt",
    )(z1, z1, conv_w, conv_b.reshape(1, c), wa, ba.reshape(1, c), wx, bx.reshape(1, c), lam.reshape(1, c))


def _rglru_decode_kernel(cx_ref, cy_ref, buf_ref, h0_ref, cw_ref, cb_ref, wa_ref, ba_ref, wx_ref, bx_ref, lam_ref,
                         o_ref, nbuf_ref, hl_ref):
    x = cx_ref[...]
    xc = cb_ref[...] + cw_ref[CONV_W - 1:CONV_W, :] * x
    for j in range(CONV_W - 1):
        xc = xc + cw_ref[j:j + 1, :] * buf_ref[j]
    for j in range(CONV_W - 2):
        nbuf_ref[j] = buf_ref[j + 1]
    nbuf_ref[CONV_W - 2] = x
    a, u = _rg_gates(xc, wa_ref, ba_ref, wx_ref, bx_ref, lam_ref)
    h = a * h0_ref[...] + u
    hl_ref[...] = h
    o_ref[...] = (h * jax.nn.gelu(cy_ref[...])).astype(BF16)


def rglru_decode(cx, cy, buf, h0, conv_w, conv_b, wa, ba, wx, bx, lam):
    bsz, c = cx.shape
    return pl.pallas_call(
        _rglru_decode_kernel,
        out_shape=[
            jax.ShapeDtypeStruct((bsz, c), BF16),
            jax.ShapeDtypeStruct(buf.shape, F32),
            jax.ShapeDtypeStruct((bsz, c), F32),
        ],
        compiler_params=pltpu.CompilerParams(vmem_limit_bytes=V7X_VMEM_LIMIT_BYTES),
        name="rglru_decode",
    )(cx, cy, buf, h0, conv_w, conv_b.reshape(1, c), wa, ba.reshape(1, c), wx, bx.reshape(1, c), lam.reshape(1, c))


def _strict_upper_ones(n):
    j = lax.broadcasted_iota(jnp.int32, (n, n), 0)
    s = lax.broadcasted_iota(jnp.int32, (n, n), 1)
    return jnp.where(j > s, 1.0, 0.0).astype(BF16)


def _sb_tile(z, later, surv, mask):
    sp = _softplus(z)
    l1m = -sp
    lsg = z - sp
    if mask is not None:
        l1m = jnp.where(mask, l1m, 0.0)
    hi, lo = _split_bf16(l1m)
    suffix = _dot(hi, later) + _dot(lo, later)
    w = jnp.exp(lsg + suffix + surv)
    if mask is not None:
        w = jnp.where(mask, w, 0.0)
    return w, jnp.sum(l1m, axis=-1, keepdims=True)


def _sb_prompt_kernel(bias_ref, q_ref, k_ref, v_ref, gq_ref, gk_ref, o_ref, ko_ref, kb_ref, vb_ref):
    kh = pl.program_id(1)
    i = pl.program_id(2)
    blk = q_ref.shape[0]
    grp = H_D // KVH_D

    @pl.when(i == 0)
    def _():
        kn = _rms_rows(k_ref[...], gk_ref[...])
        ko_ref[...] = kn
        kb_ref[...] = kn.astype(BF16)
        vb_ref[...] = v_ref[...].astype(BF16)

    gq = gq_ref[...]
    qs = jnp.concatenate(
        [_rms_rows(q_ref[:, g * HD_D:(g + 1) * HD_D], gq) for g in range(grp)], axis=0).astype(BF16)
    bias = jnp.concatenate(
        [jnp.full((blk, 1), bias_ref[kh * grp + g], F32) for g in range(grp)], axis=0)
    later = _strict_upper_ones(blk)
    scale = HD_D ** -0.5

    def tile(j, mask, acc, surv):
        off = pl.multiple_of(j * blk, blk)
        z = _dot_nt(qs, kb_ref[pl.ds(off, blk), :]) * scale + bias
        w, tot = _sb_tile(z, later, surv, mask)
        return acc + _dot(w.astype(BF16), vb_ref[pl.ds(off, blk), :]), surv + tot

    rows = lax.broadcasted_iota(jnp.int32, (grp * blk, blk), 0) & (blk - 1)
    cols = lax.broadcasted_iota(jnp.int32, (grp * blk, blk), 1)
    acc, surv = tile(i, cols < rows, jnp.zeros((grp * blk, HD_D), F32), jnp.zeros((grp * blk, 1), F32))

    def body(step, carry):
        return tile(i - 1 - step, None, *carry)

    acc, _ = lax.fori_loop(0, i, body, (acc, surv))
    for g in range(grp):
        o_ref[:, g * HD_D:(g + 1) * HD_D] = acc[g * blk:(g + 1) * blk].astype(BF16)


def sb_prompt(z1, gq, gk, bias, bsz, t):
    blk = 128
    nb = t // blk
    grp = H_D // KVH_D
    qw = grp * HD_D
    grid_spec = pltpu.PrefetchScalarGridSpec(
        num_scalar_prefetch=1,
        grid=(bsz, KVH_D, nb),
        in_specs=[
            pl.BlockSpec((blk, qw), lambda b, k, i, s: (b * nb + i, Z1_QD // qw + k)),
            pl.BlockSpec((t, HD_D), lambda b, k, i, s: (b, Z1_KD // HD_D + k)),
            pl.BlockSpec((t, HD_D), lambda b, k, i, s: (b, Z1_VD // HD_D + k)),
            pl.BlockSpec((1, HD_D), lambda b, k, i, s: (0, 0)),
            pl.BlockSpec((1, HD_D), lambda b, k, i, s: (0, 0)),
        ],
        out_specs=[
            pl.BlockSpec((blk, qw), lambda b, k, i, s: (b * nb + i, k)),
            pl.BlockSpec((t, HD_D), lambda b, k, i, s: (b, k)),
        ],
        scratch_shapes=[pltpu.VMEM((t, HD_D), BF16), pltpu.VMEM((t, HD_D), BF16)],
    )
    return pl.pallas_call(
        _sb_prompt_kernel,
        out_shape=[
            jax.ShapeDtypeStruct((bsz * t, H_D * HD_D), BF16),
            jax.ShapeDtypeStruct((bsz * t, KVH_D * HD_D), F32),
        ],
        grid_spec=grid_spec,
        compiler_params=_cp(("parallel", "parallel", "arbitrary")),
        name="sb_prompt",
    )(bias, z1, z1, z1, gq.reshape(1, HD_D), gk.reshape(1, HD_D))


def _sb_decode_kernel(pt_ref, bias_ref, q_ref, k_ref, v_ref, gq_ref, gk_ref, *rest):
    npg = SB_PAGES_PER_STEP
    kp_refs = rest[:npg]
    vp_refs = rest[npg:2 * npg]
    o_ref, ko_ref, qb_ref, acc_ref, surv_ref = rest[2 * npg:]
    s_idx = pl.program_id(1)
    grp = H_D // KVH_D
    kvw = KVH_D * HD_D
    head = lax.broadcasted_iota(jnp.int32, (H_D, kvw), 0)
    lane_blk = lax.broadcasted_iota(jnp.int32, (H_D, kvw), 1) // HD_D
    own = lane_blk == head // grp
    bias = jnp.concatenate([jnp.full((1, 1), bias_ref[h], F32) for h in range(H_D)], axis=0)
    scale = HD_D ** -0.5

    @pl.when(s_idx == 0)
    def _():
        qn = _rms_rows(q_ref[0], gq_ref[...])
        qblk = jnp.where(own, jnp.concatenate([qn] * KVH_D, axis=-1), 0.0)
        qb_ref[...] = qblk.astype(BF16)
        kn = _rms_rows(k_ref[0], gk_ref[...])
        ko_ref[0] = kn
        kn_row = jnp.concatenate([kn[k:k + 1] for k in range(KVH_D)], axis=-1)
        v_row = jnp.concatenate([v_ref[0, k:k + 1] for k in range(KVH_D)], axis=-1)
        z0 = jnp.sum(qblk * kn_row, axis=-1, keepdims=True) * scale + bias
        visible = jnp.zeros((H_D, 1), jnp.int32) < jnp.zeros((H_D, 1), jnp.int32)
        sp0 = _softplus(z0)
        l1m0 = jnp.where(visible, -sp0, 0.0)
        w0 = jnp.where(visible, jnp.exp(z0 - sp0), 0.0)
        acc_ref[...] = w0 * v_row
        surv_ref[...] = l1m0

    qblk = qb_ref[...]
    later = _strict_upper_ones(PAGE)
    acc = acc_ref[...]
    surv = surv_ref[...]
    for p in range(npg):
        z = _dot_nt(qblk, kp_refs[p][...].astype(BF16)) * scale + bias
        w, tot = _sb_tile(z, later, surv, None)
        acc = acc + _dot(w.astype(BF16), vp_refs[p][...].astype(BF16))
        surv = surv + tot
    acc_ref[...] = acc
    surv_ref[...] = surv

    @pl.when(s_idx == pl.num_programs(1) - 1)
    def _():
        sel = jnp.where(own, acc, 0.0)
        out = sel[:, 0:HD_D]
        for k in range(1, KVH_D):
            out = out + sel[:, k * HD_D:(k + 1) * HD_D]
        o_ref[0] = out.astype(BF16)


def sb_decode(q, k, v, gq, gk, bias, cache_k, cache_v, page_table):
    bsz = q.shape[0]
    n_pages = page_table.shape[1]
    npg = SB_PAGES_PER_STEP
    kvw = KVH_D * HD_D

    def page_spec(p):
        return pl.BlockSpec(
            (None, PAGE, kvw),
            lambda b, s, pt, bias_: (pt[b * n_pages + (n_pages - 1 - s * npg - p)], 0, 0))

    grid_spec = pltpu.PrefetchScalarGridSpec(
        num_scalar_prefetch=2,
        grid=(bsz, n_pages // npg),
        in_specs=[
            pl.BlockSpec((1, H_D, HD_D), lambda b, s, pt, bias_: (b, 0, 0)),
            pl.BlockSpec((1, KVH_D, HD_D), lambda b, s, pt, bias_: (b, 0, 0)),
            pl.BlockSpec((1, KVH_D, HD_D), lambda b, s, pt, bias_: (b, 0, 0)),
            pl.BlockSpec((1, HD_D), lambda b, s, pt, bias_: (0, 0)),
            pl.BlockSpec((1, HD_D), lambda b, s, pt, bias_: (0, 0)),
        ] + [page_spec(p) for p in range(npg)] + [page_spec(p) for p in range(npg)],
        out_specs=[
            pl.BlockSpec((1, H_D, HD_D), lambda b, s, pt, bias_: (b, 0, 0)),
            pl.BlockSpec((1, KVH_D, HD_D), lambda b, s, pt, bias_: (b, 0, 0)),
        ],
        scratch_shapes=[
            pltpu.VMEM((H_D, kvw), BF16),
            pltpu.VMEM((H_D, kvw), F32),
            pltpu.VMEM((H_D, 1), F32),
        ],
    )
    return pl.pallas_call(
        _sb_decode_kernel,
        out_shape=[
            jax.ShapeDtypeStruct((bsz, H_D, HD_D), BF16),
            jax.ShapeDtypeStruct((bsz, KVH_D, HD_D), F32),
        ],
        grid_spec=grid_spec,
        compiler_params=_cp(("parallel", "arbitrary")),
        name="sb_decode",
    )(page_table.reshape(-1), bias, q, k, v, gq.reshape(1, HD_D), gk.reshape(1, HD_D),
      *([cache_k] * npg), *([cache_v] * npg))


def kernel(x_prompt, x_sample, state_swa_k, state_swa_v, state_gla, state_conv, state_lru, cache_sb_k, cache_sb_v, page_table, p_prompt, p_sample, g_mix, g_ffn, w_ffn_gate, w_ffn_up, w_ffn_down, g_ple, w_ple_gate, w_ple_proj, w_in_even, g_qnorm_a, g_knorm_a, sinks_a, w_gla_gate2, b_gla_gate2, g_gla_out, w_out_even, w_in_odd, conv_w, conv_b, w_rg_a, b_rg_a, w_rg_x, b_rg_x, lru_lambda, g_qnorm_d, g_knorm_d, sb_bias, w_out_odd):
    bsz, t, d = x_prompt.shape
    dbs = x_sample.shape[0]
    n_p = bsz * t
    depth = g_mix.shape[0]
    ple = p_prompt.shape[-1]

    h = jnp.concatenate([x_prompt.reshape(n_p, d), x_sample.reshape(dbs, d)], axis=0)
    p_all = jnp.concatenate([p_prompt.reshape(depth, n_p, ple), p_sample.reshape(depth, dbs, ple)], axis=1)

    def dense_tail(h, i):
        h = ffn_residual(h, g_ffn[i], w_ffn_gate[i].astype(BF16), w_ffn_up[i].astype(BF16),
                         w_ffn_down[i].astype(BF16), 512, f"ffn_{i}")
        return ple_residual(h, g_ple[i], p_all[i], w_ple_gate[i].astype(BF16), w_ple_proj[i].astype(BF16), f"ple_{i}")

    w = w_in_even[0]
    qa, ka, va, qb, kb, vb, glr_w, gout = jnp.split(
        w, [1024, 1280, 1536, 2048, 2560, 3584, 3600], axis=1)
    w0 = jnp.concatenate([qa, vb, gout, ka, va, qb, kb], axis=1).astype(BF16)
    w0_glr = jnp.pad(glr_w, ((0, 0), (0, 128 - GLA_RANK))).astype(BF16)
    z0 = norm_matmul(h, g_mix[0], w0, 512, "in_proj_even")
    glr = norm_matmul(h, g_mix[0], w0_glr, 128, "in_proj_glr")
    wg2 = jnp.pad(w_gla_gate2[0], ((0, 128 - GLA_RANK), (0, 0))).astype(BF16)

    oa_p, swa_k_p, swa_v_p = swa_prompt(z0, g_qnorm_a[0], g_knorm_a[0], sinks_a[0], bsz, t)
    ob_p, gla_p = gla_prompt(z0, glr, wg2, b_gla_gate2[0], g_gla_out[0], bsz, t)

    zd = z0[n_p:]
    kw_a = KVH_A * HD_A
    oa_d, swa_k_d, swa_v_d = swa_decode(
        zd[:, Z0_QA:Z0_QA + 1024].reshape(dbs, 1, 1024),
        zd[:, Z0_KA:Z0_KA + kw_a].reshape(dbs, 1, kw_a),
        zd[:, Z0_VA:Z0_VA + kw_a].reshape(dbs, 1, kw_a),
        state_swa_k[0].reshape(dbs, -1, kw_a), state_swa_v[0].reshape(dbs, -1, kw_a),
        g_qnorm_a[0], g_knorm_a[0], sinks_a[0])
    ob_d, gla_d = gla_decode(
        zd[:, Z0_QB:Z0_QB + 512], zd[:, Z0_KB:Z0_KB + 512], zd[:, Z0_VB:Z0_VB + 1024],
        zd[:, Z0_GOUT:Z0_GOUT + 1024], glr[n_p:], wg2, b_gla_gate2[0], g_gla_out[0], state_gla[0])

    oa = jnp.concatenate([oa_p, oa_d.reshape(dbs, 1024)], axis=0)
    ob = jnp.concatenate([ob_p, ob_d], axis=0)
    wo = w_out_even[0].astype(BF16)
    h = proj_residual(h, oa, ob, wo[:1024], wo[1024:], "out_proj_even")
    h = dense_tail(h, 0)

    z1 = norm_matmul(h, g_mix[1], w_in_odd[0].astype(BF16), 512, "in_proj_odd")
    wa = w_rg_a[0].astype(BF16)
    wx = w_rg_x[0].astype(BF16)
    oc_p, conv_tail, lru_p = rglru_prompt(z1, conv_w[0], conv_b[0], wa, b_rg_a[0], wx, b_rg_x[0], lru_lambda[0], bsz, t)
    od_p, sb_k_p = sb_prompt(z1, g_qnorm_d[0], g_knorm_d[0], sb_bias[0], bsz, t)

    zd = z1[n_p:]
    oc_d, conv_d, lru_d = rglru_decode(
        zd[:, Z1_CX:Z1_CX + 1024], zd[:, Z1_CY:Z1_CY + 1024],
        jnp.swapaxes(state_conv[0], 0, 1), state_lru[0],
        conv_w[0], conv_b[0], wa, b_rg_a[0], wx, b_rg_x[0], lru_lambda[0])
    kvw = KVH_D * HD_D
    n_phys = cache_sb_k.shape[1]
    od_d, sb_k_d = sb_decode(
        zd[:, Z1_QD:Z1_QD + 1024].reshape(dbs, H_D, HD_D),
        zd[:, Z1_KD:Z1_KD + kvw].reshape(dbs, KVH_D, HD_D),
        zd[:, Z1_VD:Z1_VD + kvw].reshape(dbs, KVH_D, HD_D),
        g_qnorm_d[0], g_knorm_d[0], sb_bias[0],
        cache_sb_k[0].reshape(n_phys, PAGE, kvw), cache_sb_v[0].reshape(n_phys, PAGE, kvw), page_table)

    oc = jnp.concatenate([oc_p, oc_d], axis=0)
    od = jnp.concatenate([od_p, od_d.reshape(dbs, 1024)], axis=0)
    wo = w_out_odd[0].astype(BF16)
    h = proj_residual(h, oc, od, wo[:1024], wo[1024:], "out_proj_odd")
    h = dense_tail(h, 1)

    y_prompt = h[:n_p].reshape(bsz, t, d)
    y_sample = h[n_p:].reshape(dbs, 1, d)
    n_keep = min(WINDOW, t)
    return (
        y_prompt, y_sample,
        swa_k_p.reshape(1, bsz, n_keep, KVH_A, HD_A), swa_v_p.reshape(1, bsz, n_keep, KVH_A, HD_A),
        gla_p[None],
        conv_tail[:, 8 - (CONV_W - 1):][None], lru_p.reshape(1, bsz, C_WIDTH),
        sb_k_p.reshape(1, bsz, t, KVH_D, HD_D), z1[:n_p, Z1_VD:].reshape(1, bsz, t, KVH_D, HD_D),
        swa_k_d.reshape(1, dbs, -1, KVH_A, HD_A), swa_v_d.reshape(1, dbs, -1, KVH_A, HD_A),
        gla_d[None],
        jnp.swapaxes(conv_d, 0, 1)[None], lru_d[None],
        sb_k_d.reshape(1, dbs, 1, KVH_D, HD_D), zd[:, Z1_VD:].reshape(1, dbs, 1, KVH_D, HD_D),
    )
```

```python
import functools

import jax
import jax.numpy as jnp
from jax import lax
from jax.experimental import pallas as pl
from jax.experimental.pallas import tpu as pltpu

F32 = jnp.float32
BF16 = jnp.bfloat16
EPS = 1e-6

V7X_VMEM_LIMIT_BYTES = 56 * 1024 * 1024

H_A, KVH_A, HD_A, WINDOW = 16, 4, 64, 128
H_B, DK_B, DV_B, GLA_RANK, GLA_GATE_NORM, GLA_CHUNK = 4, 128, 256, 16, 16.0, 64
C_WIDTH, C_BLOCKS, CONV_W, RG_C = 1024, 8, 4, 8.0
C_BLOCK = C_WIDTH // C_BLOCKS
H_D, KVH_D, HD_D = 8, 4, 128
PAGE = 128

ROW_TILE = 640
SB_PROMPT_TILE = 256
SB_PAGES_PER_STEP = 16


def _cp(sem):
    return pltpu.CompilerParams(dimension_semantics=sem, vmem_limit_bytes=V7X_VMEM_LIMIT_BYTES)


def _rms_rows(x, g):
    r = lax.rsqrt(jnp.mean(x * x, axis=-1, keepdims=True) + EPS)
    return (x * r) * g


def _softplus(x):
    return jnp.maximum(x, 0.0) + jnp.log(1.0 + jnp.exp(-jnp.abs(x)))


def _log_sigmoid(x):
    return jnp.minimum(x, 0.0) - jnp.log(1.0 + jnp.exp(-jnp.abs(x)))


def _split_bf16(x):
    hi = x.astype(BF16)
    lo = (x - hi.astype(F32)).astype(BF16)
    return hi, lo


def _dot(a, b):
    return jnp.dot(a, b, preferred_element_type=F32)


def _dot_nt(a, b):
    return lax.dot_general(a, b, (((1,), (1,)), ((), ())), preferred_element_type=F32)


def _dot_tn(a, b):
    return lax.dot_general(a, b, (((0,), (0,)), ((), ())), preferred_element_type=F32)


def _norm_matmul_kernel(x_ref, g_ref, w_ref, o_ref, xn_ref):
    @pl.when(pl.program_id(1) == 0)
    def _():
        xn_ref[...] = _rms_rows(x_ref[...], g_ref[...]).astype(BF16)

    o_ref[...] = _dot(xn_ref[...], w_ref[...]).astype(o_ref.dtype)


def norm_matmul(x, g, w, tn, name):
    n, d = x.shape
    nout = w.shape[1]
    tm = ROW_TILE
    return pl.pallas_call(
        _norm_matmul_kernel,
        out_shape=jax.ShapeDtypeStruct((n, nout), F32),
        grid=(n // tm, nout // tn),
        in_specs=[
            pl.BlockSpec((tm, d), lambda i, j: (i, 0)),
            pl.BlockSpec((1, d), lambda i, j: (0, 0)),
            pl.BlockSpec((d, tn), lambda i, j: (0, j)),
        ],
        out_specs=pl.BlockSpec((tm, tn), lambda i, j: (i, j)),
        scratch_shapes=[pltpu.VMEM((tm, d), BF16)],
        compiler_params=_cp(("parallel", "arbitrary")),
        name=name,
    )(x, g.reshape(1, d), w)


def _proj_res_kernel(h_ref, a_ref, b_ref, wa_ref, wb_ref, o_ref):
    acc = _dot(a_ref[...], wa_ref[...])
    acc = acc + _dot(b_ref[...], wb_ref[...])
    o_ref[...] = h_ref[...] + acc


def proj_residual(h, a, b, wa, wb, name):
    n, d = h.shape
    ka, kb = a.shape[1], b.shape[1]
    tm = ROW_TILE
    return pl.pallas_call(
        _proj_res_kernel,
        out_shape=jax.ShapeDtypeStruct((n, d), F32),
        grid=(n // tm,),
        in_specs=[
            pl.BlockSpec((tm, d), lambda i: (i, 0)),
            pl.BlockSpec((tm, ka), lambda i: (i, 0)),
            pl.BlockSpec((tm, kb), lambda i: (i, 0)),
            pl.BlockSpec((ka, d), lambda i: (0, 0)),
            pl.BlockSpec((kb, d), lambda i: (0, 0)),
        ],
        out_specs=pl.BlockSpec((tm, d), lambda i: (i, 0)),
        compiler_params=_cp(("parallel",)),
        name=name,
    )(h, a, b, wa, wb)


def _ffn_kernel(h_ref, g_ref, wg_ref, wu_ref, wd_ref, o_ref, hn_ref, acc_ref):
    f = pl.program_id(1)

    @pl.when(f == 0)
    def _():
        hn_ref[...] = _rms_rows(h_ref[...], g_ref[...]).astype(BF16)
        acc_ref[...] = jnp.zeros_like(acc_ref)

    hn = hn_ref[...]
    gate = _dot(hn, wg_ref[...])
    up = _dot(hn, wu_ref[...])
    act = (gate * jax.nn.sigmoid(gate)) * up
    acc_ref[...] += _dot(act.astype(BF16), wd_ref[...])

    @pl.when(f == pl.num_programs(1) - 1)
    def _():
        o_ref[...] = h_ref[...] + acc_ref[...]


def ffn_residual(h, g, wg, wu, wd, tf, name):
    n, d = h.shape
    dff = wg.shape[1]
    tm = ROW_TILE
    return pl.pallas_call(
        _ffn_kernel,
        out_shape=jax.ShapeDtypeStruct((n, d), F32),
        grid=(n // tm, dff // tf),
        in_specs=[
            pl.BlockSpec((tm, d), lambda i, f: (i, 0)),
            pl.BlockSpec((1, d), lambda i, f: (0, 0)),
            pl.BlockSpec((d, tf), lambda i, f: (0, f)),
            pl.BlockSpec((d, tf), lambda i, f: (0, f)),
            pl.BlockSpec((tf, d), lambda i, f: (f, 0)),
        ],
        out_specs=pl.BlockSpec((tm, d), lambda i, f: (i, 0)),
        scratch_shapes=[pltpu.VMEM((tm, d), BF16), pltpu.VMEM((tm, d), F32)],
        compiler_params=_cp(("parallel", "arbitrary")),
        name=name,
    )(h, g.reshape(1, d), wg, wu, wd)


def _ple_kernel(h_ref, g_ref, p_ref, wg_ref, wp_ref, o_ref):
    h = h_ref[...]
    hn = _rms_rows(h, g_ref[...]).astype(BF16)
    gate = jax.nn.sigmoid(_dot(hn, wg_ref[...]))
    proj = _dot(p_ref[...].astype(BF16), wp_ref[...])
    o_ref[...] = h + gate * proj


def ple_residual(h, g, p, wg, wp, name):
    n, d = h.shape
    dp = p.shape[1]
    tm = ROW_TILE
    return pl.pallas_call(
        _ple_kernel,
        out_shape=jax.ShapeDtypeStruct((n, d), F32),
        grid=(n // tm,),
        in_specs=[
            pl.BlockSpec((tm, d), lambda i: (i, 0)),
            pl.BlockSpec((1, d), lambda i: (0, 0)),
            pl.BlockSpec((tm, dp), lambda i: (i, 0)),
            pl.BlockSpec((d, d), lambda i: (0, 0)),
            pl.BlockSpec((dp, d), lambda i: (0, 0)),
        ],
        out_specs=pl.BlockSpec((tm, d), lambda i: (i, 0)),
        compiler_params=_cp(("parallel",)),
        name=name,
    )(h, g.reshape(1, d), p, wg, wp)


Z0_QA, Z0_VB, Z0_GOUT, Z0_KA, Z0_VA, Z0_QB, Z0_KB = 0, 1024, 2048, 3072, 3328, 3584, 4096
Z0_WIDTH = 4608


def _swa_prompt_kernel(sink_ref, q_ref, kvp_ref, kvc_ref, gq_ref, gk_ref, o_ref, ko_ref, vo_ref):
    n = pl.program_id(1)
    blk = q_ref.shape[0]
    grp = H_A // KVH_A
    kw = KVH_A * HD_A
    gq = gq_ref[...]
    gk = gk_ref[...]
    kv = jnp.concatenate([kvp_ref[...], kvc_ref[...]], axis=0)
    rows = lax.broadcasted_iota(jnp.int32, (grp * blk, 2 * blk), 0)
    cols = lax.broadcasted_iota(jnp.int32, (grp * blk, 2 * blk), 1)
    diff = (rows & (blk - 1)) - cols + blk
    lo = jnp.where(n == 0, blk, 0)
    mask = (diff >= 0) & (diff <= WINDOW) & (cols >= lo)
    for kh in range(KVH_A):
        kn = _rms_rows(kv[:, kh * HD_A:(kh + 1) * HD_A], gk)
        ko_ref[0, :, kh * HD_A:(kh + 1) * HD_A] = kn[blk:]
        v = kv[:, kw + kh * HD_A: kw + (kh + 1) * HD_A]
        qs = jnp.concatenate(
            [_rms_rows(q_ref[:, (kh * grp + g) * HD_A:(kh * grp + g + 1) * HD_A], gq) for g in range(grp)], axis=0)
        s = _dot_nt(qs.astype(BF16), kn.astype(BF16)) * (HD_A ** -0.5)
        s = jnp.where(mask, s, -jnp.inf)
        sink = jnp.concatenate(
            [jnp.full((blk, 1), sink_ref[kh * grp + g], F32) for g in range(grp)], axis=0)
        m = jnp.maximum(jnp.max(s, axis=-1, keepdims=True), sink)
        e = jnp.exp(s - m)
        den = jnp.sum(e, axis=-1, keepdims=True) + jnp.exp(sink - m)
        p = e / den
        o = _dot(p.astype(BF16), v.astype(BF16))
        for g in range(grp):
            o_ref[:, (kh * grp + g) * HD_A:(kh * grp + g + 1) * HD_A] = o[g * blk:(g + 1) * blk].astype(BF16)
    vo_ref[0] = kvc_ref[:, kw:]


def swa_prompt(z0, gq, gk, sinks, bsz, t):
    blk = 128
    nb = t // blk
    kvw = 2 * KVH_A * HD_A
    kv_col = Z0_KA // kvw
    grid_spec = pltpu.PrefetchScalarGridSpec(
        num_scalar_prefetch=1,
        grid=(bsz, nb),
        in_specs=[
            pl.BlockSpec((blk, H_A * HD_A), lambda b, n, s: (b * nb + n, 0)),
            pl.BlockSpec((blk, kvw), lambda b, n, s: (b * nb + jnp.maximum(n - 1, 0), kv_col)),
            pl.BlockSpec((blk, kvw), lambda b, n, s: (b * nb + n, kv_col)),
            pl.BlockSpec((1, HD_A), lambda b, n, s: (0, 0)),
            pl.BlockSpec((1, HD_A), lambda b, n, s: (0, 0)),
        ],
        out_specs=[
            pl.BlockSpec((blk, H_A * HD_A), lambda b, n, s: (b * nb + n, 0)),
            pl.BlockSpec((1, blk, KVH_A * HD_A), lambda b, n, s: (b, 0, 0)),
            pl.BlockSpec((1, blk, KVH_A * HD_A), lambda b, n, s: (b, 0, 0)),
        ],
    )
    return pl.pallas_call(
        _swa_prompt_kernel,
        out_shape=[
            jax.ShapeDtypeStruct((bsz * t, H_A * HD_A), BF16),
            jax.ShapeDtypeStruct((bsz, blk, KVH_A * HD_A), F32),
            jax.ShapeDtypeStruct((bsz, blk, KVH_A * HD_A), F32),
        ],
        grid_spec=grid_spec,
        compiler_params=_cp(("parallel", "arbitrary")),
        name="swa_prompt",
    )(sinks, z0, z0, z0, gq.reshape(1, HD_A), gk.reshape(1, HD_A))


def _swa_decode_kernel(q_ref, k_ref, v_ref, kt_ref, vt_ref, gq_ref, gk_ref, sink_ref, o_ref, kto_ref, vto_ref):
    nb = q_ref.shape[0]
    grp = H_A // KVH_A
    kw = KVH_A * HD_A
    buf_len = kt_ref.shape[2]
    gq = gq_ref[...]
    gk = gk_ref[...]
    sink = sink_ref[...]
    head = lax.broadcasted_iota(jnp.int32, (H_A, kw), 0)
    lane_blk = lax.broadcasted_iota(jnp.int32, (H_A, kw), 1) // HD_A
    own = lane_blk == head // grp
    last = lax.broadcasted_iota(jnp.int32, (kw, buf_len), 1) == buf_len - 1
    kn_rows = jnp.concatenate(
        [_rms_rows(k_ref[:, kh * HD_A:(kh + 1) * HD_A], gk) for kh in range(KVH_A)], axis=-1)
    v_rows = v_ref[...]
    kn_cols = kn_rows.T
    v_cols = v_rows.T
    scale = HD_A ** -0.5
    for i in range(nb):
        qn = _rms_rows(q_ref[i], gq)
        qblk = jnp.where(own, jnp.concatenate([qn] * KVH_A, axis=-1), 0.0)
        kt = kt_ref[i]
        vt = vt_ref[i]
        s = _dot(qblk.astype(BF16), kt.astype(BF16)) * scale
        s_new = jnp.sum(qblk * kn_rows[i:i + 1], axis=-1, keepdims=True) * scale
        m = jnp.maximum(jnp.maximum(jnp.max(s, axis=-1, keepdims=True), s_new), sink)
        e = jnp.exp(s - m)
        e_new = jnp.exp(s_new - m)
        den = jnp.sum(e, axis=-1, keepdims=True) + e_new + jnp.exp(sink - m)
        o_all = _dot_nt(e.astype(BF16), vt.astype(BF16)) + e_new * v_rows[i:i + 1]
        o_all = jnp.where(own, o_all, 0.0)
        o = o_all[:, 0:HD_A]
        for kh in range(1, KVH_A):
            o = o + o_all[:, kh * HD_A:(kh + 1) * HD_A]
        o_ref[i] = (o / den).astype(BF16)
        kto_ref[i] = jnp.where(last, kn_cols[:, i:i + 1], pltpu.roll(kt, buf_len - 1, 1))
        vto_ref[i] = jnp.where(last, v_cols[:, i:i + 1], pltpu.roll(vt, buf_len - 1, 1))


def swa_decode(q, k, v, kt, vt, gq, gk, sinks):
    bsz, kw, buf_len = kt.shape
    nb = 8
    rows = pl.BlockSpec((nb, kw), lambda i: (i, 0))
    heads = pl.BlockSpec((nb, H_A, HD_A), lambda i: (i, 0, 0))
    buf = pl.BlockSpec((nb, kw, buf_len), lambda i: (i, 0, 0))
    vec = pl.BlockSpec((1, HD_A), lambda i: (0, 0))
    return pl.pallas_call(
        _swa_decode_kernel,
        out_shape=[
            jax.ShapeDtypeStruct((bsz, H_A, HD_A), BF16),
            jax.ShapeDtypeStruct(kt.shape, F32),
            jax.ShapeDtypeStruct(vt.shape, F32),
        ],
        grid=(bsz // nb,),
        in_specs=[heads, rows, rows, buf, buf, vec, vec, pl.BlockSpec((H_A, 1), lambda i: (0, 0))],
        out_specs=[heads, buf, buf],
        compiler_params=_cp(("parallel",)),
        name="swa_decode",
    )(q, k, v, kt, vt, gq.reshape(1, HD_A), gk.reshape(1, HD_A), sinks.reshape(H_A, 1))


def _gla_logdec(glr, wg2_ref, bg2_ref):
    gl = _dot(glr.astype(BF16), wg2_ref[...]) + bg2_ref[...]
    return _log_sigmoid(gl) * (1.0 / GLA_GATE_NORM)


def _col(row8, i):
    return row8.T[:, i:i + 1]


def _gla_out(o, g_go, gout):
    return (_rms_rows(o, g_go) * (gout * jax.nn.sigmoid(gout))).astype(BF16)


def _gla_prompt_kernel(q_ref, k_ref, v_ref, gout_ref, glr_ref, wg2_ref, bg2_ref, ggo_ref, o_ref, so_ref, s_ref):
    c_idx = pl.program_id(1)
    tb = q_ref.shape[0]
    ck = GLA_CHUNK

    @pl.when(c_idx == 0)
    def _():
        s_ref[...] = jnp.zeros_like(s_ref)

    ld_all = _gla_logdec(glr_ref[...], wg2_ref, bg2_ref)
    ti = lax.broadcasted_iota(jnp.int32, (ck, ck), 0)
    si = lax.broadcasted_iota(jnp.int32, (ck, ck), 1)
    causal = si <= ti
    tri = jnp.where(causal, 1.0, 0.0).astype(BF16)
    g_go = ggo_ref[...]
    for c in range(tb // ck):
        rs = slice(c * ck, (c + 1) * ck)
        for h in range(H_B):
            ks = slice(h * DK_B, (h + 1) * DK_B)
            vs = slice(h * DV_B, (h + 1) * DV_B)
            hi, lo = _split_bf16(ld_all[rs, ks])
            b = _dot(tri, hi) + _dot(tri, lo)
            k = k_ref[rs, ks]
            v = v_ref[rs, vs].astype(BF16)
            qt = ((q_ref[rs, ks] * (DK_B ** -0.5)) * jnp.exp(b)).astype(BF16)
            kt = (k * jnp.exp(-b)).astype(BF16)
            att = jnp.where(causal, _dot_nt(qt, kt), 0.0)
            state = s_ref[h]
            o = _dot(att.astype(BF16), v) + _dot(qt, state.astype(BF16))
            b_last = b[ck - 1:ck, :]
            kd = (k * jnp.exp(b_last - b)).astype(BF16)
            d_state = _dot_tn(kd, v)
            decay = _col(jnp.broadcast_to(jnp.exp(b_last), (8, DK_B)), 0)
            s_ref[h] = decay * state + d_state
            o_ref[rs, vs] = _gla_out(o, g_go, gout_ref[rs, vs])
    so_ref[0] = s_ref[...]


def gla_prompt(z0, glr, wg2, bg2, g_go, bsz, t):
    tb = 256
    nt = t // tb
    qk_w = H_B * DK_B
    v_w = H_B * DV_B
    return pl.pallas_call(
        _gla_prompt_kernel,
        out_shape=[
            jax.ShapeDtypeStruct((bsz * t, v_w), BF16),
            jax.ShapeDtypeStruct((bsz, H_B, DK_B, DV_B), F32),
        ],
        grid=(bsz, nt),
        in_specs=[
            pl.BlockSpec((tb, qk_w), lambda b, c: (b * nt + c, Z0_QB // qk_w)),
            pl.BlockSpec((tb, qk_w), lambda b, c: (b * nt + c, Z0_KB // qk_w)),
            pl.BlockSpec((tb, v_w), lambda b, c: (b * nt + c, Z0_VB // v_w)),
            pl.BlockSpec((tb, v_w), lambda b, c: (b * nt + c, Z0_GOUT // v_w)),
            pl.BlockSpec((tb, 128), lambda b, c: (b * nt + c, 0)),
            pl.BlockSpec((128, qk_w), lambda b, c: (0, 0)),
            pl.BlockSpec((1, qk_w), lambda b, c: (0, 0)),
            pl.BlockSpec((1, DV_B), lambda b, c: (0, 0)),
        ],
        out_specs=[
            pl.BlockSpec((tb, v_w), lambda b, c: (b * nt + c, 0)),
            pl.BlockSpec((1, H_B, DK_B, DV_B), lambda b, c: (b, 0, 0, 0)),
        ],
        scratch_shapes=[pltpu.VMEM((H_B, DK_B, DV_B), F32)],
        compiler_params=_cp(("parallel", "arbitrary")),
        name="gla_prompt",
    )(z0, z0, z0, z0, glr, wg2, bg2.reshape(1, qk_w), g_go.reshape(1, DV_B))


def _gla_decode_kernel(q_ref, k_ref, v_ref, gout_ref, glr_ref, wg2_ref, bg2_ref, ggo_ref, s_ref, o_ref, so_ref):
    nb = q_ref.shape[0]
    ld = _gla_logdec(glr_ref[...], wg2_ref, bg2_ref)
    g_go = ggo_ref[...]
    for h in range(H_B):
        ks = slice(h * DK_B, (h + 1) * DK_B)
        vs = slice(h * DV_B, (h + 1) * DV_B)
        eg_t = jnp.exp(ld[:, ks]).T
        k_t = k_ref[:, ks].T
        q_t = (q_ref[:, ks] * (DK_B ** -0.5)).T
        for i in range(nb):
            state = eg_t[:, i:i + 1] * s_ref[i, h] + k_t[:, i:i + 1] * v_ref[i:i + 1, vs]
            so_ref[i, h] = state
            o = jnp.sum(q_t[:, i:i + 1] * state, axis=0, keepdims=True)
            o_ref[i:i + 1, vs] = _gla_out(o, g_go, gout_ref[i:i + 1, vs])


def gla_decode(q, k, v, gout, glr, wg2, bg2, g_go, state):
    bsz = q.shape[0]
    nb = 8
    qk_w = H_B * DK_B
    v_w = H_B * DV_B
    rows = lambda w: pl.BlockSpec((nb, w), lambda i: (i, 0))
    st = pl.BlockSpec((nb, H_B, DK_B, DV_B), lambda i: (i, 0, 0, 0))
    return pl.pallas_call(
        _gla_decode_kernel,
        out_shape=[
            jax.ShapeDtypeStruct((bsz, v_w), BF16),
            jax.ShapeDtypeStruct(state.shape, F32),
        ],
        grid=(bsz // nb,),
        in_specs=[
            rows(qk_w), rows(qk_w), rows(v_w), rows(v_w), rows(128),
            pl.BlockSpec((128, qk_w), lambda i: (0, 0)),
            pl.BlockSpec((1, qk_w), lambda i: (0, 0)),
            pl.BlockSpec((1, DV_B), lambda i: (0, 0)),
            st,
        ],
        out_specs=[rows(v_w), st],
        compiler_params=_cp(("parallel",)),
        name="gla_decode",
    )(q, k, v, gout, glr, wg2, bg2.reshape(1, qk_w), g_go.reshape(1, DV_B), state)


Z1_CX, Z1_CY, Z1_QD, Z1_KD, Z1_VD = 0, 1024, 2048, 3072, 3584


def _rg_gates(xc, wa_ref, ba_ref, wx_ref, bx_ref, lam_ref):
    xb = xc.astype(BF16)
    ra = jnp.concatenate(
        [_dot(xb[:, n * C_BLOCK:(n + 1) * C_BLOCK], wa_ref[n]) for n in range(C_BLOCKS)], axis=-1) + ba_ref[...]
    rx = jnp.concatenate(
        [_dot(xb[:, n * C_BLOCK:(n + 1) * C_BLOCK], wx_ref[n]) for n in range(C_BLOCKS)], axis=-1) + bx_ref[...]
    r_gate = jax.nn.sigmoid(ra)
    i_gate = jax.nn.sigmoid(rx)
    log_a = (-RG_C * r_gate) * _softplus(-lam_ref[...])
    a = jnp.exp(log_a)
    one_minus_a2 = -jnp.tanh(log_a) * (a * a + 1.0)
    u = jnp.sqrt(one_minus_a2) * (i_gate * xc)
    return a, u


def _shift_rows(x, d, fill):
    rows = lax.broadcasted_iota(jnp.int32, x.shape, 0)
    return jnp.where(rows < d, fill, pltpu.roll(x, d, 0))


def _rglru_prompt_kernel(cx_ref, cy_ref, cw_ref, cb_ref, wa_ref, ba_ref, wx_ref, bx_ref, lam_ref,
                         o_ref, tail_ref, hl_ref, prev_ref, hc_ref):
    t_idx = pl.program_id(1)
    tt = cx_ref.shape[0]

    @pl.when(t_idx == 0)
    def _():
        prev_ref[...] = jnp.zeros_like(prev_ref)
        hc_ref[...] = jnp.zeros_like(hc_ref)

    x = cx_ref[...]
    xp = jnp.concatenate([prev_ref[...], x], axis=0)
    xc = cb_ref[...] + cw_ref[CONV_W - 1:CONV_W, :] * x
    for j in range(CONV_W - 1):
        d = CONV_W - 1 - j
        xc = xc + cw_ref[j:j + 1, :] * pltpu.roll(xp, d, 0)[8:]
    prev_ref[...] = x[tt - 8:]
    tail_ref[0] = x[tt - 8:]

    a, u = _rg_gates(xc, wa_ref, ba_ref, wx_ref, bx_ref, lam_ref)
    d = 1
    while d < tt:
        u = a * _shift_rows(u, d, 0.0) + u
        a = a * _shift_rows(a, d, 1.0)
        d *= 2
    h = a * hc_ref[...] + u
    hc_ref[...] = h[tt - 1:]
    hl_ref[0] = h[tt - 1:]
    o_ref[...] = (h * jax.nn.gelu(cy_ref[...])).astype(BF16)


def rglru_prompt(z1, conv_w, conv_b, wa, ba, wx, bx, lam, bsz, t):
    tt = 256
    nt = t // tt
    c = C_WIDTH
    vec = pl.BlockSpec((1, c), lambda b, i: (0, 0))
    wblk = pl.BlockSpec((C_BLOCKS, C_BLOCK, C_BLOCK), lambda b, i: (0, 0, 0))
    return pl.pallas_call(
        _rglru_prompt_kernel,
        out_shape=[
            jax.ShapeDtypeStruct((bsz * t, c), BF16),
            jax.ShapeDtypeStruct((bsz, 8, c), F32),
            jax.ShapeDtypeStruct((bsz, 1, c), F32),
        ],
        grid=(bsz, nt),
        in_specs=[
            pl.BlockSpec((tt, c), lambda b, i: (b * nt + i, Z1_CX // c)),
            pl.BlockSpec((tt, c), lambda b, i: (b * nt + i, Z1_CY // c)),
            pl.BlockSpec((CONV_W, c), lambda b, i: (0, 0)),
            vec, wblk, vec, wblk, vec, vec,
        ],
        out_specs=[
            pl.BlockSpec((tt, c), lambda b, i: (b * nt + i, 0)),
            pl.BlockSpec((1, 8, c), lambda b, i: (b, 0, 0)),
            pl.BlockSpec((1, 1, c), lambda b, i: (b, 0, 0)),
        ],
        scratch_shapes=[pltpu.VMEM((8, c), F32), pltpu.VMEM((1, c), F32)],
        compiler_params=_cp(("parallel", "arbitrary")),
        name="rglru_prompt",
    )(z1, z1, conv_w, conv_b.reshape(1, c), wa, ba.reshape(1, c), wx, bx.reshape(1, c), lam.reshape(1, c))


def _rglru_decode_kernel(cx_ref, cy_ref, buf_ref, h0_ref, cw_ref, cb_ref, wa_ref, ba_ref, wx_ref, bx_ref, lam_ref,
                         o_ref, nbuf_ref, hl_ref):
    x = cx_ref[...]
    xc = cb_ref[...] + cw_ref[CONV_W - 1:CONV_W, :] * x
    for j in range(CONV_W - 1):
        xc = xc + cw_ref[j:j + 1, :] * buf_ref[j]
    for j in range(CONV_W - 2):
        nbuf_ref[j] = buf_ref[j + 1]
    nbuf_ref[CONV_W - 2] = x
    a, u = _rg_gates(xc, wa_ref, ba_ref, wx_ref, bx_ref, lam_ref)
    h = a * h0_ref[...] + u
    hl_ref[...] = h
    o_ref[...] = (h * jax.nn.gelu(cy_ref[...])).astype(BF16)


def rglru_decode(cx, cy, buf, h0, conv_w, conv_b, wa, ba, wx, bx, lam):
    bsz, c = cx.shape
    return pl.pallas_call(
        _rglru_decode_kernel,
        out_shape=[
            jax.ShapeDtypeStruct((bsz, c), BF16),
            jax.ShapeDtypeStruct(buf.shape, F32),
            jax.ShapeDtypeStruct((bsz, c), F32),
        ],
        compiler_params=pltpu.CompilerParams(vmem_limit_bytes=V7X_VMEM_LIMIT_BYTES),
        name="rglru_decode",
    )(cx, cy, buf, h0, conv_w, conv_b.reshape(1, c), wa, ba.reshape(1, c), wx, bx.reshape(1, c), lam.reshape(1, c))


def _strict_upper_ones(n):
    j = lax.broadcasted_iota(jnp.int32, (n, n), 0)
    s = lax.broadcasted_iota(jnp.int32, (n, n), 1)
    return jnp.where(j > s, 1.0, 0.0).astype(BF16)


def _sb_tile(z, later, surv, mask):
    sp = _softplus(z)
    l1m = -sp
    lsg = z - sp
    if mask is not None:
        l1m = jnp.where(mask, l1m, 0.0)
    hi, lo = _split_bf16(l1m)
    suffix = _dot(hi, later) + _dot(lo, later)
    w = jnp.exp(lsg + suffix + surv)
    if mask is not None:
        w = jnp.where(mask, w, 0.0)
    return w, jnp.sum(l1m, axis=-1, keepdims=True)


def _sb_prompt_kernel(bias_ref, q_ref, k_ref, v_ref, gq_ref, gk_ref, o_ref, ko_ref, kb_ref, vb_ref):
    kh = pl.program_id(1)
    i = pl.program_id(2)
    blk = q_ref.shape[0]
    grp = H_D // KVH_D

    @pl.when(i == 0)
    def _():
        kn = _rms_rows(k_ref[...], gk_ref[...])
        ko_ref[...] = kn
        kb_ref[...] = kn.astype(BF16)
        vb_ref[...] = v_ref[...].astype(BF16)

    gq = gq_ref[...]
    qs = jnp.concatenate(
        [_rms_rows(q_ref[:, g * HD_D:(g + 1) * HD_D], gq) for g in range(grp)], axis=0).astype(BF16)
    bias = jnp.concatenate(
        [jnp.full((blk, 1), bias_ref[kh * grp + g], F32) for g in range(grp)], axis=0)
    later = _strict_upper_ones(blk)
    scale = HD_D ** -0.5

    def tile(j, mask, acc, surv):
        off = pl.multiple_of(j * blk, blk)
        z = _dot_nt(qs, kb_ref[pl.ds(off, blk), :]) * scale + bias
        w, tot = _sb_tile(z, later, surv, mask)
        return acc + _dot(w.astype(BF16), vb_ref[pl.ds(off, blk), :]), surv + tot

    rows = lax.broadcasted_iota(jnp.int32, (grp * blk, blk), 0) & (blk - 1)
    cols = lax.broadcasted_iota(jnp.int32, (grp * blk, blk), 1)
    acc, surv = tile(i, cols < rows, jnp.zeros((grp * blk, HD_D), F32), jnp.zeros((grp * blk, 1), F32))

    def body(step, carry):
        return tile(i - 1 - step, None, *carry)

    acc, _ = lax.fori_loop(0, i, body, (acc, surv))
    for g in range(grp):
        o_ref[:, g * HD_D:(g + 1) * HD_D] = acc[g * blk:(g + 1) * blk].astype(BF16)


def sb_prompt(z1, gq, gk, bias, bsz, t):
    blk = SB_PROMPT_TILE
    nb = t // blk
    grp = H_D // KVH_D
    qw = grp * HD_D
    grid_spec = pltpu.PrefetchScalarGridSpec(
        num_scalar_prefetch=1,
        grid=(bsz, KVH_D, nb),
        in_specs=[
            pl.BlockSpec((blk, qw), lambda b, k, i, s: (b * nb + i, Z1_QD // qw + k)),
            pl.BlockSpec((t, HD_D), lambda b, k, i, s: (b, Z1_KD // HD_D + k)),
            pl.BlockSpec((t, HD_D), lambda b, k, i, s: (b, Z1_VD // HD_D + k)),
            pl.BlockSpec((1, HD_D), lambda b, k, i, s: (0, 0)),
            pl.BlockSpec((1, HD_D), lambda b, k, i, s: (0, 0)),
        ],
        out_specs=[
            pl.BlockSpec((blk, qw), lambda b, k, i, s: (b * nb + i, k)),
            pl.BlockSpec((t, HD_D), lambda b, k, i, s: (b, k)),
        ],
        scratch_shapes=[pltpu.VMEM((t, HD_D), BF16), pltpu.VMEM((t, HD_D), BF16)],
    )
    return pl.pallas_call(
        _sb_prompt_kernel,
        out_shape=[
            jax.ShapeDtypeStruct((bsz * t, H_D * HD_D), BF16),
            jax.ShapeDtypeStruct((bsz * t, KVH_D * HD_D), F32),
        ],
        grid_spec=grid_spec,
        compiler_params=_cp(("parallel", "parallel", "arbitrary")),
        name="sb_prompt",
    )(bias, z1, z1, z1, gq.reshape(1, HD_D), gk.reshape(1, HD_D))


def _sb_decode_kernel(n_pages, pt_ref, bias_ref, q_ref, k_ref, v_ref, gq_ref, gk_ref, ck_hbm, cv_hbm,
                      o_ref, ko_ref, kbuf, vbuf, sem):
    b = pl.program_id(0)
    nb = pl.num_programs(0)
    npg = SB_PAGES_PER_STEP
    n_chunks = n_pages // npg
    prow = PAGE * KVH_D
    grp = H_D // KVH_D
    scale = HD_D ** -0.5

    def page_copies(bb, chunk, slot):
        cps = []
        for p in range(npg):
            page = pt_ref[bb * n_pages + chunk * npg + p]
            dst = pl.ds(p * prow, prow)
            cps.append(pltpu.make_async_copy(ck_hbm.at[page], kbuf.at[slot, dst], sem.at[0, slot]))
            cps.append(pltpu.make_async_copy(cv_hbm.at[page], vbuf.at[slot, dst], sem.at[1, slot]))
        return cps

    @pl.when(b == 0)
    def _():
        for cp in page_copies(0, n_chunks - 1, 0):
            cp.start()

    bias = jnp.concatenate([jnp.full((1, 1), bias_ref[h], F32) for h in range(H_D)], axis=0)
    qn = _rms_rows(q_ref[0], gq_ref[...])
    qb = qn.astype(BF16)
    kn = _rms_rows(k_ref[0], gk_ref[...])
    ko_ref[0] = kn
    kn_sel = jnp.concatenate([kn[h // grp:h // grp + 1] for h in range(H_D)], axis=0)
    v_sel = jnp.concatenate([v_ref[0, h // grp:h // grp + 1] for h in range(H_D)], axis=0)
    z0 = jnp.sum(qn * kn_sel, axis=-1, keepdims=True) * scale + bias
    visible = jnp.zeros((H_D, 1), jnp.int32) < jnp.zeros((H_D, 1), jnp.int32)
    sp0 = _softplus(z0)
    acc = jnp.where(visible, jnp.exp(z0 - sp0), 0.0) * v_sel
    surv = jnp.where(visible, -sp0, 0.0)

    later = _strict_upper_ones(prow)
    row = lax.broadcasted_iota(jnp.int32, (npg * H_D, prow), 0)
    col = lax.broadcasted_iota(jnp.int32, (npg * H_D, prow), 1)
    valid = (col & (KVH_D - 1)) == ((row & (H_D - 1)) // grp)
    bias_r = jnp.concatenate([bias] * npg, axis=0)

    for i in range(n_chunks):
        slot = i % 2
        if i + 1 < n_chunks:
            for cp in page_copies(b, n_chunks - 2 - i, 1 - slot):
                cp.start()
        else:
            @pl.when(b + 1 < nb)
            def _():
                for cp in page_copies(b + 1, n_chunks - 1, 1 - slot):
                    cp.start()
        for cp in page_copies(b, n_chunks - 1 - i, slot):
            cp.wait()

        z = _dot_nt(qb, kbuf[slot].astype(BF16))
        z = jnp.concatenate([z[:, p * prow:(p + 1) * prow] for p in range(npg)], axis=0) * scale + bias_r
        sp = _softplus(z)
        l1m = jnp.where(valid, -sp, 0.0)
        hi, lo = _split_bf16(l1m)
        suffix = _dot(hi, later) + _dot(lo, later)
        tot = jnp.sum(l1m, axis=-1, keepdims=True)
        survs = [None] * npg
        for p in reversed(range(npg)):
            survs[p] = surv
            surv = surv + tot[p * H_D:(p + 1) * H_D]
        w = jnp.where(valid, jnp.exp((z - sp) + suffix + jnp.concatenate(survs, axis=0)), 0.0)
        w = jnp.concatenate([w[p * H_D:(p + 1) * H_D] for p in range(npg)], axis=1).astype(BF16)
        acc = acc + _dot(w, vbuf[slot].astype(BF16))

    o_ref[0] = acc.astype(BF16)


def sb_decode(q, k, v, gq, gk, bias, cache_k, cache_v, page_table):
    bsz = q.shape[0]
    n_pages = page_table.shape[1]
    npg = SB_PAGES_PER_STEP
    assert n_pages % (2 * npg) == 0
    prow = PAGE * KVH_D
    grid_spec = pltpu.PrefetchScalarGridSpec(
        num_scalar_prefetch=2,
        grid=(bsz,),
        in_specs=[
            pl.BlockSpec((1, H_D, HD_D), lambda b, pt, bias_: (b, 0, 0)),
            pl.BlockSpec((1, KVH_D, HD_D), lambda b, pt, bias_: (b, 0, 0)),
            pl.BlockSpec((1, KVH_D, HD_D), lambda b, pt, bias_: (b, 0, 0)),
            pl.BlockSpec((1, HD_D), lambda b, pt, bias_: (0, 0)),
            pl.BlockSpec((1, HD_D), lambda b, pt, bias_: (0, 0)),
            pl.BlockSpec(memory_space=pl.ANY),
            pl.BlockSpec(memory_space=pl.ANY),
        ],
        out_specs=[
            pl.BlockSpec((1, H_D, HD_D), lambda b, pt, bias_: (b, 0, 0)),
            pl.BlockSpec((1, KVH_D, HD_D), lambda b, pt, bias_: (b, 0, 0)),
        ],
        scratch_shapes=[
            pltpu.VMEM((2, npg * prow, HD_D), F32),
            pltpu.VMEM((2, npg * prow, HD_D), F32),
            pltpu.SemaphoreType.DMA((2, 2)),
        ],
    )
    return pl.pallas_call(
        functools.partial(_sb_decode_kernel, n_pages),
        out_shape=[
            jax.ShapeDtypeStruct((bsz, H_D, HD_D), BF16),
            jax.ShapeDtypeStruct((bsz, KVH_D, HD_D), F32),
        ],
        grid_spec=grid_spec,
        compiler_params=_cp(("arbitrary",)),
        name="sb_decode",
    )(page_table.reshape(-1), bias, q, k, v, gq.reshape(1, HD_D), gk.reshape(1, HD_D), cache_k, cache_v)


def kernel(x_prompt, x_sample, state_swa_k, state_swa_v, state_gla, state_conv, state_lru, cache_sb_k, cache_sb_v, page_table, p_prompt, p_sample, g_mix, g_ffn, w_ffn_gate, w_ffn_up, w_ffn_down, g_ple, w_ple_gate, w_ple_proj, w_in_even, g_qnorm_a, g_knorm_a, sinks_a, w_gla_gate2, b_gla_gate2, g_gla_out, w_out_even, w_in_odd, conv_w, conv_b, w_rg_a, b_rg_a, w_rg_x, b_rg_x, lru_lambda, g_qnorm_d, g_knorm_d, sb_bias, w_out_odd):
    bsz, t, d = x_prompt.shape
    dbs = x_sample.shape[0]
    n_p = bsz * t
    depth = g_mix.shape[0]
    ple = p_prompt.shape[-1]

    h = jnp.concatenate([x_prompt.reshape(n_p, d), x_sample.reshape(dbs, d)], axis=0)
    p_all = jnp.concatenate([p_prompt.reshape(depth, n_p, ple), p_sample.reshape(depth, dbs, ple)], axis=1)

    def dense_tail(h, i):
        h = ffn_residual(h, g_ffn[i], w_ffn_gate[i].astype(BF16), w_ffn_up[i].astype(BF16),
                         w_ffn_down[i].astype(BF16), 512, f"ffn_{i}")
        return ple_residual(h, g_ple[i], p_all[i], w_ple_gate[i].astype(BF16), w_ple_proj[i].astype(BF16), f"ple_{i}")

    w = w_in_even[0]
    qa, ka, va, qb, kb, vb, glr_w, gout = jnp.split(
        w, [1024, 1280, 1536, 2048, 2560, 3584, 3600], axis=1)
    w0 = jnp.concatenate([qa, vb, gout, ka, va, qb, kb], axis=1).astype(BF16)
    w0_glr = jnp.pad(glr_w, ((0, 0), (0, 128 - GLA_RANK))).astype(BF16)
    z0 = norm_matmul(h, g_mix[0], w0, 1536, "in_proj_even")
    glr = norm_matmul(h, g_mix[0], w0_glr, 128, "in_proj_glr")
    wg2 = jnp.pad(w_gla_gate2[0], ((0, 128 - GLA_RANK), (0, 0))).astype(BF16)

    oa_p, swa_k_p, swa_v_p = swa_prompt(z0, g_qnorm_a[0], g_knorm_a[0], sinks_a[0], bsz, t)
    ob_p, gla_p = gla_prompt(z0, glr, wg2, b_gla_gate2[0], g_gla_out[0], bsz, t)

    zd = z0[n_p:]
    kw_a = KVH_A * HD_A
    to_keys_last = lambda s: jnp.transpose(s, (0, 2, 3, 1)).reshape(dbs, kw_a, -1)
    from_keys_last = lambda s: jnp.transpose(s.reshape(dbs, KVH_A, HD_A, -1), (0, 3, 1, 2))[None]
    oa_d, swa_kt_d, swa_vt_d = swa_decode(
        zd[:, Z0_QA:Z0_QA + 1024].reshape(dbs, H_A, HD_A),
        zd[:, Z0_KA:Z0_KA + kw_a], zd[:, Z0_VA:Z0_VA + kw_a],
        to_keys_last(state_swa_k[0]), to_keys_last(state_swa_v[0]),
        g_qnorm_a[0], g_knorm_a[0], sinks_a[0])
    ob_d, gla_d = gla_decode(
        zd[:, Z0_QB:Z0_QB + 512], zd[:, Z0_KB:Z0_KB + 512], zd[:, Z0_VB:Z0_VB + 1024],
        zd[:, Z0_GOUT:Z0_GOUT + 1024], glr[n_p:], wg2, b_gla_gate2[0], g_gla_out[0], state_gla[0])

    oa = jnp.concatenate([oa_p, oa_d.reshape(dbs, 1024)], axis=0)
    ob = jnp.concatenate([ob_p, ob_d], axis=0)
    wo = w_out_even[0].astype(BF16)
    h = proj_residual(h, oa, ob, wo[:1024], wo[1024:], "out_proj_even")
    h = dense_tail(h, 0)

    z1 = norm_matmul(h, g_mix[1], w_in_odd[0].astype(BF16), 1024, "in_proj_odd")
    wa = w_rg_a[0].astype(BF16)
    wx = w_rg_x[0].astype(BF16)
    oc_p, conv_tail, lru_p = rglru_prompt(z1, conv_w[0], conv_b[0], wa, b_rg_a[0], wx, b_rg_x[0], lru_lambda[0], bsz, t)
    od_p, sb_k_p = sb_prompt(z1, g_qnorm_d[0], g_knorm_d[0], sb_bias[0], bsz, t)

    zd = z1[n_p:]
    oc_d, conv_d, lru_d = rglru_decode(
        zd[:, Z1_CX:Z1_CX + 1024], zd[:, Z1_CY:Z1_CY + 1024],
        jnp.swapaxes(state_conv[0], 0, 1), state_lru[0],
        conv_w[0], conv_b[0], wa, b_rg_a[0], wx, b_rg_x[0], lru_lambda[0])
    kvw = KVH_D * HD_D
    n_phys = cache_sb_k.shape[1]
    od_d, sb_k_d = sb_decode(
        zd[:, Z1_QD:Z1_QD + 1024].reshape(dbs, H_D, HD_D),
        zd[:, Z1_KD:Z1_KD + kvw].reshape(dbs, KVH_D, HD_D),
        zd[:, Z1_VD:Z1_VD + kvw].reshape(dbs, KVH_D, HD_D),
        g_qnorm_d[0], g_knorm_d[0], sb_bias[0],
        cache_sb_k[0].reshape(n_phys, PAGE * KVH_D, HD_D), cache_sb_v[0].reshape(n_phys, PAGE * KVH_D, HD_D),
        page_table)

    oc = jnp.concatenate([oc_p, oc_d], axis=0)
    od = jnp.concatenate([od_p, od_d.reshape(dbs, 1024)], axis=0)
    wo = w_out_odd[0].astype(BF16)
    h = proj_residual(h, oc, od, wo[:1024], wo[1024:], "out_proj_odd")
    h = dense_tail(h, 1)

    y_prompt = h[:n_p].reshape(bsz, t, d)
    y_sample = h[n_p:].reshape(dbs, 1, d)
    n_keep = min(WINDOW, t)
    return (
        y_prompt, y_sample,
        swa_k_p.reshape(1, bsz, n_keep, KVH_A, HD_A), swa_v_p.reshape(1, bsz, n_keep, KVH_A, HD_A),
        gla_p[None],
        conv_tail[:, 8 - (CONV_W - 1):][None], lru_p.reshape(1, bsz, C_WIDTH),
        sb_k_p.reshape(1, bsz, t, KVH_D, HD_D), z1[:n_p, Z1_VD:].reshape(1, bsz, t, KVH_D, HD_D),
        from_keys_last(swa_kt_d), from_keys_last(swa_vt_d),
        gla_d[None],
        jnp.swapaxes(conv_d, 0, 1)[None], lru_d[None],
        sb_k_d.reshape(1, dbs, 1, KVH_D, HD_D), zd[:, Z1_VD:].reshape(1, dbs, 1, KVH_D, HD_D),
    )
```

```python
import functools

import jax
import jax.numpy as jnp
from jax import lax
from jax.experimental import pallas as pl
from jax.experimental.pallas import tpu as pltpu

F32 = jnp.float32
BF16 = jnp.bfloat16
EPS = 1e-6

V7X_VMEM_LIMIT_BYTES = 56 * 1024 * 1024

H_A, KVH_A, HD_A, WINDOW = 16, 4, 64, 128
H_B, DK_B, DV_B, GLA_RANK, GLA_GATE_NORM, GLA_CHUNK = 4, 128, 256, 16, 16.0, 64
C_WIDTH, C_BLOCKS, CONV_W, RG_C = 1024, 8, 4, 8.0
C_BLOCK = C_WIDTH // C_BLOCKS
H_D, KVH_D, HD_D = 8, 4, 128
PAGE = 128

ROW_TILE = 640
SB_PROMPT_TILE = 256
SB_PAGES_PER_STEP = 16


def _cp(sem):
    return pltpu.CompilerParams(dimension_semantics=sem, vmem_limit_bytes=V7X_VMEM_LIMIT_BYTES)


def _rms_rows(x, g):
    r = lax.rsqrt(jnp.mean(x * x, axis=-1, keepdims=True) + EPS)
    return (x * r) * g


def _softplus(x):
    return jnp.maximum(x, 0.0) + jnp.log(1.0 + jnp.exp(-jnp.abs(x)))


def _log_sigmoid(x):
    return jnp.minimum(x, 0.0) - jnp.log(1.0 + jnp.exp(-jnp.abs(x)))


def _split_bf16(x):
    hi = x.astype(BF16)
    lo = (x - hi.astype(F32)).astype(BF16)
    return hi, lo


def _dot(a, b):
    return jnp.dot(a, b, preferred_element_type=F32)


def _dot_nt(a, b):
    return lax.dot_general(a, b, (((1,), (1,)), ((), ())), preferred_element_type=F32)


def _dot_tn(a, b):
    return lax.dot_general(a, b, (((0,), (0,)), ((), ())), preferred_element_type=F32)


def _norm_matmul_kernel(x_ref, g_ref, w_ref, *rest):
    ws_ref, o_ref, os_ref, xn_ref = rest if len(rest) == 4 else (None, rest[0], None, rest[1])

    @pl.when(pl.program_id(1) == 0)
    def _():
        xn_ref[...] = _rms_rows(x_ref[...], g_ref[...]).astype(BF16)
        if ws_ref is not None:
            os_ref[...] = _dot(xn_ref[...], ws_ref[...])

    o_ref[...] = _dot(xn_ref[...], w_ref[...])


def norm_matmul(x, g, w, tn, name, w_side=None):
    n, d = x.shape
    nout = w.shape[1]
    tm = ROW_TILE
    in_specs = [
        pl.BlockSpec((tm, d), lambda i, j: (i, 0)),
        pl.BlockSpec((1, d), lambda i, j: (0, 0)),
        pl.BlockSpec((d, tn), lambda i, j: (0, j)),
    ]
    out_shape = [jax.ShapeDtypeStruct((n, nout), F32)]
    out_specs = [pl.BlockSpec((tm, tn), lambda i, j: (i, j))]
    args = [x, g.reshape(1, d), w]
    if w_side is not None:
        ns = w_side.shape[1]
        in_specs.append(pl.BlockSpec((d, ns), lambda i, j: (0, 0)))
        out_shape.append(jax.ShapeDtypeStruct((n, ns), F32))
        out_specs.append(pl.BlockSpec((tm, ns), lambda i, j: (i, 0)))
        args.append(w_side)
    out = pl.pallas_call(
        _norm_matmul_kernel,
        out_shape=out_shape,
        grid=(n // tm, nout // tn),
        in_specs=in_specs,
        out_specs=out_specs,
        scratch_shapes=[pltpu.VMEM((tm, d), BF16)],
        compiler_params=_cp(("parallel", "arbitrary")),
        name=name,
    )(*args)
    return out if w_side is not None else out[0]


def _proj_res_kernel(h_ref, a_ref, b_ref, at_ref, bt_ref, wa_ref, wb_ref, o_ref):
    tm = h_ref.shape[0]
    nt = at_ref.shape[0]
    o_ref[...] = h_ref[...] + (_dot(a_ref[...], wa_ref[...]) + _dot(b_ref[...], wb_ref[...]))

    @pl.when(pl.program_id(0) == pl.num_programs(0) - 1)
    def _():
        tail = _dot(at_ref[...], wa_ref[...]) + _dot(bt_ref[...], wb_ref[...])
        o_ref[tm - nt:, :] = h_ref[tm - nt:, :] + tail


def proj_residual(h, a, b, a_tail, b_tail, wa, wb, name):
    n, d = h.shape
    ka, kb = a.shape[1], b.shape[1]
    nt = a_tail.shape[0]
    tm = ROW_TILE
    assert a.shape[0] + nt == n and n % tm == 0 and a.shape[0] % tm + nt == tm
    return pl.pallas_call(
        _proj_res_kernel,
        out_shape=jax.ShapeDtypeStruct((n, d), F32),
        grid=(n // tm,),
        in_specs=[
            pl.BlockSpec((tm, d), lambda i: (i, 0)),
            pl.BlockSpec((tm, ka), lambda i: (i, 0)),
            pl.BlockSpec((tm, kb), lambda i: (i, 0)),
            pl.BlockSpec((nt, ka), lambda i: (0, 0)),
            pl.BlockSpec((nt, kb), lambda i: (0, 0)),
            pl.BlockSpec((ka, d), lambda i: (0, 0)),
            pl.BlockSpec((kb, d), lambda i: (0, 0)),
        ],
        out_specs=pl.BlockSpec((tm, d), lambda i: (i, 0)),
        compiler_params=_cp(("parallel",)),
        name=name,
    )(h, a, b, a_tail, b_tail, wa, wb)


def _ffn_kernel(h_ref, g_ref, wg_ref, wu_ref, wd_ref, o_ref, hn_ref, acc_ref):
    f = pl.program_id(1)

    @pl.when(f == 0)
    def _():
        hn_ref[...] = _rms_rows(h_ref[...], g_ref[...]).astype(BF16)
        acc_ref[...] = jnp.zeros_like(acc_ref)

    hn = hn_ref[...]
    gate = _dot(hn, wg_ref[...])
    up = _dot(hn, wu_ref[...])
    act = (gate * jax.nn.sigmoid(gate)) * up
    acc_ref[...] += _dot(act.astype(BF16), wd_ref[...])

    @pl.when(f == pl.num_programs(1) - 1)
    def _():
        o_ref[...] = h_ref[...] + acc_ref[...]


def ffn_residual(h, g, wg, wu, wd, layer, tf, name):
    n, d = h.shape
    dff = wg.shape[2]
    tm = ROW_TILE
    return pl.pallas_call(
        _ffn_kernel,
        out_shape=jax.ShapeDtypeStruct((n, d), F32),
        grid=(n // tm, dff // tf),
        in_specs=[
            pl.BlockSpec((tm, d), lambda i, f: (i, 0)),
            pl.BlockSpec((1, d), lambda i, f: (0, 0)),
            pl.BlockSpec((None, d, tf), lambda i, f: (layer, 0, f)),
            pl.BlockSpec((None, d, tf), lambda i, f: (layer, 0, f)),
            pl.BlockSpec((None, tf, d), lambda i, f: (layer, f, 0)),
        ],
        out_specs=pl.BlockSpec((tm, d), lambda i, f: (i, 0)),
        scratch_shapes=[pltpu.VMEM((tm, d), BF16), pltpu.VMEM((tm, d), F32)],
        compiler_params=_cp(("parallel", "arbitrary")),
        name=name,
    )(h, g.reshape(1, d), wg, wu, wd)


def _ple_kernel(h_ref, g_ref, p_ref, wg_ref, wp_ref, o_ref, *tail_ref):
    h = h_ref[...]
    hn = _rms_rows(h, g_ref[...]).astype(BF16)
    gate = jax.nn.sigmoid(_dot(hn, wg_ref[...]))
    proj = _dot(p_ref[...].astype(BF16), wp_ref[...])
    y = h + gate * proj
    o_ref[...] = y
    if tail_ref:
        @pl.when(pl.program_id(0) == pl.num_programs(0) - 1)
        def _():
            nt = tail_ref[0].shape[0]
            tail_ref[0][...] = y[y.shape[0] - nt:, :]


def ple_residual(h, g, p, wg, wp, layer, name, n_tail=0):
    n, d = h.shape
    dp = p.shape[2]
    tm = ROW_TILE
    assert n % tm == 0 and (n_tail == 0 or (n - n_tail) % tm + n_tail == tm)
    out_shape = [jax.ShapeDtypeStruct((n - n_tail, d), F32)]
    out_specs = [pl.BlockSpec((tm, d), lambda i: (i, 0))]
    if n_tail:
        out_shape.append(jax.ShapeDtypeStruct((n_tail, d), F32))
        out_specs.append(pl.BlockSpec((n_tail, d), lambda i: (0, 0)))
    out = pl.pallas_call(
        _ple_kernel,
        out_shape=out_shape,
        grid=(n // tm,),
        in_specs=[
            pl.BlockSpec((tm, d), lambda i: (i, 0)),
            pl.BlockSpec((1, d), lambda i: (0, 0)),
            pl.BlockSpec((None, tm, dp), lambda i: (layer, i, 0)),
            pl.BlockSpec((None, d, d), lambda i: (layer, 0, 0)),
            pl.BlockSpec((None, dp, d), lambda i: (layer, 0, 0)),
        ],
        out_specs=out_specs,
        compiler_params=_cp(("arbitrary",)),
        name=name,
    )(h, g.reshape(1, d), p, wg, wp)
    return out if n_tail else out[0]


Z0_QA, Z0_VB, Z0_GOUT, Z0_KA, Z0_VA, Z0_QB, Z0_KB = 0, 1024, 2048, 3072, 3328, 3584, 4096
Z0_WIDTH = 4608


def _swa_prompt_kernel(sink_ref, q_ref, kvp_ref, kvc_ref, gq_ref, gk_ref, o_ref, ko_ref, vo_ref):
    n = pl.program_id(1)
    blk = q_ref.shape[0]
    grp = H_A // KVH_A
    kw = KVH_A * HD_A
    lanes = 2 * HD_A
    gq = gq_ref[...]
    gk = gk_ref[...]
    kv = jnp.concatenate([kvp_ref[...], kvc_ref[...]], axis=0)
    rows = lax.broadcasted_iota(jnp.int32, (grp * blk, 2 * blk), 0)
    cols = lax.broadcasted_iota(jnp.int32, (grp * blk, 2 * blk), 1)
    diff = (rows & (blk - 1)) - cols + blk
    lo = jnp.where(n == 0, blk, 0)
    mask = (diff >= 0) & (diff <= WINDOW) & (cols >= lo)

    r = lax.broadcasted_iota(jnp.int32, (lanes, lanes), 0)
    c = lax.broadcasted_iota(jnp.int32, (lanes, lanes), 1)
    head_sum = jnp.where(r // HD_A == c // HD_A, 1.0, 0.0).astype(BF16)
    spread = [jnp.where(r == (c & (HD_A - 1)) + half * HD_A, 1.0, 0.0).astype(BF16) for half in range(2)]
    ones_kv = jnp.ones((2 * blk, lanes), BF16)
    low_half = lax.broadcasted_iota(jnp.int32, (blk, lanes), 1) < HD_A

    def head_norm(x, g):
        hi, lo_ = _split_bf16(x * x)
        ss = _dot(hi, head_sum) + _dot(lo_, head_sum)
        return (x * lax.rsqrt(ss * (1.0 / HD_A) + EPS)) * g

    for pair in range(KVH_A // 2):
        ps = slice(pair * lanes, (pair + 1) * lanes)
        kn_pair = head_norm(kv[:, ps], gk)
        ko_ref[0, :, ps] = kn_pair[blk:]
        kn_b = kn_pair.astype(BF16)
        v_b = kv[:, kw + pair * lanes: kw + (pair + 1) * lanes].astype(BF16)
        for half in range(2):
            kh = 2 * pair + half
            kd = _dot(kn_b, spread[half]).astype(BF16)
            vd = _dot(v_b, spread[half]).astype(BF16)
            s_parts = []
            for j in range(grp // 2):
                qs = slice(kh * grp * HD_A + j * lanes, kh * grp * HD_A + (j + 1) * lanes)
                qn = head_norm(q_ref[:, qs], gq)
                s_parts.append(_dot_nt(jnp.where(low_half, qn, 0.0).astype(BF16), kd))
                s_parts.append(_dot_nt(jnp.where(low_half, 0.0, qn).astype(BF16), kd))
            s = jnp.concatenate(s_parts, axis=0) * (HD_A ** -0.5)
            s = jnp.where(mask, s, -jnp.inf)
            sink = jnp.concatenate(
                [jnp.full((blk, 1), sink_ref[kh * grp + g], F32) for g in range(grp)], axis=0)
            m = jnp.maximum(jnp.max(s, axis=-1, keepdims=True), sink)
            e_hi, e_lo = _split_bf16(jnp.exp(s - m))
            den = _dot(e_hi, ones_kv) + _dot(e_lo, ones_kv) + jnp.exp(sink - m)
            o = _dot(e_hi, vd) / den
            for j in range(grp // 2):
                o_pair = jnp.where(low_half, o[2 * j * blk:(2 * j + 1) * blk], o[(2 * j + 1) * blk:(2 * j + 2) * blk])
                o_ref[:, kh * grp * HD_A + j * lanes: kh * grp * HD_A + (j + 1) * lanes] = o_pair.astype(BF16)
    vo_ref[0] = kvc_ref[:, kw:]


def swa_prompt(z0, gq, gk, sinks, bsz, t):
    blk = 128
    nb = t // blk
    kvw = 2 * KVH_A * HD_A
    kv_col = Z0_KA // kvw
    grid_spec = pltpu.PrefetchScalarGridSpec(
        num_scalar_prefetch=1,
        grid=(bsz, nb),
        in_specs=[
            pl.BlockSpec((blk, H_A * HD_A), lambda b, n, s: (b * nb + n, 0)),
            pl.BlockSpec((blk, kvw), lambda b, n, s: (b * nb + jnp.maximum(n - 1, 0), kv_col)),
            pl.BlockSpec((blk, kvw), lambda b, n, s: (b * nb + n, kv_col)),
            pl.BlockSpec((1, 2 * HD_A), lambda b, n, s: (0, 0)),
            pl.BlockSpec((1, 2 * HD_A), lambda b, n, s: (0, 0)),
        ],
        out_specs=[
            pl.BlockSpec((blk, H_A * HD_A), lambda b, n, s: (b * nb + n, 0)),
            pl.BlockSpec((1, blk, KVH_A * HD_A), lambda b, n, s: (b, 0, 0)),
            pl.BlockSpec((1, blk, KVH_A * HD_A), lambda b, n, s: (b, 0, 0)),
        ],
    )
    return pl.pallas_call(
        _swa_prompt_kernel,
        out_shape=[
            jax.ShapeDtypeStruct((bsz * t, H_A * HD_A), BF16),
            jax.ShapeDtypeStruct((bsz, blk, KVH_A * HD_A), F32),
            jax.ShapeDtypeStruct((bsz, blk, KVH_A * HD_A), F32),
        ],
        grid_spec=grid_spec,
        compiler_params=_cp(("parallel", "arbitrary")),
        name="swa_prompt",
    )(sinks, z0, z0, z0, jnp.tile(gq, 2).reshape(1, 2 * HD_A), jnp.tile(gk, 2).reshape(1, 2 * HD_A))


def _swa_decode_kernel(q_ref, k_ref, v_ref, kt_ref, vt_ref, gq_ref, gk_ref, sink_ref, o_ref, kto_ref, vto_ref):
    nb = q_ref.shape[0]
    grp = H_A // KVH_A
    kw = KVH_A * HD_A
    buf_len = kt_ref.shape[2]
    gq = gq_ref[...]
    gk = gk_ref[...]
    sink = sink_ref[...]
    head = lax.broadcasted_iota(jnp.int32, (H_A, kw), 0)
    lane_blk = lax.broadcasted_iota(jnp.int32, (H_A, kw), 1) // HD_A
    own = lane_blk == head // grp
    last = lax.broadcasted_iota(jnp.int32, (kw, buf_len), 1) == buf_len - 1
    kn_rows = jnp.concatenate(
        [_rms_rows(k_ref[:, kh * HD_A:(kh + 1) * HD_A], gk) for kh in range(KVH_A)], axis=-1)
    v_rows = v_ref[...]
    kn_cols = kn_rows.T
    v_cols = v_rows.T
    scale = HD_A ** -0.5
    for i in range(nb):
        qn = _rms_rows(q_ref[i], gq)
        qblk = jnp.where(own, jnp.concatenate([qn] * KVH_A, axis=-1), 0.0)
        kt = kt_ref[i]
        vt = vt_ref[i]
        s = _dot(qblk.astype(BF16), kt.astype(BF16)) * scale
        s_new = jnp.sum(qblk * kn_rows[i:i + 1], axis=-1, keepdims=True) * scale
        m = jnp.maximum(jnp.maximum(jnp.max(s, axis=-1, keepdims=True), s_new), sink)
        e = jnp.exp(s - m)
        e_new = jnp.exp(s_new - m)
        den = jnp.sum(e, axis=-1, keepdims=True) + e_new + jnp.exp(sink - m)
        o_all = _dot_nt(e.astype(BF16), vt.astype(BF16)) + e_new * v_rows[i:i + 1]
        o_all = jnp.where(own, o_all, 0.0)
        o = o_all[:, 0:HD_A]
        for kh in range(1, KVH_A):
            o = o + o_all[:, kh * HD_A:(kh + 1) * HD_A]
        o_ref[i] = (o / den).astype(BF16)
        kto_ref[i] = jnp.where(last, kn_cols[:, i:i + 1], pltpu.roll(kt, buf_len - 1, 1))
        vto_ref[i] = jnp.where(last, v_cols[:, i:i + 1], pltpu.roll(vt, buf_len - 1, 1))


def swa_decode(q, k, v, kt, vt, gq, gk, sinks):
    bsz, kw, buf_len = kt.shape
    nb = 8
    rows = pl.BlockSpec((nb, kw), lambda i: (i, 0))
    heads = pl.BlockSpec((nb, H_A, HD_A), lambda i: (i, 0, 0))
    buf = pl.BlockSpec((nb, kw, buf_len), lambda i: (i, 0, 0))
    vec = pl.BlockSpec((1, HD_A), lambda i: (0, 0))
    return pl.pallas_call(
        _swa_decode_kernel,
        out_shape=[
            jax.ShapeDtypeStruct((bsz, H_A, HD_A), BF16),
            jax.ShapeDtypeStruct(kt.shape, F32),
            jax.ShapeDtypeStruct(vt.shape, F32),
        ],
        grid=(bsz // nb,),
        in_specs=[heads, rows, rows, buf, buf, vec, vec, pl.BlockSpec((H_A, 1), lambda i: (0, 0))],
        out_specs=[heads, buf, buf],
        compiler_params=_cp(("parallel",)),
        name="swa_decode",
    )(q, k, v, kt, vt, gq.reshape(1, HD_A), gk.reshape(1, HD_A), sinks.reshape(H_A, 1))


def _gla_logdec(glr, wg2_ref, bg2_ref):
    gl = _dot(glr.astype(BF16), wg2_ref[...]) + bg2_ref[...]
    return _log_sigmoid(gl) * (1.0 / GLA_GATE_NORM)


def _col(row8, i):
    return row8.T[:, i:i + 1]


def _gla_out(o, g_go, gout):
    return (_rms_rows(o, g_go) * (gout * jax.nn.sigmoid(gout))).astype(BF16)


def _gla_prompt_kernel(q_ref, k_ref, v_ref, gout_ref, glr_ref, wg2_ref, bg2_ref, ggo_ref, o_ref, so_ref, s_ref):
    c_idx = pl.program_id(1)
    tb = q_ref.shape[0]
    ck = GLA_CHUNK

    @pl.when(c_idx == 0)
    def _():
        s_ref[...] = jnp.zeros_like(s_ref)

    ld_all = _gla_logdec(glr_ref[...], wg2_ref, bg2_ref)
    ti = lax.broadcasted_iota(jnp.int32, (ck, ck), 0)
    si = lax.broadcasted_iota(jnp.int32, (ck, ck), 1)
    causal = si <= ti
    tri = jnp.where(causal, 1.0, 0.0).astype(BF16)
    g_go = ggo_ref[...]
    for c in range(tb // ck):
        rs = slice(c * ck, (c + 1) * ck)
        for h in range(H_B):
            ks = slice(h * DK_B, (h + 1) * DK_B)
            vs = slice(h * DV_B, (h + 1) * DV_B)
            hi, lo = _split_bf16(ld_all[rs, ks])
            b = _dot(tri, hi) + _dot(tri, lo)
            k = k_ref[rs, ks]
            v = v_ref[rs, vs].astype(BF16)
            qt = ((q_ref[rs, ks] * (DK_B ** -0.5)) * jnp.exp(b)).astype(BF16)
            kt = (k * jnp.exp(-b)).astype(BF16)
            att = jnp.where(causal, _dot_nt(qt, kt), 0.0)
            state = s_ref[h]
            o = _dot(att.astype(BF16), v) + _dot(qt, state.astype(BF16))
            b_last = b[ck - 1:ck, :]
            kd = (k * jnp.exp(b_last - b)).astype(BF16)
            d_state = _dot_tn(kd, v)
            decay = _col(jnp.broadcast_to(jnp.exp(b_last), (8, DK_B)), 0)
            s_ref[h] = decay * state + d_state
            o_ref[rs, vs] = _gla_out(o, g_go, gout_ref[rs, vs])
    so_ref[0] = s_ref[...]


def gla_prompt(z0, glr, wg2, bg2, g_go, bsz, t):
    tb = 256
    nt = t // tb
    qk_w = H_B * DK_B
    v_w = H_B * DV_B
    return pl.pallas_call(
        _gla_prompt_kernel,
        out_shape=[
            jax.ShapeDtypeStruct((bsz * t, v_w), BF16),
            jax.ShapeDtypeStruct((bsz, H_B, DK_B, DV_B), F32),
        ],
        grid=(bsz, nt),
        in_specs=[
            pl.BlockSpec((tb, qk_w), lambda b, c: (b * nt + c, Z0_QB // qk_w)),
            pl.BlockSpec((tb, qk_w), lambda b, c: (b * nt + c, Z0_KB // qk_w)),
            pl.BlockSpec((tb, v_w), lambda b, c: (b * nt + c, Z0_VB // v_w)),
            pl.BlockSpec((tb, v_w), lambda b, c: (b * nt + c, Z0_GOUT // v_w)),
            pl.BlockSpec((tb, 128), lambda b, c: (b * nt + c, 0)),
            pl.BlockSpec((128, qk_w), lambda b, c: (0, 0)),
            pl.BlockSpec((1, qk_w), lambda b, c: (0, 0)),
            pl.BlockSpec((1, DV_B), lambda b, c: (0, 0)),
        ],
        out_specs=[
            pl.BlockSpec((tb, v_w), lambda b, c: (b * nt + c, 0)),
            pl.BlockSpec((1, H_B, DK_B, DV_B), lambda b, c: (b, 0, 0, 0)),
        ],
        scratch_shapes=[pltpu.VMEM((H_B, DK_B, DV_B), F32)],
        compiler_params=_cp(("parallel", "arbitrary")),
        name="gla_prompt",
    )(z0, z0, z0, z0, glr, wg2, bg2.reshape(1, qk_w), g_go.reshape(1, DV_B))


def _gla_decode_kernel(q_ref, k_ref, v_ref, gout_ref, glr_ref, wg2_ref, bg2_ref, ggo_ref, s_ref, o_ref, so_ref):
    nb = q_ref.shape[0]
    ld = _gla_logdec(glr_ref[...], wg2_ref, bg2_ref)
    g_go = ggo_ref[...]
    for h in range(H_B):
        ks = slice(h * DK_B, (h + 1) * DK_B)
        vs = slice(h * DV_B, (h + 1) * DV_B)
        eg_t = jnp.exp(ld[:, ks]).T
        k_t = k_ref[:, ks].T
        q_t = (q_ref[:, ks] * (DK_B ** -0.5)).T
        for i in range(nb):
            state = eg_t[:, i:i + 1] * s_ref[i, h] + k_t[:, i:i + 1] * v_ref[i:i + 1, vs]
            so_ref[i, h] = state
            o = jnp.sum(q_t[:, i:i + 1] * state, axis=0, keepdims=True)
            o_ref[i:i + 1, vs] = _gla_out(o, g_go, gout_ref[i:i + 1, vs])


def gla_decode(q, k, v, gout, glr, wg2, bg2, g_go, state):
    bsz = q.shape[0]
    nb = 8
    qk_w = H_B * DK_B
    v_w = H_B * DV_B
    rows = lambda w: pl.BlockSpec((nb, w), lambda i: (i, 0))
    st = pl.BlockSpec((nb, H_B, DK_B, DV_B), lambda i: (i, 0, 0, 0))
    return pl.pallas_call(
        _gla_decode_kernel,
        out_shape=[
            jax.ShapeDtypeStruct((bsz, v_w), BF16),
            jax.ShapeDtypeStruct(state.shape, F32),
        ],
        grid=(bsz // nb,),
        in_specs=[
            rows(qk_w), rows(qk_w), rows(v_w), rows(v_w), rows(128),
            pl.BlockSpec((128, qk_w), lambda i: (0, 0)),
            pl.BlockSpec((1, qk_w), lambda i: (0, 0)),
            pl.BlockSpec((1, DV_B), lambda i: (0, 0)),
            st,
        ],
        out_specs=[rows(v_w), st],
        compiler_params=_cp(("parallel",)),
        name="gla_decode",
    )(q, k, v, gout, glr, wg2, bg2.reshape(1, qk_w), g_go.reshape(1, DV_B), state)


Z1_CX, Z1_CY, Z1_QD, Z1_KD, Z1_VD = 0, 1024, 2048, 3072, 3584


def _rg_gates(xc, wa_ref, ba_ref, wx_ref, bx_ref, lam_ref):
    xb = xc.astype(BF16)
    ra = jnp.concatenate(
        [_dot(xb[:, n * C_BLOCK:(n + 1) * C_BLOCK], wa_ref[n]) for n in range(C_BLOCKS)], axis=-1) + ba_ref[...]
    rx = jnp.concatenate(
        [_dot(xb[:, n * C_BLOCK:(n + 1) * C_BLOCK], wx_ref[n]) for n in range(C_BLOCKS)], axis=-1) + bx_ref[...]
    r_gate = jax.nn.sigmoid(ra)
    i_gate = jax.nn.sigmoid(rx)
    log_a = (-RG_C * r_gate) * _softplus(-lam_ref[...])
    a = jnp.exp(log_a)
    one_minus_a2 = -jnp.tanh(log_a) * (a * a + 1.0)
    u = jnp.sqrt(one_minus_a2) * (i_gate * xc)
    return a, u


def _shift_rows(x, d, fill):
    if d % 8 == 0:
        return jnp.concatenate([jnp.full((d,) + x.shape[1:], fill, x.dtype), x[:x.shape[0] - d]], axis=0)
    rows = lax.broadcasted_iota(jnp.int32, x.shape, 0)
    return jnp.where(rows < d, fill, pltpu.roll(x, d, 0))


def _rglru_prompt_kernel(cx_ref, cy_ref, cw_ref, cb_ref, wa_ref, ba_ref, wx_ref, bx_ref, lam_ref,
                         o_ref, tail_ref, hl_ref, prev_ref, hc_ref):
    t_idx = pl.program_id(1)
    tt = cx_ref.shape[0]

    @pl.when(t_idx == 0)
    def _():
        prev_ref[...] = jnp.zeros_like(prev_ref)
        hc_ref[...] = jnp.zeros_like(hc_ref)

    x = cx_ref[...]
    xp = jnp.concatenate([prev_ref[...], x], axis=0)
    xc = cb_ref[...] + cw_ref[CONV_W - 1:CONV_W, :] * x
    for j in range(CONV_W - 1):
        d = CONV_W - 1 - j
        xc = xc + cw_ref[j:j + 1, :] * pltpu.roll(xp, d, 0)[8:]
    prev_ref[...] = x[tt - 8:]
    tail_ref[0] = x[tt - 8:]

    a, u = _rg_gates(xc, wa_ref, ba_ref, wx_ref, bx_ref, lam_ref)
    d = 1
    while d < tt:
        u = a * _shift_rows(u, d, 0.0) + u
        a = a * _shift_rows(a, d, 1.0)
        d *= 2
    h = a * hc_ref[...] + u
    hc_ref[...] = h[tt - 1:]
    hl_ref[0] = h[tt - 1:]
    o_ref[...] = (h * jax.nn.gelu(cy_ref[...])).astype(BF16)


def rglru_prompt(z1, conv_w, conv_b, wa, ba, wx, bx, lam, bsz, t):
    tt = 256
    nt = t // tt
    c = C_WIDTH
    vec = pl.BlockSpec((1, c), lambda b, i: (0, 0))
    wblk = pl.BlockSpec((C_BLOCKS, C_BLOCK, C_BLOCK), lambda b, i: (0, 0, 0))
    return pl.pallas_call(
        _rglru_prompt_kernel,
        out_shape=[
            jax.ShapeDtypeStruct((bsz * t, c), BF16),
            jax.ShapeDtypeStruct((bsz, 8, c), F32),
            jax.ShapeDtypeStruct((bsz, 1, c), F32),
        ],
        grid=(bsz, nt),
        in_specs=[
            pl.BlockSpec((tt, c), lambda b, i: (b * nt + i, Z1_CX // c)),
            pl.BlockSpec((tt, c), lambda b, i: (b * nt + i, Z1_CY // c)),
            pl.BlockSpec((CONV_W, c), lambda b, i: (0, 0)),
            vec, wblk, vec, wblk, vec, vec,
        ],
        out_specs=[
            pl.BlockSpec((tt, c), lambda b, i: (b * nt + i, 0)),
            pl.BlockSpec((1, 8, c), lambda b, i: (b, 0, 0)),
            pl.BlockSpec((1, 1, c), lambda b, i: (b, 0, 0)),
        ],
        scratch_shapes=[pltpu.VMEM((8, c), F32), pltpu.VMEM((1, c), F32)],
        compiler_params=_cp(("parallel", "arbitrary")),
        name="rglru_prompt",
    )(z1, z1, conv_w, conv_b.reshape(1, c), wa, ba.reshape(1, c), wx, bx.reshape(1, c), lam.reshape(1, c))


def _rglru_decode_kernel(cx_ref, cy_ref, buf_ref, h0_ref, cw_ref, cb_ref, wa_ref, ba_ref, wx_ref, bx_ref, lam_ref,
                         o_ref, nbuf_ref, hl_ref):
    x = cx_ref[...]
    xc = cb_ref[...] + cw_ref[CONV_W - 1:CONV_W, :] * x
    for j in range(CONV_W - 1):
        xc = xc + cw_ref[j:j + 1, :] * buf_ref[j]
    for j in range(CONV_W - 2):
        nbuf_ref[j] = buf_ref[j + 1]
    nbuf_ref[CONV_W - 2] = x
    a, u = _rg_gates(xc, wa_ref, ba_ref, wx_ref, bx_ref, lam_ref)
    h = a * h0_ref[...] + u
    hl_ref[...] = h
    o_ref[...] = (h * jax.nn.gelu(cy_ref[...])).astype(BF16)


def rglru_decode(cx, cy, buf, h0, conv_w, conv_b, wa, ba, wx, bx, lam):
    bsz, c = cx.shape
    return pl.pallas_call(
        _rglru_decode_kernel,
        out_shape=[
            jax.ShapeDtypeStruct((bsz, c), BF16),
            jax.ShapeDtypeStruct(buf.shape, F32),
            jax.ShapeDtypeStruct((bsz, c), F32),
        ],
        compiler_params=pltpu.CompilerParams(vmem_limit_bytes=V7X_VMEM_LIMIT_BYTES),
        name="rglru_decode",
    )(cx, cy, buf, h0, conv_w, conv_b.reshape(1, c), wa, ba.reshape(1, c), wx, bx.reshape(1, c), lam.reshape(1, c))


def _strict_upper_ones(n):
    j = lax.broadcasted_iota(jnp.int32, (n, n), 0)
    s = lax.broadcasted_iota(jnp.int32, (n, n), 1)
    return jnp.where(j > s, 1.0, 0.0).astype(BF16)


def _sb_tile(z, later, surv, mask):
    sp = _softplus(z)
    l1m = -sp
    lsg = z - sp
    if mask is not None:
        l1m = jnp.where(mask, l1m, 0.0)
    hi, lo = _split_bf16(l1m)
    suffix = _dot(hi, later) + _dot(lo, later)
    w = jnp.exp(lsg + suffix + surv)
    if mask is not None:
        w = jnp.where(mask, w, 0.0)
    return w, jnp.sum(l1m, axis=-1, keepdims=True)


def _sb_prompt_kernel(bias_ref, q_ref, k_ref, v_ref, gq_ref, gk_ref, o_ref, ko_ref, kb_ref, vb_ref):
    kh = pl.program_id(1)
    i = pl.program_id(2)
    blk = q_ref.shape[0]
    grp = H_D // KVH_D

    @pl.when(i == 0)
    def _():
        kn = _rms_rows(k_ref[...], gk_ref[...])
        ko_ref[...] = kn
        kb_ref[...] = kn.astype(BF16)
        vb_ref[...] = v_ref[...].astype(BF16)

    gq = gq_ref[...]
    qs = jnp.concatenate(
        [_rms_rows(q_ref[:, g * HD_D:(g + 1) * HD_D], gq) for g in range(grp)], axis=0).astype(BF16)
    bias = jnp.concatenate(
        [jnp.full((blk, 1), bias_ref[kh * grp + g], F32) for g in range(grp)], axis=0)
    later = _strict_upper_ones(blk)
    scale = HD_D ** -0.5

    def tile(j, mask, acc, surv):
        off = pl.multiple_of(j * blk, blk)
        z = _dot_nt(qs, kb_ref[pl.ds(off, blk), :]) * scale + bias
        w, tot = _sb_tile(z, later, surv, mask)
        return acc + _dot(w.astype(BF16), vb_ref[pl.ds(off, blk), :]), surv + tot

    rows = lax.broadcasted_iota(jnp.int32, (grp * blk, blk), 0) & (blk - 1)
    cols = lax.broadcasted_iota(jnp.int32, (grp * blk, blk), 1)
    acc, surv = tile(i, cols < rows, jnp.zeros((grp * blk, HD_D), F32), jnp.zeros((grp * blk, 1), F32))

    def pair(step, carry):
        j = i - 1 - 2 * step
        return tile(j - 1, None, *tile(j, None, *carry))

    def last(step, carry):
        return tile(0, None, *carry)

    n_pairs = i // 2
    carry = lax.fori_loop(0, n_pairs, pair, (acc, surv))
    acc, _ = lax.fori_loop(0, i - 2 * n_pairs, last, carry)
    for g in range(grp):
        o_ref[:, g * HD_D:(g + 1) * HD_D] = acc[g * blk:(g + 1) * blk].astype(BF16)


def sb_prompt(z1, gq, gk, bias, bsz, t):
    blk = SB_PROMPT_TILE
    nb = t // blk
    grp = H_D // KVH_D
    qw = grp * HD_D
    grid_spec = pltpu.PrefetchScalarGridSpec(
        num_scalar_prefetch=1,
        grid=(bsz, KVH_D, nb),
        in_specs=[
            pl.BlockSpec((blk, qw), lambda b, k, i, s: (b * nb + i, Z1_QD // qw + k)),
            pl.BlockSpec((t, HD_D), lambda b, k, i, s: (b, Z1_KD // HD_D + k)),
            pl.BlockSpec((t, HD_D), lambda b, k, i, s: (b, Z1_VD // HD_D + k)),
            pl.BlockSpec((1, HD_D), lambda b, k, i, s: (0, 0)),
            pl.BlockSpec((1, HD_D), lambda b, k, i, s: (0, 0)),
        ],
        out_specs=[
            pl.BlockSpec((blk, qw), lambda b, k, i, s: (b * nb + i, k)),
            pl.BlockSpec((t, HD_D), lambda b, k, i, s: (b, k)),
        ],
        scratch_shapes=[pltpu.VMEM((t, HD_D), BF16), pltpu.VMEM((t, HD_D), BF16)],
    )
    return pl.pallas_call(
        _sb_prompt_kernel,
        out_shape=[
            jax.ShapeDtypeStruct((bsz * t, H_D * HD_D), BF16),
            jax.ShapeDtypeStruct((bsz * t, KVH_D * HD_D), F32),
        ],
        grid_spec=grid_spec,
        compiler_params=_cp(("parallel", "parallel", "arbitrary")),
        name="sb_prompt",
    )(bias, z1, z1, z1, gq.reshape(1, HD_D), gk.reshape(1, HD_D))


def _sb_decode_kernel(n_pages, pt_ref, bias_ref, q_ref, k_ref, v_ref, gq_ref, gk_ref, ck_hbm, cv_hbm,
                      o_ref, ko_ref, kbuf, vbuf, sem):
    b = pl.program_id(0)
    nb = pl.num_programs(0)
    npg = SB_PAGES_PER_STEP
    n_chunks = n_pages // npg
    prow = PAGE * KVH_D
    grp = H_D // KVH_D
    scale = HD_D ** -0.5

    def page_copies(bb, chunk, slot):
        cps = []
        for p in range(npg):
            page = pt_ref[bb * n_pages + chunk * npg + p]
            dst = pl.ds(p * prow, prow)
            cps.append(pltpu.make_async_copy(ck_hbm.at[page], kbuf.at[slot, dst], sem.at[0, slot]))
            cps.append(pltpu.make_async_copy(cv_hbm.at[page], vbuf.at[slot, dst], sem.at[1, slot]))
        return cps

    @pl.when(b == 0)
    def _():
        for cp in page_copies(0, n_chunks - 1, 0):
            cp.start()

    bias = jnp.concatenate([jnp.full((1, 1), bias_ref[h], F32) for h in range(H_D)], axis=0)
    qn = _rms_rows(q_ref[0], gq_ref[...])
    qb = qn.astype(BF16)
    kn = _rms_rows(k_ref[0], gk_ref[...])
    ko_ref[0] = kn
    kn_sel = jnp.concatenate([kn[h // grp:h // grp + 1] for h in range(H_D)], axis=0)
    v_sel = jnp.concatenate([v_ref[0, h // grp:h // grp + 1] for h in range(H_D)], axis=0)
    z0 = jnp.sum(qn * kn_sel, axis=-1, keepdims=True) * scale + bias
    visible = jnp.zeros((H_D, 1), jnp.int32) < jnp.zeros((H_D, 1), jnp.int32)
    sp0 = _softplus(z0)
    acc = jnp.where(visible, jnp.exp(z0 - sp0), 0.0) * v_sel
    surv = jnp.where(visible, -sp0, 0.0)

    later = _strict_upper_ones(prow)
    row = lax.broadcasted_iota(jnp.int32, (npg * H_D, prow), 0)
    col = lax.broadcasted_iota(jnp.int32, (npg * H_D, prow), 1)
    valid = (col & (KVH_D - 1)) == ((row & (H_D - 1)) // grp)
    bias_r = jnp.concatenate([bias] * npg, axis=0)

    for i in range(n_chunks):
        slot = i % 2
        if i + 1 < n_chunks:
            for cp in page_copies(b, n_chunks - 2 - i, 1 - slot):
                cp.start()
        else:
            @pl.when(b + 1 < nb)
            def _():
                for cp in page_copies(b + 1, n_chunks - 1, 1 - slot):
                    cp.start()
        for cp in page_copies(b, n_chunks - 1 - i, slot):
            cp.wait()

        z = _dot_nt(qb, kbuf[slot].astype(BF16))
        z = jnp.concatenate([z[:, p * prow:(p + 1) * prow] for p in range(npg)], axis=0) * scale + bias_r
        sp = _softplus(z)
        l1m = jnp.where(valid, -sp, 0.0)
        hi, lo = _split_bf16(l1m)
        suffix = _dot(hi, later) + _dot(lo, later)
        tot = jnp.sum(l1m, axis=-1, keepdims=True)
        survs = [None] * npg
        for p in reversed(range(npg)):
            survs[p] = surv
            surv = surv + tot[p * H_D:(p + 1) * H_D]
        w = jnp.where(valid, jnp.exp((z - sp) + suffix + jnp.concatenate(survs, axis=0)), 0.0)
        w = jnp.concatenate([w[p * H_D:(p + 1) * H_D] for p in range(npg)], axis=1).astype(BF16)
        acc = acc + _dot(w, vbuf[slot].astype(BF16))

    o_ref[0] = acc.astype(BF16)


def sb_decode(q, k, v, gq, gk, bias, cache_k, cache_v, page_table):
    bsz = q.shape[0]
    n_pages = page_table.shape[1]
    npg = SB_PAGES_PER_STEP
    assert n_pages % (2 * npg) == 0
    prow = PAGE * KVH_D
    grid_spec = pltpu.PrefetchScalarGridSpec(
        num_scalar_prefetch=2,
        grid=(bsz,),
        in_specs=[
            pl.BlockSpec((1, H_D, HD_D), lambda b, pt, bias_: (b, 0, 0)),
            pl.BlockSpec((1, KVH_D, HD_D), lambda b, pt, bias_: (b, 0, 0)),
            pl.BlockSpec((1, KVH_D, HD_D), lambda b, pt, bias_: (b, 0, 0)),
            pl.BlockSpec((1, HD_D), lambda b, pt, bias_: (0, 0)),
            pl.BlockSpec((1, HD_D), lambda b, pt, bias_: (0, 0)),
            pl.BlockSpec(memory_space=pl.ANY),
            pl.BlockSpec(memory_space=pl.ANY),
        ],
        out_specs=[
            pl.BlockSpec((1, H_D, HD_D), lambda b, pt, bias_: (b, 0, 0)),
            pl.BlockSpec((1, KVH_D, HD_D), lambda b, pt, bias_: (b, 0, 0)),
        ],
        scratch_shapes=[
            pltpu.VMEM((2, npg * prow, HD_D), F32),
            pltpu.VMEM((2, npg * prow, HD_D), F32),
            pltpu.SemaphoreType.DMA((2, 2)),
        ],
    )
    return pl.pallas_call(
        functools.partial(_sb_decode_kernel, n_pages),
        out_shape=[
            jax.ShapeDtypeStruct((bsz, H_D, HD_D), BF16),
            jax.ShapeDtypeStruct((bsz, KVH_D, HD_D), F32),
        ],
        grid_spec=grid_spec,
        compiler_params=_cp(("arbitrary",)),
        name="sb_decode",
    )(page_table.reshape(-1), bias, q, k, v, gq.reshape(1, HD_D), gk.reshape(1, HD_D), cache_k, cache_v)


def kernel(x_prompt, x_sample, state_swa_k, state_swa_v, state_gla, state_conv, state_lru, cache_sb_k, cache_sb_v, page_table, p_prompt, p_sample, g_mix, g_ffn, w_ffn_gate, w_ffn_up, w_ffn_down, g_ple, w_ple_gate, w_ple_proj, w_in_even, g_qnorm_a, g_knorm_a, sinks_a, w_gla_gate2, b_gla_gate2, g_gla_out, w_out_even, w_in_odd, conv_w, conv_b, w_rg_a, b_rg_a, w_rg_x, b_rg_x, lru_lambda, g_qnorm_d, g_knorm_d, sb_bias, w_out_odd):
    bsz, t, d = x_prompt.shape
    dbs = x_sample.shape[0]
    n_p = bsz * t
    depth = g_mix.shape[0]
    ple = p_prompt.shape[-1]

    h = jnp.concatenate([x_prompt.reshape(n_p, d), x_sample.reshape(dbs, d)], axis=0)
    p_all = jnp.concatenate([p_prompt.reshape(depth, n_p, ple), p_sample.reshape(depth, dbs, ple)], axis=1)

    ffn_w = (w_ffn_gate.astype(BF16), w_ffn_up.astype(BF16), w_ffn_down.astype(BF16))
    ple_w = (w_ple_gate.astype(BF16), w_ple_proj.astype(BF16))

    def dense_tail(h, i, n_tail=0):
        h = ffn_residual(h, g_ffn[i], *ffn_w, i, 512, f"ffn_{i}")
        return ple_residual(h, g_ple[i], p_all, *ple_w, i, f"ple_{i}", n_tail)

    w = w_in_even[0]
    qa, ka, va, qb, kb, vb, glr_w, gout = jnp.split(
        w, [1024, 1280, 1536, 2048, 2560, 3584, 3600], axis=1)
    w0 = jnp.concatenate([qa, vb, gout, ka, va, qb, kb], axis=1).astype(BF16)
    w0_glr = jnp.pad(glr_w, ((0, 0), (0, 128 - GLA_RANK))).astype(BF16)
    z0, glr = norm_matmul(h, g_mix[0], w0, 1536, "in_proj_even", w_side=w0_glr)
    wg2 = jnp.pad(w_gla_gate2[0], ((0, 128 - GLA_RANK), (0, 0))).astype(BF16)

    oa_p, swa_k_p, swa_v_p = swa_prompt(z0, g_qnorm_a[0], g_knorm_a[0], sinks_a[0], bsz, t)
    ob_p, gla_p = gla_prompt(z0, glr, wg2, b_gla_gate2[0], g_gla_out[0], bsz, t)

    zd = z0[n_p:]
    kw_a = KVH_A * HD_A
    to_keys_last = lambda s: jnp.transpose(s, (0, 2, 3, 1)).reshape(dbs, kw_a, -1)
    from_keys_last = lambda s: jnp.transpose(s.reshape(dbs, KVH_A, HD_A, -1), (0, 3, 1, 2))[None]
    oa_d, swa_kt_d, swa_vt_d = swa_decode(
        zd[:, Z0_QA:Z0_QA + 1024].reshape(dbs, H_A, HD_A),
        zd[:, Z0_KA:Z0_KA + kw_a], zd[:, Z0_VA:Z0_VA + kw_a],
        to_keys_last(state_swa_k[0]), to_keys_last(state_swa_v[0]),
        g_qnorm_a[0], g_knorm_a[0], sinks_a[0])
    ob_d, gla_d = gla_decode(
        zd[:, Z0_QB:Z0_QB + 512], zd[:, Z0_KB:Z0_KB + 512], zd[:, Z0_VB:Z0_VB + 1024],
        zd[:, Z0_GOUT:Z0_GOUT + 1024], glr[n_p:], wg2, b_gla_gate2[0], g_gla_out[0], state_gla[0])

    wo = w_out_even[0].astype(BF16)
    h = proj_residual(h, oa_p, ob_p, oa_d.reshape(dbs, 1024), ob_d, wo[:1024], wo[1024:], "out_proj_even")
    h = dense_tail(h, 0)

    z1 = norm_matmul(h, g_mix[1], w_in_odd[0].astype(BF16), 1024, "in_proj_odd")
    wa = w_rg_a[0].astype(BF16)
    wx = w_rg_x[0].astype(BF16)
    oc_p, conv_tail, lru_p = rglru_prompt(z1, conv_w[0], conv_b[0], wa, b_rg_a[0], wx, b_rg_x[0], lru_lambda[0], bsz, t)
    od_p, sb_k_p = sb_prompt(z1, g_qnorm_d[0], g_knorm_d[0], sb_bias[0], bsz, t)

    zd = z1[n_p:]
    oc_d, conv_d, lru_d = rglru_decode(
        zd[:, Z1_CX:Z1_CX + 1024], zd[:, Z1_CY:Z1_CY + 1024],
        jnp.swapaxes(state_conv[0], 0, 1), state_lru[0],
        conv_w[0], conv_b[0], wa, b_rg_a[0], wx, b_rg_x[0], lru_lambda[0])
    kvw = KVH_D * HD_D
    n_phys = cache_sb_k.shape[1]
    od_d, sb_k_d = sb_decode(
        zd[:, Z1_QD:Z1_QD + 1024].reshape(dbs, H_D, HD_D),
        zd[:, Z1_KD:Z1_KD + kvw].reshape(dbs, KVH_D, HD_D),
        zd[:, Z1_VD:Z1_VD + kvw].reshape(dbs, KVH_D, HD_D),
        g_qnorm_d[0], g_knorm_d[0], sb_bias[0],
        cache_sb_k[0].reshape(n_phys, PAGE * KVH_D, HD_D), cache_sb_v[0].reshape(n_phys, PAGE * KVH_D, HD_D),
        page_table)

    wo = w_out_odd[0].astype(BF16)
    h = proj_residual(h, oc_p, od_p, oc_d, od_d.reshape(dbs, 1024), wo[:1024], wo[1024:], "out_proj_odd")
    y_prompt, y_sample = dense_tail(h, 1, n_tail=dbs)

    n_keep = min(WINDOW, t)
    return (
        y_prompt.reshape(bsz, t, d), y_sample.reshape(dbs, 1, d),
        swa_k_p.reshape(1, bsz, n_keep, KVH_A, HD_A), swa_v_p.reshape(1, bsz, n_keep, KVH_A, HD_A),
        gla_p[None],
        conv_tail[:, 8 - (CONV_W - 1):][None], lru_p.reshape(1, bsz, C_WIDTH),
        sb_k_p.reshape(1, bsz, t, KVH_D, HD_D), z1[:n_p, Z1_VD:].reshape(1, bsz, t, KVH_D, HD_D),
        from_keys_last(swa_kt_d), from_keys_last(swa_vt_d),
        gla_d[None],
        jnp.swapaxes(conv_d, 0, 1)[None], lru_d[None],
        sb_k_d.reshape(1, dbs, 1, KVH_D, HD_D), zd[:, Z1_VD:].reshape(1, dbs, 1, KVH_D, HD_D),
    )
```

```python
import functools

import jax
import jax.numpy as jnp
from jax import lax
from jax.experimental import pallas as pl
from jax.experimental.pallas import tpu as pltpu

F32 = jnp.float32
BF16 = jnp.bfloat16
EPS = 1e-6

V7X_VMEM_LIMIT_BYTES = 56 * 1024 * 1024

H_A, KVH_A, HD_A, WINDOW = 16, 4, 64, 128
H_B, DK_B, DV_B, GLA_RANK, GLA_GATE_NORM, GLA_CHUNK = 4, 128, 256, 16, 16.0, 64
C_WIDTH, C_BLOCKS, CONV_W, RG_C = 1024, 8, 4, 8.0
C_BLOCK = C_WIDTH // C_BLOCKS
H_D, KVH_D, HD_D = 8, 4, 128
PAGE = 128

ROW_TILE = 640
WIDE_ROW_TILE = 1040
SB_PROMPT_TILE = 256
SB_PAGES_PER_STEP = 16


def _cp(sem):
    return pltpu.CompilerParams(dimension_semantics=sem, vmem_limit_bytes=V7X_VMEM_LIMIT_BYTES)


def _rms_rows(x, g):
    r = lax.rsqrt(jnp.mean(x * x, axis=-1, keepdims=True) + EPS)
    return (x * r) * g


def _softplus(x):
    return jnp.maximum(x, 0.0) + jnp.log(1.0 + jnp.exp(-jnp.abs(x)))


def _log_sigmoid(x):
    return jnp.minimum(x, 0.0) - jnp.log(1.0 + jnp.exp(-jnp.abs(x)))


def _split_bf16(x):
    hi = x.astype(BF16)
    lo = (x - hi.astype(F32)).astype(BF16)
    return hi, lo


def _dot(a, b):
    return jnp.dot(a, b, preferred_element_type=F32)


def _dot_nt(a, b):
    return lax.dot_general(a, b, (((1,), (1,)), ((), ())), preferred_element_type=F32)


def _dot_tn(a, b):
    return lax.dot_general(a, b, (((0,), (0,)), ((), ())), preferred_element_type=F32)


def _norm_matmul_kernel(has_tail, has_side, x_ref, g_ref, w_ref, *rest):
    rest = list(rest)
    xt_ref = rest.pop(0) if has_tail else None
    ws_ref = rest.pop(0) if has_side else None
    o_ref = rest.pop(0)
    os_ref = rest.pop(0) if has_side else None
    xn_ref, = rest

    @pl.when(pl.program_id(1) == 0)
    def _():
        xn_ref[...] = _rms_rows(x_ref[...], g_ref[...]).astype(BF16)
        if has_tail:
            @pl.when(pl.program_id(0) == pl.num_programs(0) - 1)
            def _():
                nt = xt_ref.shape[0]
                xn_ref[xn_ref.shape[0] - nt:, :] = _rms_rows(xt_ref[...], g_ref[...]).astype(BF16)
        if has_side:
            os_ref[...] = _dot(xn_ref[...], ws_ref[...])

    o_ref[...] = _dot(xn_ref[...], w_ref[...])


def norm_matmul(x, g, w, tn, name, w_side=None, x_tail=None):
    d = x.shape[1]
    nt = 0 if x_tail is None else x_tail.shape[0]
    n = x.shape[0] + nt
    nout = w.shape[1]
    tm = WIDE_ROW_TILE
    assert n % tm == 0 and (nt == 0 or x.shape[0] % tm + nt == tm)
    in_specs = [
        pl.BlockSpec((tm, d), lambda i, j: (i, 0)),
        pl.BlockSpec((1, d), lambda i, j: (0, 0)),
        pl.BlockSpec((d, tn), lambda i, j: (0, j)),
    ]
    out_shape = [jax.ShapeDtypeStruct((n, nout), F32)]
    out_specs = [pl.BlockSpec((tm, tn), lambda i, j: (i, j))]
    args = [x, g.reshape(1, d), w]
    if x_tail is not None:
        in_specs.append(pl.BlockSpec((nt, d), lambda i, j: (0, 0)))
        args.append(x_tail)
    if w_side is not None:
        ns = w_side.shape[1]
        in_specs.append(pl.BlockSpec((d, ns), lambda i, j: (0, 0)))
        out_shape.append(jax.ShapeDtypeStruct((n, ns), F32))
        out_specs.append(pl.BlockSpec((tm, ns), lambda i, j: (i, 0)))
        args.append(w_side)
    out = pl.pallas_call(
        functools.partial(_norm_matmul_kernel, x_tail is not None, w_side is not None),
        out_shape=out_shape,
        grid=(n // tm, nout // tn),
        in_specs=in_specs,
        out_specs=out_specs,
        scratch_shapes=[pltpu.VMEM((tm, d), BF16)],
        compiler_params=_cp(("parallel", "arbitrary")),
        name=name,
    )(*args)
    return out if w_side is not None else out[0]


def _proj_res_kernel(h_ref, a_ref, b_ref, at_ref, bt_ref, wa_ref, wb_ref, *rest):
    ht_ref, o_ref = rest if len(rest) == 2 else (None, rest[0])
    tm = h_ref.shape[0]
    nt = at_ref.shape[0]
    o_ref[...] = h_ref[...] + (_dot(a_ref[...], wa_ref[...]) + _dot(b_ref[...], wb_ref[...]))

    @pl.when(pl.program_id(0) == pl.num_programs(0) - 1)
    def _():
        tail = _dot(at_ref[...], wa_ref[...]) + _dot(bt_ref[...], wb_ref[...])
        h_tail = h_ref[tm - nt:, :] if ht_ref is None else ht_ref[...]
        o_ref[tm - nt:, :] = h_tail + tail


def proj_residual(h, a, b, a_tail, b_tail, wa, wb, name, h_tail=None):
    d = h.shape[1]
    ka, kb = a.shape[1], b.shape[1]
    nt = a_tail.shape[0]
    n = a.shape[0] + nt
    tm = ROW_TILE
    assert h.shape[0] == (n if h_tail is None else a.shape[0]) and n % tm == 0 and a.shape[0] % tm + nt == tm
    in_specs = [
        pl.BlockSpec((tm, d), lambda i: (i, 0)),
        pl.BlockSpec((tm, ka), lambda i: (i, 0)),
        pl.BlockSpec((tm, kb), lambda i: (i, 0)),
        pl.BlockSpec((nt, ka), lambda i: (0, 0)),
        pl.BlockSpec((nt, kb), lambda i: (0, 0)),
        pl.BlockSpec((ka, d), lambda i: (0, 0)),
        pl.BlockSpec((kb, d), lambda i: (0, 0)),
    ]
    args = [h, a, b, a_tail, b_tail, wa, wb]
    if h_tail is not None:
        in_specs.append(pl.BlockSpec((nt, d), lambda i: (0, 0)))
        args.append(h_tail)
    return pl.pallas_call(
        _proj_res_kernel,
        out_shape=jax.ShapeDtypeStruct((n, d), F32),
        grid=(n // tm,),
        in_specs=in_specs,
        out_specs=pl.BlockSpec((tm, d), lambda i: (i, 0)),
        compiler_params=_cp(("parallel",)),
        name=name,
    )(*args)


def _ffn_kernel(h_ref, g_ref, wg_ref, wu_ref, wd_ref, o_ref, hn_ref):
    @pl.when(pl.program_id(1) == 0)
    def _():
        h = h_ref[...]
        hn_ref[...] = _rms_rows(h, g_ref[...]).astype(BF16)
        o_ref[...] = h

    hn = hn_ref[...]
    gate = _dot(hn, wg_ref[...])
    up = _dot(hn, wu_ref[...])
    act = (gate * jax.nn.sigmoid(gate)) * up
    o_ref[...] += _dot(act.astype(BF16), wd_ref[...])


def ffn_residual(h, g, wg, wu, wd, layer, tf, name):
    n, d = h.shape
    dff = wg.shape[2]
    tm = WIDE_ROW_TILE
    return pl.pallas_call(
        _ffn_kernel,
        out_shape=jax.ShapeDtypeStruct((n, d), F32),
        grid=(n // tm, dff // tf),
        in_specs=[
            pl.BlockSpec((tm, d), lambda i, f: (i, 0)),
            pl.BlockSpec((1, d), lambda i, f: (0, 0)),
            pl.BlockSpec((None, d, tf), lambda i, f: (layer, 0, f)),
            pl.BlockSpec((None, d, tf), lambda i, f: (layer, 0, f)),
            pl.BlockSpec((None, tf, d), lambda i, f: (layer, f, 0)),
        ],
        out_specs=pl.BlockSpec((tm, d), lambda i, f: (i, 0)),
        scratch_shapes=[pltpu.VMEM((tm, d), BF16)],
        compiler_params=_cp(("parallel", "arbitrary")),
        name=name,
    )(h, g.reshape(1, d), wg, wu, wd)


def _ple_kernel(h_ref, g_ref, p_ref, wg_ref, wp_ref, o_ref, *tail_ref):
    h = h_ref[...]
    hn = _rms_rows(h, g_ref[...]).astype(BF16)
    gate = jax.nn.sigmoid(_dot(hn, wg_ref[...]))
    proj = _dot(p_ref[...].astype(BF16), wp_ref[...])
    y = h + gate * proj
    o_ref[...] = y
    if tail_ref:
        @pl.when(pl.program_id(0) == pl.num_programs(0) - 1)
        def _():
            nt = tail_ref[0].shape[0]
            tail_ref[0][...] = y[y.shape[0] - nt:, :]


def ple_residual(h, g, p, wg, wp, layer, name, n_tail=0):
    n, d = h.shape
    dp = p.shape[2]
    tm = ROW_TILE
    assert n % tm == 0 and (n_tail == 0 or (n - n_tail) % tm + n_tail == tm)
    out_shape = [jax.ShapeDtypeStruct((n - n_tail, d), F32)]
    out_specs = [pl.BlockSpec((tm, d), lambda i: (i, 0))]
    if n_tail:
        out_shape.append(jax.ShapeDtypeStruct((n_tail, d), F32))
        out_specs.append(pl.BlockSpec((n_tail, d), lambda i: (0, 0)))
    out = pl.pallas_call(
        _ple_kernel,
        out_shape=out_shape,
        grid=(n // tm,),
        in_specs=[
            pl.BlockSpec((tm, d), lambda i: (i, 0)),
            pl.BlockSpec((1, d), lambda i: (0, 0)),
            pl.BlockSpec((None, tm, dp), lambda i: (layer, i, 0)),
            pl.BlockSpec((None, d, d), lambda i: (layer, 0, 0)),
            pl.BlockSpec((None, dp, d), lambda i: (layer, 0, 0)),
        ],
        out_specs=out_specs,
        compiler_params=_cp(("arbitrary",)),
        name=name,
    )(h, g.reshape(1, d), p, wg, wp)
    return out if n_tail else out[0]


Z0_QA, Z0_VB, Z0_GOUT, Z0_KA, Z0_VA, Z0_QB, Z0_KB = 0, 1024, 2048, 3072, 3328, 3584, 4096
Z0_WIDTH = 4608


def _swa_prompt_kernel(sink_ref, q_ref, kvp_ref, kvc_ref, gq_ref, gk_ref, o_ref, ko_ref, vo_ref):
    n = pl.program_id(1)
    blk = q_ref.shape[0]
    grp = H_A // KVH_A
    kw = KVH_A * HD_A
    lanes = 2 * HD_A
    gq = gq_ref[...]
    gk = gk_ref[...]
    kv = jnp.concatenate([kvp_ref[...], kvc_ref[...]], axis=0)
    rows = lax.broadcasted_iota(jnp.int32, (grp * blk, 2 * blk), 0)
    cols = lax.broadcasted_iota(jnp.int32, (grp * blk, 2 * blk), 1)
    diff = (rows & (blk - 1)) - cols + blk
    lo = jnp.where(n == 0, blk, 0)
    mask = (diff >= 0) & (diff <= WINDOW) & (cols >= lo)

    r = lax.broadcasted_iota(jnp.int32, (lanes, lanes), 0)
    c = lax.broadcasted_iota(jnp.int32, (lanes, lanes), 1)
    head_sum = jnp.where(r // HD_A == c // HD_A, 1.0, 0.0).astype(BF16)
    spread = [jnp.where(r == (c & (HD_A - 1)) + half * HD_A, 1.0, 0.0).astype(BF16) for half in range(2)]
    ones_kv = jnp.ones((2 * blk, lanes), BF16)
    low_half = lax.broadcasted_iota(jnp.int32, (blk, lanes), 1) < HD_A

    def head_norm(x, g):
        hi, lo_ = _split_bf16(x * x)
        ss = _dot(hi, head_sum) + _dot(lo_, head_sum)
        return (x * lax.rsqrt(ss * (1.0 / HD_A) + EPS)) * g

    for pair in range(KVH_A // 2):
        ps = slice(pair * lanes, (pair + 1) * lanes)
        kn_pair = head_norm(kv[:, ps], gk)
        ko_ref[0, :, ps] = kn_pair[blk:]
        kn_b = kn_pair.astype(BF16)
        v_b = kv[:, kw + pair * lanes: kw + (pair + 1) * lanes].astype(BF16)
        for half in range(2):
            kh = 2 * pair + half
            kd = _dot(kn_b, spread[half]).astype(BF16)
            vd = _dot(v_b, spread[half]).astype(BF16)
            s_parts = []
            for j in range(grp // 2):
                qs = slice(kh * grp * HD_A + j * lanes, kh * grp * HD_A + (j + 1) * lanes)
                qn = head_norm(q_ref[:, qs], gq)
                s_parts.append(_dot_nt(jnp.where(low_half, qn, 0.0).astype(BF16), kd))
                s_parts.append(_dot_nt(jnp.where(low_half, 0.0, qn).astype(BF16), kd))
            s = jnp.concatenate(s_parts, axis=0) * (HD_A ** -0.5)
            s = jnp.where(mask, s, -jnp.inf)
            sink = jnp.concatenate(
                [jnp.full((blk, 1), sink_ref[kh * grp + g], F32) for g in range(grp)], axis=0)
            m = jnp.maximum(jnp.max(s, axis=-1, keepdims=True), sink)
            e_hi, e_lo = _split_bf16(jnp.exp(s - m))
            den = _dot(e_hi, ones_kv) + _dot(e_lo, ones_kv) + jnp.exp(sink - m)
            o = _dot(e_hi, vd) / den
            for j in range(grp // 2):
                o_pair = jnp.where(low_half, o[2 * j * blk:(2 * j + 1) * blk], o[(2 * j + 1) * blk:(2 * j + 2) * blk])
                o_ref[:, kh * grp * HD_A + j * lanes: kh * grp * HD_A + (j + 1) * lanes] = o_pair.astype(BF16)
    vo_ref[0] = kvc_ref[:, kw:]


def swa_prompt(z0, gq, gk, sinks, bsz, t):
    blk = 128
    nb = t // blk
    kvw = 2 * KVH_A * HD_A
    kv_col = Z0_KA // kvw
    grid_spec = pltpu.PrefetchScalarGridSpec(
        num_scalar_prefetch=1,
        grid=(bsz, nb),
        in_specs=[
            pl.BlockSpec((blk, H_A * HD_A), lambda b, n, s: (b * nb + n, 0)),
            pl.BlockSpec((blk, kvw), lambda b, n, s: (b * nb + jnp.maximum(n - 1, 0), kv_col)),
            pl.BlockSpec((blk, kvw), lambda b, n, s: (b * nb + n, kv_col)),
            pl.BlockSpec((1, 2 * HD_A), lambda b, n, s: (0, 0)),
            pl.BlockSpec((1, 2 * HD_A), lambda b, n, s: (0, 0)),
        ],
        out_specs=[
            pl.BlockSpec((blk, H_A * HD_A), lambda b, n, s: (b * nb + n, 0)),
            pl.BlockSpec((1, blk, KVH_A * HD_A), lambda b, n, s: (b, 0, 0)),
            pl.BlockSpec((1, blk, KVH_A * HD_A), lambda b, n, s: (b, 0, 0)),
        ],
    )
    return pl.pallas_call(
        _swa_prompt_kernel,
        out_shape=[
            jax.ShapeDtypeStruct((bsz * t, H_A * HD_A), BF16),
            jax.ShapeDtypeStruct((bsz, blk, KVH_A * HD_A), F32),
            jax.ShapeDtypeStruct((bsz, blk, KVH_A * HD_A), F32),
        ],
        grid_spec=grid_spec,
        compiler_params=_cp(("parallel", "arbitrary")),
        name="swa_prompt",
    )(sinks, z0, z0, z0, jnp.tile(gq, 2).reshape(1, 2 * HD_A), jnp.tile(gk, 2).reshape(1, 2 * HD_A))


def _swa_decode_kernel(q_ref, k_ref, v_ref, kt_ref, vt_ref, gq_ref, gk_ref, sink_ref, o_ref, kto_ref, vto_ref):
    nb = q_ref.shape[0]
    grp = H_A // KVH_A
    kw = KVH_A * HD_A
    buf_len = kt_ref.shape[2]
    gq = gq_ref[...]
    gk = gk_ref[...]
    sink = sink_ref[...]
    head = lax.broadcasted_iota(jnp.int32, (H_A, kw), 0)
    lane_blk = lax.broadcasted_iota(jnp.int32, (H_A, kw), 1) // HD_A
    own = lane_blk == head // grp
    last = lax.broadcasted_iota(jnp.int32, (kw, buf_len), 1) == buf_len - 1
    kn_rows = jnp.concatenate(
        [_rms_rows(k_ref[:, kh * HD_A:(kh + 1) * HD_A], gk) for kh in range(KVH_A)], axis=-1)
    v_rows = v_ref[...]
    kn_cols = kn_rows.T
    v_cols = v_rows.T
    scale = HD_A ** -0.5
    qn = _rms_rows(q_ref[...].reshape(nb * H_A, HD_A), gq)
    own_all = jnp.concatenate([own] * nb, axis=0)
    qblk = jnp.where(own_all, jnp.concatenate([qn] * KVH_A, axis=-1), 0.0)
    qblk_b = qblk.astype(BF16)
    rows = [slice(i * H_A, (i + 1) * H_A) for i in range(nb)]
    s = jnp.concatenate([_dot(qblk_b[rows[i]], kt_ref[i].astype(BF16)) for i in range(nb)], axis=0) * scale
    k_new = jnp.concatenate([jnp.broadcast_to(kn_rows[i:i + 1], (H_A, kw)) for i in range(nb)], axis=0)
    v_new = jnp.concatenate([jnp.broadcast_to(v_rows[i:i + 1], (H_A, kw)) for i in range(nb)], axis=0)
    s_new = jnp.sum(qblk * k_new, axis=-1, keepdims=True) * scale
    sink_all = jnp.concatenate([sink] * nb, axis=0)
    m = jnp.maximum(jnp.maximum(jnp.max(s, axis=-1, keepdims=True), s_new), sink_all)
    e = jnp.exp(s - m)
    e_new = jnp.exp(s_new - m)
    den = jnp.sum(e, axis=-1, keepdims=True) + e_new + jnp.exp(sink_all - m)
    e_b = e.astype(BF16)
    o_all = jnp.concatenate([_dot_nt(e_b[rows[i]], vt_ref[i].astype(BF16)) for i in range(nb)], axis=0)
    o_all = jnp.where(own_all, o_all + e_new * v_new, 0.0)
    o = o_all[:, 0:HD_A]
    for kh in range(1, KVH_A):
        o = o + o_all[:, kh * HD_A:(kh + 1) * HD_A]
    o_ref[...] = (o / den).astype(BF16).reshape(nb, H_A, HD_A)
    for i in range(nb):
        kto_ref[i] = jnp.where(last, kn_cols[:, i:i + 1], pltpu.roll(kt_ref[i], buf_len - 1, 1))
        vto_ref[i] = jnp.where(last, v_cols[:, i:i + 1], pltpu.roll(vt_ref[i], buf_len - 1, 1))


def swa_decode(q, k, v, kt, vt, gq, gk, sinks):
    bsz, kw, buf_len = kt.shape
    nb = 8
    rows = pl.BlockSpec((nb, kw), lambda i: (i, 0))
    heads = pl.BlockSpec((nb, H_A, HD_A), lambda i: (i, 0, 0))
    buf = pl.BlockSpec((nb, kw, buf_len), lambda i: (i, 0, 0))
    vec = pl.BlockSpec((1, HD_A), lambda i: (0, 0))
    return pl.pallas_call(
        _swa_decode_kernel,
        out_shape=[
            jax.ShapeDtypeStruct((bsz, H_A, HD_A), BF16),
            jax.ShapeDtypeStruct(kt.shape, F32),
            jax.ShapeDtypeStruct(vt.shape, F32),
        ],
        grid=(bsz // nb,),
        in_specs=[heads, rows, rows, buf, buf, vec, vec, pl.BlockSpec((H_A, 1), lambda i: (0, 0))],
        out_specs=[heads, buf, buf],
        compiler_params=_cp(("parallel",)),
        name="swa_decode",
    )(q, k, v, kt, vt, gq.reshape(1, HD_A), gk.reshape(1, HD_A), sinks.reshape(H_A, 1))


def _gla_logdec(glr, wg2_ref, bg2_ref):
    gl = _dot(glr.astype(BF16), wg2_ref[...]) + bg2_ref[...]
    return _log_sigmoid(gl) * (1.0 / GLA_GATE_NORM)


def _col(row8, i):
    return row8.T[:, i:i + 1]


def _gla_out(o, g_go, gout):
    return (_rms_rows(o, g_go) * (gout * jax.nn.sigmoid(gout))).astype(BF16)


def _gla_prompt_kernel(q_ref, k_ref, v_ref, gout_ref, glr_ref, wg2_ref, bg2_ref, ggo_ref, o_ref, so_ref, s_ref):
    c_idx = pl.program_id(1)
    tb = q_ref.shape[0]
    ck = GLA_CHUNK

    @pl.when(c_idx == 0)
    def _():
        s_ref[...] = jnp.zeros_like(s_ref)

    nck = tb // ck
    ld = _gla_logdec(glr_ref[...], wg2_ref, bg2_ref)
    ti = lax.broadcasted_iota(jnp.int32, (tb, tb), 0)
    si = lax.broadcasted_iota(jnp.int32, (tb, tb), 1)
    causal = (si <= ti) & (si // ck == ti // ck)
    tri = jnp.where(causal, 1.0, 0.0).astype(BF16)
    hi, lo = _split_bf16(ld)
    b = _dot(tri, hi) + _dot(tri, lo)
    last_rows = [b[(c + 1) * ck - 1:(c + 1) * ck, :] for c in range(nck)]
    b_last = jnp.concatenate([jnp.broadcast_to(r, (ck, r.shape[1])) for r in last_rows], axis=0)
    k = k_ref[...]
    qt = ((q_ref[...] * (DK_B ** -0.5)) * jnp.exp(b)).astype(BF16)
    kt = (k * jnp.exp(-b)).astype(BF16)
    kd = (k * jnp.exp(b_last - b)).astype(BF16)
    g_go = ggo_ref[...]
    for h in range(H_B):
        ks = slice(h * DK_B, (h + 1) * DK_B)
        vs = slice(h * DV_B, (h + 1) * DV_B)
        v = v_ref[:, vs].astype(BF16)
        att = jnp.where(causal, _dot_nt(qt[:, ks], kt[:, ks]), 0.0)
        o = _dot(att.astype(BF16), v)
        state = s_ref[h]
        o_state = []
        for c in range(nck):
            rs = slice(c * ck, (c + 1) * ck)
            o_state.append(_dot(qt[rs, ks], state.astype(BF16)))
            d_state = _dot_tn(kd[rs, ks], v[rs])
            decay = _col(jnp.broadcast_to(jnp.exp(last_rows[c][:, ks]), (8, DK_B)), 0)
            state = decay * state + d_state
        s_ref[h] = state
        o = o + jnp.concatenate(o_state, axis=0)
        o_ref[:, vs] = _gla_out(o, g_go, gout_ref[:, vs])
    so_ref[0] = s_ref[...]


def gla_prompt(z0, glr, wg2, bg2, g_go, bsz, t):
    tb = 256
    nt = t // tb
    qk_w = H_B * DK_B
    v_w = H_B * DV_B
    return pl.pallas_call(
        _gla_prompt_kernel,
        out_shape=[
            jax.ShapeDtypeStruct((bsz * t, v_w), BF16),
            jax.ShapeDtypeStruct((bsz, H_B, DK_B, DV_B), F32),
        ],
        grid=(bsz, nt),
        in_specs=[
            pl.BlockSpec((tb, qk_w), lambda b, c: (b * nt + c, Z0_QB // qk_w)),
            pl.BlockSpec((tb, qk_w), lambda b, c: (b * nt + c, Z0_KB // qk_w)),
            pl.BlockSpec((tb, v_w), lambda b, c: (b * nt + c, Z0_VB // v_w)),
            pl.BlockSpec((tb, v_w), lambda b, c: (b * nt + c, Z0_GOUT // v_w)),
            pl.BlockSpec((tb, 128), lambda b, c: (b * nt + c, 0)),
            pl.BlockSpec((128, qk_w), lambda b, c: (0, 0)),
            pl.BlockSpec((1, qk_w), lambda b, c: (0, 0)),
            pl.BlockSpec((1, DV_B), lambda b, c: (0, 0)),
        ],
        out_specs=[
            pl.BlockSpec((tb, v_w), lambda b, c: (b * nt + c, 0)),
            pl.BlockSpec((1, H_B, DK_B, DV_B), lambda b, c: (b, 0, 0, 0)),
        ],
        scratch_shapes=[pltpu.VMEM((H_B, DK_B, DV_B), F32)],
        compiler_params=_cp(("parallel", "arbitrary")),
        name="gla_prompt",
    )(z0, z0, z0, z0, glr, wg2, bg2.reshape(1, qk_w), g_go.reshape(1, DV_B))


def _gla_decode_kernel(q_ref, k_ref, v_ref, gout_ref, glr_ref, wg2_ref, bg2_ref, ggo_ref, s_ref, o_ref, so_ref):
    nb = q_ref.shape[0]
    ld = _gla_logdec(glr_ref[...], wg2_ref, bg2_ref)
    g_go = ggo_ref[...]
    for h in range(H_B):
        ks = slice(h * DK_B, (h + 1) * DK_B)
        vs = slice(h * DV_B, (h + 1) * DV_B)
        eg_t = jnp.exp(ld[:, ks]).T
        k_t = k_ref[:, ks].T
        q_t = (q_ref[:, ks] * (DK_B ** -0.5)).T
        for i in range(nb):
            state = eg_t[:, i:i + 1] * s_ref[i, h] + k_t[:, i:i + 1] * v_ref[i:i + 1, vs]
            so_ref[i, h] = state
            o = jnp.sum(q_t[:, i:i + 1] * state, axis=0, keepdims=True)
            o_ref[i:i + 1, vs] = _gla_out(o, g_go, gout_ref[i:i + 1, vs])


def gla_decode(q, k, v, gout, glr, wg2, bg2, g_go, state):
    bsz = q.shape[0]
    nb = 8
    qk_w = H_B * DK_B
    v_w = H_B * DV_B
    rows = lambda w: pl.BlockSpec((nb, w), lambda i: (i, 0))
    st = pl.BlockSpec((nb, H_B, DK_B, DV_B), lambda i: (i, 0, 0, 0))
    return pl.pallas_call(
        _gla_decode_kernel,
        out_shape=[
            jax.ShapeDtypeStruct((bsz, v_w), BF16),
            jax.ShapeDtypeStruct(state.shape, F32),
        ],
        grid=(bsz // nb,),
        in_specs=[
            rows(qk_w), rows(qk_w), rows(v_w), rows(v_w), rows(128),
            pl.BlockSpec((128, qk_w), lambda i: (0, 0)),
            pl.BlockSpec((1, qk_w), lambda i: (0, 0)),
            pl.BlockSpec((1, DV_B), lambda i: (0, 0)),
            st,
        ],
        out_specs=[rows(v_w), st],
        compiler_params=_cp(("parallel",)),
        name="gla_decode",
    )(q, k, v, gout, glr, wg2, bg2.reshape(1, qk_w), g_go.reshape(1, DV_B), state)


Z1_CX, Z1_CY, Z1_QD, Z1_KD, Z1_VD = 0, 1024, 2048, 3072, 3584


def _rg_gates(xc, wa_ref, ba_ref, wx_ref, bx_ref, lam_ref):
    xb = xc.astype(BF16)
    ra = jnp.concatenate(
        [_dot(xb[:, n * C_BLOCK:(n + 1) * C_BLOCK], wa_ref[n]) for n in range(C_BLOCKS)], axis=-1) + ba_ref[...]
    rx = jnp.concatenate(
        [_dot(xb[:, n * C_BLOCK:(n + 1) * C_BLOCK], wx_ref[n]) for n in range(C_BLOCKS)], axis=-1) + bx_ref[...]
    r_gate = jax.nn.sigmoid(ra)
    i_gate = jax.nn.sigmoid(rx)
    log_a = (-RG_C * r_gate) * _softplus(-lam_ref[...])
    a = jnp.exp(log_a)
    one_minus_a2 = -jnp.tanh(log_a) * (a * a + 1.0)
    u = jnp.sqrt(one_minus_a2) * (i_gate * xc)
    return a, u


def _shift_rows(x, d, fill):
    if d % 8 == 0:
        return jnp.concatenate([jnp.full((d,) + x.shape[1:], fill, x.dtype), x[:x.shape[0] - d]], axis=0)
    rows = lax.broadcasted_iota(jnp.int32, x.shape, 0)
    return jnp.where(rows < d, fill, pltpu.roll(x, d, 0))


def _rglru_prompt_kernel(cx_ref, cy_ref, cw_ref, cb_ref, wa_ref, ba_ref, wx_ref, bx_ref, lam_ref,
                         o_ref, tail_ref, hl_ref, prev_ref, hc_ref):
    t_idx = pl.program_id(1)
    tt = cx_ref.shape[0]

    @pl.when(t_idx == 0)
    def _():
        prev_ref[...] = jnp.zeros_like(prev_ref)
        hc_ref[...] = jnp.zeros_like(hc_ref)

    x = cx_ref[...]
    xp = jnp.concatenate([prev_ref[...], x], axis=0)
    xc = cb_ref[...] + cw_ref[CONV_W - 1:CONV_W, :] * x
    for j in range(CONV_W - 1):
        d = CONV_W - 1 - j
        xc = xc + cw_ref[j:j + 1, :] * pltpu.roll(xp, d, 0)[8:]
    prev_ref[...] = x[tt - 8:]
    tail_ref[0] = x[tt - 8:]

    a, u = _rg_gates(xc, wa_ref, ba_ref, wx_ref, bx_ref, lam_ref)
    d = 1
    while d < tt:
        u = a * _shift_rows(u, d, 0.0) + u
        a = a * _shift_rows(a, d, 1.0)
        d *= 2
    h = a * hc_ref[...] + u
    hc_ref[...] = h[tt - 1:]
    hl_ref[0] = h[tt - 1:]
    o_ref[...] = (h * jax.nn.gelu(cy_ref[...])).astype(BF16)


def rglru_prompt(z1, conv_w, conv_b, wa, ba, wx, bx, lam, bsz, t):
    tt = 256
    nt = t // tt
    c = C_WIDTH
    vec = pl.BlockSpec((1, c), lambda b, i: (0, 0))
    wblk = pl.BlockSpec((C_BLOCKS, C_BLOCK, C_BLOCK), lambda b, i: (0, 0, 0))
    return pl.pallas_call(
        _rglru_prompt_kernel,
        out_shape=[
            jax.ShapeDtypeStruct((bsz * t, c), BF16),
            jax.ShapeDtypeStruct((bsz, 8, c), F32),
            jax.ShapeDtypeStruct((bsz, 1, c), F32),
        ],
        grid=(bsz, nt),
        in_specs=[
            pl.BlockSpec((tt, c), lambda b, i: (b * nt + i, Z1_CX // c)),
            pl.BlockSpec((tt, c), lambda b, i: (b * nt + i, Z1_CY // c)),
            pl.BlockSpec((CONV_W, c), lambda b, i: (0, 0)),
            vec, wblk, vec, wblk, vec, vec,
        ],
        out_specs=[
            pl.BlockSpec((tt, c), lambda b, i: (b * nt + i, 0)),
            pl.BlockSpec((1, 8, c), lambda b, i: (b, 0, 0)),
            pl.BlockSpec((1, 1, c), lambda b, i: (b, 0, 0)),
        ],
        scratch_shapes=[pltpu.VMEM((8, c), F32), pltpu.VMEM((1, c), F32)],
        compiler_params=_cp(("parallel", "arbitrary")),
        name="rglru_prompt",
    )(z1, z1, conv_w, conv_b.reshape(1, c), wa, ba.reshape(1, c), wx, bx.reshape(1, c), lam.reshape(1, c))


def _rglru_decode_kernel(cx_ref, cy_ref, buf_ref, h0_ref, cw_ref, cb_ref, wa_ref, ba_ref, wx_ref, bx_ref, lam_ref,
                         o_ref, nbuf_ref, hl_ref):
    x = cx_ref[...]
    xc = cb_ref[...] + cw_ref[CONV_W - 1:CONV_W, :] * x
    for j in range(CONV_W - 1):
        xc = xc + cw_ref[j:j + 1, :] * buf_ref[j]
    for j in range(CONV_W - 2):
        nbuf_ref[j] = buf_ref[j + 1]
    nbuf_ref[CONV_W - 2] = x
    a, u = _rg_gates(xc, wa_ref, ba_ref, wx_ref, bx_ref, lam_ref)
    h = a * h0_ref[...] + u
    hl_ref[...] = h
    o_ref[...] = (h * jax.nn.gelu(cy_ref[...])).astype(BF16)


def rglru_decode(cx, cy, buf, h0, conv_w, conv_b, wa, ba, wx, bx, lam):
    bsz, c = cx.shape
    return pl.pallas_call(
        _rglru_decode_kernel,
        out_shape=[
            jax.ShapeDtypeStruct((bsz, c), BF16),
            jax.ShapeDtypeStruct(buf.shape, F32),
            jax.ShapeDtypeStruct((bsz, c), F32),
        ],
        compiler_params=pltpu.CompilerParams(vmem_limit_bytes=V7X_VMEM_LIMIT_BYTES),
        name="rglru_decode",
    )(cx, cy, buf, h0, conv_w, conv_b.reshape(1, c), wa, ba.reshape(1, c), wx, bx.reshape(1, c), lam.reshape(1, c))


def _strict_upper_ones(n):
    j = lax.broadcasted_iota(jnp.int32, (n, n), 0)
    s = lax.broadcasted_iota(jnp.int32, (n, n), 1)
    return jnp.where(j > s, 1.0, 0.0).astype(BF16)


def _sb_tile(z, later, surv, mask):
    sp = _softplus(z)
    l1m = -sp
    lsg = z - sp
    if mask is not None:
        l1m = jnp.where(mask, l1m, 0.0)
    hi, lo = _split_bf16(l1m)
    suffix = _dot(hi, later) + _dot(lo, later)
    w = jnp.exp(lsg + suffix + surv)
    if mask is not None:
        w = jnp.where(mask, w, 0.0)
    return w, jnp.sum(l1m, axis=-1, keepdims=True)


def _sb_prompt_kernel(bias_ref, q_ref, k_ref, v_ref, gq_ref, gk_ref, o_ref, ko_ref, kb_ref, vb_ref):
    kh = pl.program_id(1)
    i = pl.program_id(2)
    blk = q_ref.shape[0]
    grp = H_D // KVH_D

    @pl.when(i == 0)
    def _():
        kn = _rms_rows(k_ref[...], gk_ref[...])
        ko_ref[...] = kn
        kb_ref[...] = kn.astype(BF16)
        vb_ref[...] = v_ref[...].astype(BF16)

    gq = gq_ref[...]
    qs = jnp.concatenate(
        [_rms_rows(q_ref[:, g * HD_D:(g + 1) * HD_D], gq) for g in range(grp)], axis=0).astype(BF16)
    bias = jnp.concatenate(
        [jnp.full((blk, 1), bias_ref[kh * grp + g], F32) for g in range(grp)], axis=0)
    later = _strict_upper_ones(blk)
    scale = HD_D ** -0.5

    def tile(j, mask, acc, surv):
        off = pl.multiple_of(j * blk, blk)
        z = _dot_nt(qs, kb_ref[pl.ds(off, blk), :]) * scale + bias
        w, tot = _sb_tile(z, later, surv, mask)
        return acc + _dot(w.astype(BF16), vb_ref[pl.ds(off, blk), :]), surv + tot

    rows = lax.broadcasted_iota(jnp.int32, (grp * blk, blk), 0) & (blk - 1)
    cols = lax.broadcasted_iota(jnp.int32, (grp * blk, blk), 1)
    acc, surv = tile(i, cols < rows, jnp.zeros((grp * blk, HD_D), F32), jnp.zeros((grp * blk, 1), F32))

    def pair(step, carry):
        j = i - 1 - 2 * step
        return tile(j - 1, None, *tile(j, None, *carry))

    def last(step, carry):
        return tile(0, None, *carry)

    n_pairs = i // 2
    carry = lax.fori_loop(0, n_pairs, pair, (acc, surv))
    acc, _ = lax.fori_loop(0, i - 2 * n_pairs, last, carry)
    for g in range(grp):
        o_ref[:, g * HD_D:(g + 1) * HD_D] = acc[g * blk:(g + 1) * blk].astype(BF16)


def sb_prompt(z1, gq, gk, bias, bsz, t):
    blk = SB_PROMPT_TILE
    nb = t // blk
    grp = H_D // KVH_D
    qw = grp * HD_D
    grid_spec = pltpu.PrefetchScalarGridSpec(
        num_scalar_prefetch=1,
        grid=(bsz, KVH_D, nb),
        in_specs=[
            pl.BlockSpec((blk, qw), lambda b, k, i, s: (b * nb + i, Z1_QD // qw + k)),
            pl.BlockSpec((t, HD_D), lambda b, k, i, s: (b, Z1_KD // HD_D + k)),
            pl.BlockSpec((t, HD_D), lambda b, k, i, s: (b, Z1_VD // HD_D + k)),
            pl.BlockSpec((1, HD_D), lambda b, k, i, s: (0, 0)),
            pl.BlockSpec((1, HD_D), lambda b, k, i, s: (0, 0)),
        ],
        out_specs=[
            pl.BlockSpec((blk, qw), lambda b, k, i, s: (b * nb + i, k)),
            pl.BlockSpec((t, HD_D), lambda b, k, i, s: (b, k)),
        ],
        scratch_shapes=[pltpu.VMEM((t, HD_D), BF16), pltpu.VMEM((t, HD_D), BF16)],
    )
    return pl.pallas_call(
        _sb_prompt_kernel,
        out_shape=[
            jax.ShapeDtypeStruct((bsz * t, H_D * HD_D), BF16),
            jax.ShapeDtypeStruct((bsz * t, KVH_D * HD_D), F32),
        ],
        grid_spec=grid_spec,
        compiler_params=_cp(("parallel", "parallel", "arbitrary")),
        name="sb_prompt",
    )(bias, z1, z1, z1, gq.reshape(1, HD_D), gk.reshape(1, HD_D))


def _sb_decode_kernel(n_pages, pt_ref, bias_ref, q_ref, k_ref, v_ref, gq_ref, gk_ref, ck_hbm, cv_hbm,
                      o_ref, ko_ref, kbuf, vbuf, sem):
    b = pl.program_id(0)
    nb = pl.num_programs(0)
    npg = SB_PAGES_PER_STEP
    n_chunks = n_pages // npg
    prow = PAGE * KVH_D
    grp = H_D // KVH_D
    scale = HD_D ** -0.5

    def page_copies(bb, chunk, slot):
        cps = []
        for p in range(npg):
            page = pt_ref[bb * n_pages + chunk * npg + p]
            dst = pl.ds(p * prow, prow)
            cps.append(pltpu.make_async_copy(ck_hbm.at[page], kbuf.at[slot, dst], sem.at[0, slot]))
            cps.append(pltpu.make_async_copy(cv_hbm.at[page], vbuf.at[slot, dst], sem.at[1, slot]))
        return cps

    @pl.when(b == 0)
    def _():
        for cp in page_copies(0, n_chunks - 1, 0):
            cp.start()

    bias = jnp.concatenate([jnp.full((1, 1), bias_ref[h], F32) for h in range(H_D)], axis=0)
    qn = _rms_rows(q_ref[0], gq_ref[...])
    qb = qn.astype(BF16)
    kn = _rms_rows(k_ref[0], gk_ref[...])
    ko_ref[0] = kn
    kn_sel = jnp.concatenate([kn[h // grp:h // grp + 1] for h in range(H_D)], axis=0)
    v_sel = jnp.concatenate([v_ref[0, h // grp:h // grp + 1] for h in range(H_D)], axis=0)
    z0 = jnp.sum(qn * kn_sel, axis=-1, keepdims=True) * scale + bias
    visible = jnp.zeros((H_D, 1), jnp.int32) < jnp.zeros((H_D, 1), jnp.int32)
    sp0 = _softplus(z0)
    acc = jnp.where(visible, jnp.exp(z0 - sp0), 0.0) * v_sel
    surv = jnp.where(visible, -sp0, 0.0)

    later = _strict_upper_ones(prow)
    row = lax.broadcasted_iota(jnp.int32, (npg * H_D, prow), 0)
    col = lax.broadcasted_iota(jnp.int32, (npg * H_D, prow), 1)
    valid = (col & (KVH_D - 1)) == ((row & (H_D - 1)) // grp)
    bias_r = jnp.concatenate([bias] * npg, axis=0)

    for i in range(n_chunks):
        slot = i % 2
        if i + 1 < n_chunks:
            for cp in page_copies(b, n_chunks - 2 - i, 1 - slot):
                cp.start()
        else:
            @pl.when(b + 1 < nb)
            def _():
                for cp in page_copies(b + 1, n_chunks - 1, 1 - slot):
                    cp.start()
        for cp in page_copies(b, n_chunks - 1 - i, slot):
            cp.wait()

        z = _dot_nt(qb, kbuf[slot].astype(BF16))
        z = jnp.concatenate([z[:, p * prow:(p + 1) * prow] for p in range(npg)], axis=0) * scale + bias_r
        sp = _softplus(z)
        l1m = jnp.where(valid, -sp, 0.0)
        hi, lo = _split_bf16(l1m)
        suffix = _dot(hi, later) + _dot(lo, later)
        tot = jnp.sum(l1m, axis=-1, keepdims=True)
        survs = [None] * npg
        for p in reversed(range(npg)):
            survs[p] = surv
            surv = surv + tot[p * H_D:(p + 1) * H_D]
        w = jnp.where(valid, jnp.exp((z - sp) + suffix + jnp.concatenate(survs, axis=0)), 0.0)
        w = jnp.concatenate([w[p * H_D:(p + 1) * H_D] for p in range(npg)], axis=1).astype(BF16)
        acc = acc + _dot(w, vbuf[slot].astype(BF16))

    o_ref[0] = acc.astype(BF16)


def sb_decode(q, k, v, gq, gk, bias, cache_k, cache_v, page_table):
    bsz = q.shape[0]
    n_pages = page_table.shape[1]
    npg = SB_PAGES_PER_STEP
    assert n_pages % (2 * npg) == 0
    prow = PAGE * KVH_D
    grid_spec = pltpu.PrefetchScalarGridSpec(
        num_scalar_prefetch=2,
        grid=(bsz,),
        in_specs=[
            pl.BlockSpec((1, H_D, HD_D), lambda b, pt, bias_: (b, 0, 0)),
            pl.BlockSpec((1, KVH_D, HD_D), lambda b, pt, bias_: (b, 0, 0)),
            pl.BlockSpec((1, KVH_D, HD_D), lambda b, pt, bias_: (b, 0, 0)),
            pl.BlockSpec((1, HD_D), lambda b, pt, bias_: (0, 0)),
            pl.BlockSpec((1, HD_D), lambda b, pt, bias_: (0, 0)),
            pl.BlockSpec(memory_space=pl.ANY),
            pl.BlockSpec(memory_space=pl.ANY),
        ],
        out_specs=[
            pl.BlockSpec((1, H_D, HD_D), lambda b, pt, bias_: (b, 0, 0)),
            pl.BlockSpec((1, KVH_D, HD_D), lambda b, pt, bias_: (b, 0, 0)),
        ],
        scratch_shapes=[
            pltpu.VMEM((2, npg * prow, HD_D), F32),
            pltpu.VMEM((2, npg * prow, HD_D), F32),
            pltpu.SemaphoreType.DMA((2, 2)),
        ],
    )
    return pl.pallas_call(
        functools.partial(_sb_decode_kernel, n_pages),
        out_shape=[
            jax.ShapeDtypeStruct((bsz, H_D, HD_D), BF16),
            jax.ShapeDtypeStruct((bsz, KVH_D, HD_D), F32),
        ],
        grid_spec=grid_spec,
        compiler_params=_cp(("arbitrary",)),
        name="sb_decode",
    )(page_table.reshape(-1), bias, q, k, v, gq.reshape(1, HD_D), gk.reshape(1, HD_D), cache_k, cache_v)


def kernel(x_prompt, x_sample, state_swa_k, state_swa_v, state_gla, state_conv, state_lru, cache_sb_k, cache_sb_v, page_table, p_prompt, p_sample, g_mix, g_ffn, w_ffn_gate, w_ffn_up, w_ffn_down, g_ple, w_ple_gate, w_ple_proj, w_in_even, g_qnorm_a, g_knorm_a, sinks_a, w_gla_gate2, b_gla_gate2, g_gla_out, w_out_even, w_in_odd, conv_w, conv_b, w_rg_a, b_rg_a, w_rg_x, b_rg_x, lru_lambda, g_qnorm_d, g_knorm_d, sb_bias, w_out_odd):
    bsz, t, d = x_prompt.shape
    dbs = x_sample.shape[0]
    n_p = bsz * t
    depth = g_mix.shape[0]
    ple = p_prompt.shape[-1]

    x_p = x_prompt.reshape(n_p, d)
    x_s = x_sample.reshape(dbs, d)
    p_all = jnp.concatenate([p_prompt.reshape(depth, n_p, ple), p_sample.reshape(depth, dbs, ple)], axis=1)

    ffn_w = (w_ffn_gate.astype(BF16), w_ffn_up.astype(BF16), w_ffn_down.astype(BF16))
    ple_w = (w_ple_gate.astype(BF16), w_ple_proj.astype(BF16))

    def dense_tail(h, i, n_tail=0):
        h = ffn_residual(h, g_ffn[i], *ffn_w, i, 512, f"ffn_{i}")
        return ple_residual(h, g_ple[i], p_all, *ple_w, i, f"ple_{i}", n_tail)

    w = w_in_even[0]
    qa, ka, va, qb, kb, vb, glr_w, gout = jnp.split(
        w, [1024, 1280, 1536, 2048, 2560, 3584, 3600], axis=1)
    w0 = jnp.concatenate([qa, vb, gout, ka, va, qb, kb], axis=1).astype(BF16)
    w0_glr = jnp.pad(glr_w, ((0, 0), (0, 128 - GLA_RANK))).astype(BF16)
    z0, glr = norm_matmul(x_p, g_mix[0], w0, 1536, "in_proj_even", w_side=w0_glr, x_tail=x_s)
    wg2 = jnp.pad(w_gla_gate2[0], ((0, 128 - GLA_RANK), (0, 0))).astype(BF16)

    oa_p, swa_k_p, swa_v_p = swa_prompt(z0, g_qnorm_a[0], g_knorm_a[0], sinks_a[0], bsz, t)
    ob_p, gla_p = gla_prompt(z0, glr, wg2, b_gla_gate2[0], g_gla_out[0], bsz, t)

    zd = z0[n_p:]
    kw_a = KVH_A * HD_A
    to_keys_last = lambda s: jnp.transpose(s, (0, 2, 3, 1)).reshape(dbs, kw_a, -1)
    from_keys_last = lambda s: jnp.transpose(s.reshape(dbs, KVH_A, HD_A, -1), (0, 3, 1, 2))[None]
    oa_d, swa_kt_d, swa_vt_d = swa_decode(
        zd[:, Z0_QA:Z0_QA + 1024].reshape(dbs, H_A, HD_A),
        zd[:, Z0_KA:Z0_KA + kw_a], zd[:, Z0_VA:Z0_VA + kw_a],
        to_keys_last(state_swa_k[0]), to_keys_last(state_swa_v[0]),
        g_qnorm_a[0], g_knorm_a[0], sinks_a[0])
    ob_d, gla_d = gla_decode(
        zd[:, Z0_QB:Z0_QB + 512], zd[:, Z0_KB:Z0_KB + 512], zd[:, Z0_VB:Z0_VB + 1024],
        zd[:, Z0_GOUT:Z0_GOUT + 1024], glr[n_p:], wg2, b_gla_gate2[0], g_gla_out[0], state_gla[0])

    wo = w_out_even[0].astype(BF16)
    h = proj_residual(x_p, oa_p, ob_p, oa_d.reshape(dbs, 1024), ob_d, wo[:1024], wo[1024:], "out_proj_even",
                      h_tail=x_s)
    h = dense_tail(h, 0)

    z1 = norm_matmul(h, g_mix[1], w_in_odd[0].astype(BF16), 1024, "in_proj_odd")
    wa = w_rg_a[0].astype(BF16)
    wx = w_rg_x[0].astype(BF16)
    oc_p, conv_tail, lru_p = rglru_prompt(z1, conv_w[0], conv_b[0], wa, b_rg_a[0], wx, b_rg_x[0], lru_lambda[0], bsz, t)
    od_p, sb_k_p = sb_prompt(z1, g_qnorm_d[0], g_knorm_d[0], sb_bias[0], bsz, t)

    zd = z1[n_p:]
    oc_d, conv_d, lru_d = rglru_decode(
        zd[:, Z1_CX:Z1_CX + 1024], zd[:, Z1_CY:Z1_CY + 1024],
        jnp.swapaxes(state_conv[0], 0, 1), state_lru[0],
        conv_w[0], conv_b[0], wa, b_rg_a[0], wx, b_rg_x[0], lru_lambda[0])
    kvw = KVH_D * HD_D
    n_phys = cache_sb_k.shape[1]
    od_d, sb_k_d = sb_decode(
        zd[:, Z1_QD:Z1_QD + 1024].reshape(dbs, H_D, HD_D),
        zd[:, Z1_KD:Z1_KD + kvw].reshape(dbs, KVH_D, HD_D),
        zd[:, Z1_VD:Z1_VD + kvw].reshape(dbs, KVH_D, HD_D),
        g_qnorm_d[0], g_knorm_d[0], sb_bias[0],
        cache_sb_k[0].reshape(n_phys, PAGE * KVH_D, HD_D), cache_sb_v[0].reshape(n_phys, PAGE * KVH_D, HD_D),
        page_table)

    wo = w_out_odd[0].astype(BF16)
    h = proj_residual(h, oc_p, od_p, oc_d, od_d.reshape(dbs, 1024), wo[:1024], wo[1024:], "out_proj_odd")
    y_prompt, y_sample = dense_tail(h, 1, n_tail=dbs)

    n_keep = min(WINDOW, t)
    return (
        y_prompt.reshape(bsz, t, d), y_sample.reshape(dbs, 1, d),
        swa_k_p.reshape(1, bsz, n_keep, KVH_A, HD_A), swa_v_p.reshape(1, bsz, n_keep, KVH_A, HD_A),
        gla_p[None],
        conv_tail[:, 8 - (CONV_W - 1):][None], lru_p.reshape(1, bsz, C_WIDTH),
        sb_k_p.reshape(1, bsz, t, KVH_D, HD_D), z1[:n_p, Z1_VD:].reshape(1, bsz, t, KVH_D, HD_D),
        from_keys_last(swa_kt_d), from_keys_last(swa_vt_d),
        gla_d[None],
        jnp.swapaxes(conv_d, 0, 1)[None], lru_d[None],
        sb_k_d.reshape(1, dbs, 1, KVH_D, HD_D), zd[:, Z1_VD:].reshape(1, dbs, 1, KVH_D, HD_D),
    )
```

```python
import functools

import jax
import jax.numpy as jnp
from jax import lax
from jax.experimental import pallas as pl
from jax.experimental.pallas import tpu as pltpu

F32 = jnp.float32
BF16 = jnp.bfloat16
EPS = 1e-6

V7X_VMEM_LIMIT_BYTES = 56 * 1024 * 1024

H_A, KVH_A, HD_A, WINDOW = 16, 4, 64, 128
H_B, DK_B, DV_B, GLA_RANK, GLA_GATE_NORM, GLA_CHUNK = 4, 128, 256, 16, 16.0, 64
C_WIDTH, C_BLOCKS, CONV_W, RG_C = 1024, 8, 4, 8.0
C_BLOCK = C_WIDTH // C_BLOCKS
H_D, KVH_D, HD_D = 8, 4, 128
PAGE = 128

ROW_TILE = 640
WIDE_ROW_TILE = 1040
SB_PROMPT_TILE = 256
SB_PAGES_PER_STEP = 16


def _cp(sem):
    return pltpu.CompilerParams(dimension_semantics=sem, vmem_limit_bytes=V7X_VMEM_LIMIT_BYTES)


def _rms_rows(x, g):
    r = lax.rsqrt(jnp.mean(x * x, axis=-1, keepdims=True) + EPS)
    return (x * r) * g


def _softplus(x):
    return jnp.maximum(x, 0.0) + jnp.log(1.0 + jnp.exp(-jnp.abs(x)))


def _log_sigmoid(x):
    return jnp.minimum(x, 0.0) - jnp.log(1.0 + jnp.exp(-jnp.abs(x)))


def _split_bf16(x):
    hi = x.astype(BF16)
    lo = (x - hi.astype(F32)).astype(BF16)
    return hi, lo


def _dot(a, b):
    return jnp.dot(a, b, preferred_element_type=F32)


def _dot_nt(a, b):
    return lax.dot_general(a, b, (((1,), (1,)), ((), ())), preferred_element_type=F32)


def _dot_tn(a, b):
    return lax.dot_general(a, b, (((0,), (0,)), ((), ())), preferred_element_type=F32)


def _norm_matmul_kernel(has_tail, has_side, x_ref, g_ref, w_ref, *rest):
    rest = list(rest)
    xt_ref = rest.pop(0) if has_tail else None
    ws_ref = rest.pop(0) if has_side else None
    o_ref = rest.pop(0)
    os_ref = rest.pop(0) if has_side else None
    xn_ref, = rest

    @pl.when(pl.program_id(1) == 0)
    def _():
        xn_ref[...] = _rms_rows(x_ref[...], g_ref[...]).astype(BF16)
        if has_tail:
            @pl.when(pl.program_id(0) == pl.num_programs(0) - 1)
            def _():
                nt = xt_ref.shape[0]
                xn_ref[xn_ref.shape[0] - nt:, :] = _rms_rows(xt_ref[...], g_ref[...]).astype(BF16)
        if has_side:
            os_ref[...] = _dot(xn_ref[...], ws_ref[...])

    o_ref[...] = _dot(xn_ref[...], w_ref[...])


def norm_matmul(x, g, w, tn, name, w_side=None, x_tail=None):
    d = x.shape[1]
    nt = 0 if x_tail is None else x_tail.shape[0]
    n = x.shape[0] + nt
    nout = w.shape[1]
    tm = WIDE_ROW_TILE
    assert n % tm == 0 and (nt == 0 or x.shape[0] % tm + nt == tm)
    in_specs = [
        pl.BlockSpec((tm, d), lambda i, j: (i, 0)),
        pl.BlockSpec((1, d), lambda i, j: (0, 0)),
        pl.BlockSpec((d, tn), lambda i, j: (0, j)),
    ]
    out_shape = [jax.ShapeDtypeStruct((n, nout), F32)]
    out_specs = [pl.BlockSpec((tm, tn), lambda i, j: (i, j))]
    args = [x, g.reshape(1, d), w]
    if x_tail is not None:
        in_specs.append(pl.BlockSpec((nt, d), lambda i, j: (0, 0)))
        args.append(x_tail)
    if w_side is not None:
        ns = w_side.shape[1]
        in_specs.append(pl.BlockSpec((d, ns), lambda i, j: (0, 0)))
        out_shape.append(jax.ShapeDtypeStruct((n, ns), F32))
        out_specs.append(pl.BlockSpec((tm, ns), lambda i, j: (i, 0)))
        args.append(w_side)
    out = pl.pallas_call(
        functools.partial(_norm_matmul_kernel, x_tail is not None, w_side is not None),
        out_shape=out_shape,
        grid=(n // tm, nout // tn),
        in_specs=in_specs,
        out_specs=out_specs,
        scratch_shapes=[pltpu.VMEM((tm, d), BF16)],
        compiler_params=_cp(("parallel", "arbitrary")),
        name=name,
    )(*args)
    return out if w_side is not None else out[0]


def _proj_res_kernel(h_ref, a_ref, b_ref, at_ref, bt_ref, wa_ref, wb_ref, *rest):
    ht_ref, o_ref = rest if len(rest) == 2 else (None, rest[0])
    tm = h_ref.shape[0]
    nt = at_ref.shape[0]
    o_ref[...] = h_ref[...] + (_dot(a_ref[...], wa_ref[...]) + _dot(b_ref[...], wb_ref[...]))

    @pl.when(pl.program_id(0) == pl.num_programs(0) - 1)
    def _():
        tail = _dot(at_ref[...], wa_ref[...]) + _dot(bt_ref[...], wb_ref[...])
        h_tail = h_ref[tm - nt:, :] if ht_ref is None else ht_ref[...]
        o_ref[tm - nt:, :] = h_tail + tail


def proj_residual(h, a, b, a_tail, b_tail, wa, wb, name, h_tail=None):
    d = h.shape[1]
    ka, kb = a.shape[1], b.shape[1]
    nt = a_tail.shape[0]
    n = a.shape[0] + nt
    tm = ROW_TILE
    assert h.shape[0] == (n if h_tail is None else a.shape[0]) and n % tm == 0 and a.shape[0] % tm + nt == tm
    in_specs = [
        pl.BlockSpec((tm, d), lambda i: (i, 0)),
        pl.BlockSpec((tm, ka), lambda i: (i, 0)),
        pl.BlockSpec((tm, kb), lambda i: (i, 0)),
        pl.BlockSpec((nt, ka), lambda i: (0, 0)),
        pl.BlockSpec((nt, kb), lambda i: (0, 0)),
        pl.BlockSpec((ka, d), lambda i: (0, 0)),
        pl.BlockSpec((kb, d), lambda i: (0, 0)),
    ]
    args = [h, a, b, a_tail, b_tail, wa, wb]
    if h_tail is not None:
        in_specs.append(pl.BlockSpec((nt, d), lambda i: (0, 0)))
        args.append(h_tail)
    return pl.pallas_call(
        _proj_res_kernel,
        out_shape=jax.ShapeDtypeStruct((n, d), F32),
        grid=(n // tm,),
        in_specs=in_specs,
        out_specs=pl.BlockSpec((tm, d), lambda i: (i, 0)),
        compiler_params=_cp(("parallel",)),
        name=name,
    )(*args)


def _ffn_kernel(h_ref, g_ref, wg_ref, wu_ref, wd_ref, o_ref, hn_ref):
    @pl.when(pl.program_id(1) == 0)
    def _():
        h = h_ref[...]
        hn_ref[...] = _rms_rows(h, g_ref[...]).astype(BF16)
        o_ref[...] = h

    hn = hn_ref[...]
    gate = _dot(hn, wg_ref[...])
    up = _dot(hn, wu_ref[...])
    act = (gate * jax.nn.sigmoid(gate)) * up
    o_ref[...] += _dot(act.astype(BF16), wd_ref[...])


def ffn_residual(h, g, wg, wu, wd, layer, tf, name):
    n, d = h.shape
    dff = wg.shape[2]
    tm = WIDE_ROW_TILE
    return pl.pallas_call(
        _ffn_kernel,
        out_shape=jax.ShapeDtypeStruct((n, d), F32),
        grid=(n // tm, dff // tf),
        in_specs=[
            pl.BlockSpec((tm, d), lambda i, f: (i, 0)),
            pl.BlockSpec((1, d), lambda i, f: (0, 0)),
            pl.BlockSpec((None, d, tf), lambda i, f: (layer, 0, f)),
            pl.BlockSpec((None, d, tf), lambda i, f: (layer, 0, f)),
            pl.BlockSpec((None, tf, d), lambda i, f: (layer, f, 0)),
        ],
        out_specs=pl.BlockSpec((tm, d), lambda i, f: (i, 0)),
        scratch_shapes=[pltpu.VMEM((tm, d), BF16)],
        compiler_params=_cp(("parallel", "arbitrary")),
        name=name,
    )(h, g.reshape(1, d), wg, wu, wd)


def _ple_kernel(h_ref, g_ref, p_ref, wg_ref, wp_ref, o_ref, *tail_ref):
    h = h_ref[...]
    hn = _rms_rows(h, g_ref[...]).astype(BF16)
    gate = jax.nn.sigmoid(_dot(hn, wg_ref[...]))
    proj = _dot(p_ref[...].astype(BF16), wp_ref[...])
    y = h + gate * proj
    o_ref[...] = y
    if tail_ref:
        @pl.when(pl.program_id(0) == pl.num_programs(0) - 1)
        def _():
            nt = tail_ref[0].shape[0]
            tail_ref[0][...] = y[y.shape[0] - nt:, :]


def ple_residual(h, g, p, wg, wp, layer, name, n_tail=0):
    n, d = h.shape
    dp = p.shape[2]
    tm = ROW_TILE
    assert n % tm == 0 and (n_tail == 0 or (n - n_tail) % tm + n_tail == tm)
    out_shape = [jax.ShapeDtypeStruct((n - n_tail, d), F32)]
    out_specs = [pl.BlockSpec((tm, d), lambda i: (i, 0))]
    if n_tail:
        out_shape.append(jax.ShapeDtypeStruct((n_tail, d), F32))
        out_specs.append(pl.BlockSpec((n_tail, d), lambda i: (0, 0)))
    out = pl.pallas_call(
        _ple_kernel,
        out_shape=out_shape,
        grid=(n // tm,),
        in_specs=[
            pl.BlockSpec((tm, d), lambda i: (i, 0)),
            pl.BlockSpec((1, d), lambda i: (0, 0)),
            pl.BlockSpec((None, tm, dp), lambda i: (layer, i, 0)),
            pl.BlockSpec((None, d, d), lambda i: (layer, 0, 0)),
            pl.BlockSpec((None, dp, d), lambda i: (layer, 0, 0)),
        ],
        out_specs=out_specs,
        compiler_params=_cp(("arbitrary",)),
        name=name,
    )(h, g.reshape(1, d), p, wg, wp)
    return out if n_tail else out[0]


Z0_QA, Z0_KA, Z0_VA, Z0_QB, Z0_KB, Z0_VB, Z0_GOUT = 0, 1024, 1280, 1536, 2048, 2560, 3584
Z0_WIDTH = 4608


def _swa_prompt_kernel(sink_ref, q_ref, kvp_ref, kvc_ref, gq_ref, gk_ref, o_ref, ko_ref, vo_ref):
    n = pl.program_id(1)
    blk = q_ref.shape[0]
    grp = H_A // KVH_A
    kw = KVH_A * HD_A
    lanes = 2 * HD_A
    gq = gq_ref[...]
    gk = gk_ref[...]
    kv = jnp.concatenate([kvp_ref[...], kvc_ref[...]], axis=0)
    rows = lax.broadcasted_iota(jnp.int32, (grp * blk, 2 * blk), 0)
    cols = lax.broadcasted_iota(jnp.int32, (grp * blk, 2 * blk), 1)
    diff = (rows & (blk - 1)) - cols + blk
    lo = jnp.where(n == 0, blk, 0)
    mask = (diff >= 0) & (diff <= WINDOW) & (cols >= lo)

    r = lax.broadcasted_iota(jnp.int32, (lanes, lanes), 0)
    c = lax.broadcasted_iota(jnp.int32, (lanes, lanes), 1)
    head_sum = jnp.where(r // HD_A == c // HD_A, 1.0, 0.0).astype(BF16)
    spread = [jnp.where(r == (c & (HD_A - 1)) + half * HD_A, 1.0, 0.0).astype(BF16) for half in range(2)]
    one = jnp.ones((), BF16)
    low_half = lax.broadcasted_iota(jnp.int32, (blk, lanes), 1) < HD_A
    low_kv = lax.broadcasted_iota(jnp.int32, (2 * blk, lanes), 1) < HD_A

    def head_norm(x, g):
        hi, lo_ = _split_bf16(x * x)
        ss = _dot(hi, head_sum) + _dot(lo_, head_sum)
        return (x * lax.rsqrt(ss * (1.0 / HD_A) + EPS)) * g

    for pair in range(KVH_A // 2):
        ps = slice(pair * lanes, (pair + 1) * lanes)
        kn_pair = head_norm(kv[:, ps], gk)
        ko_ref[0, :, ps] = kn_pair[blk:]
        kn_b = kn_pair.astype(BF16)
        v_b = kv[:, kw + pair * lanes: kw + (pair + 1) * lanes].astype(BF16)
        for half in range(2):
            kh = 2 * pair + half
            kd = _dot(kn_b, spread[half]).astype(BF16)
            vd = _dot(v_b, spread[half]).astype(BF16)
            s_low, s_high = [], []
            for j in range(grp // 2):
                qs = slice(kh * grp * HD_A + j * lanes, kh * grp * HD_A + (j + 1) * lanes)
                qn = head_norm(q_ref[:, qs], gq)
                s_low.append(_dot_nt(jnp.where(low_half, qn, 0.0).astype(BF16), kd))
                s_high.append(_dot_nt(jnp.where(low_half, 0.0, qn).astype(BF16), kd))
            s = jnp.concatenate(s_low + s_high, axis=0) * (HD_A ** -0.5)
            s = jnp.where(mask, s, -jnp.inf)
            order = [2 * j for j in range(grp // 2)] + [2 * j + 1 for j in range(grp // 2)]
            sink = jnp.concatenate([jnp.full((blk, 1), sink_ref[kh * grp + g], F32) for g in order], axis=0)
            m = jnp.maximum(jnp.max(s, axis=-1, keepdims=True), sink)
            e = jnp.exp(s - m).astype(BF16)
            e_sink = jnp.exp(sink - m)
            nh = (grp // 2) * blk
            o_low = _dot(e[:nh], jnp.where(low_kv, vd, one))
            o_high = _dot(e[nh:], jnp.where(low_kv, one, vd))
            o_low = o_low / (pltpu.roll(o_low, HD_A, 1) + e_sink[:nh])
            o_high = o_high / (pltpu.roll(o_high, HD_A, 1) + e_sink[nh:])
            for j in range(grp // 2):
                o_pair = jnp.where(low_half, o_low[j * blk:(j + 1) * blk], o_high[j * blk:(j + 1) * blk])
                o_ref[:, kh * grp * HD_A + j * lanes: kh * grp * HD_A + (j + 1) * lanes] = o_pair.astype(BF16)
    vo_ref[0] = kvc_ref[:, kw:]


def swa_prompt(z0, gq, gk, sinks, bsz, t):
    blk = 128
    nb = t // blk
    kvw = 2 * KVH_A * HD_A
    kv_col = Z0_KA // kvw
    grid_spec = pltpu.PrefetchScalarGridSpec(
        num_scalar_prefetch=1,
        grid=(bsz, nb),
        in_specs=[
            pl.BlockSpec((blk, H_A * HD_A), lambda b, n, s: (b * nb + n, 0)),
            pl.BlockSpec((blk, kvw), lambda b, n, s: (b * nb + jnp.maximum(n - 1, 0), kv_col)),
            pl.BlockSpec((blk, kvw), lambda b, n, s: (b * nb + n, kv_col)),
            pl.BlockSpec((1, 2 * HD_A), lambda b, n, s: (0, 0)),
            pl.BlockSpec((1, 2 * HD_A), lambda b, n, s: (0, 0)),
        ],
        out_specs=[
            pl.BlockSpec((blk, H_A * HD_A), lambda b, n, s: (b * nb + n, 0)),
            pl.BlockSpec((1, blk, KVH_A * HD_A), lambda b, n, s: (b, 0, 0)),
            pl.BlockSpec((1, blk, KVH_A * HD_A), lambda b, n, s: (b, 0, 0)),
        ],
    )
    return pl.pallas_call(
        _swa_prompt_kernel,
        out_shape=[
            jax.ShapeDtypeStruct((bsz * t, H_A * HD_A), BF16),
            jax.ShapeDtypeStruct((bsz, blk, KVH_A * HD_A), F32),
            jax.ShapeDtypeStruct((bsz, blk, KVH_A * HD_A), F32),
        ],
        grid_spec=grid_spec,
        compiler_params=_cp(("parallel", "arbitrary")),
        name="swa_prompt",
    )(sinks, z0, z0, z0, jnp.tile(gq, 2).reshape(1, 2 * HD_A), jnp.tile(gk, 2).reshape(1, 2 * HD_A))


def _swa_decode_kernel(q_ref, k_ref, v_ref, kt_ref, vt_ref, gq_ref, gk_ref, sink_ref, o_ref, kto_ref, vto_ref):
    nb = q_ref.shape[0]
    grp = H_A // KVH_A
    kw = KVH_A * HD_A
    buf_len = kt_ref.shape[2]
    gq = gq_ref[...]
    gk = gk_ref[...]
    sink = sink_ref[...]
    head = lax.broadcasted_iota(jnp.int32, (H_A, kw), 0)
    lane_blk = lax.broadcasted_iota(jnp.int32, (H_A, kw), 1) // HD_A
    own = lane_blk == head // grp
    last = lax.broadcasted_iota(jnp.int32, (kw, buf_len), 1) == buf_len - 1
    kn_rows = jnp.concatenate(
        [_rms_rows(k_ref[:, kh * HD_A:(kh + 1) * HD_A], gk) for kh in range(KVH_A)], axis=-1)
    v_rows = v_ref[...]
    kn_cols = kn_rows.T
    v_cols = v_rows.T
    scale = HD_A ** -0.5
    qn = _rms_rows(q_ref[...].reshape(nb * H_A, HD_A), gq)
    own_all = jnp.concatenate([own] * nb, axis=0)
    qblk = jnp.where(own_all, jnp.concatenate([qn] * KVH_A, axis=-1), 0.0)
    qblk_b = qblk.astype(BF16)
    rows = [slice(i * H_A, (i + 1) * H_A) for i in range(nb)]
    s = jnp.concatenate([_dot(qblk_b[rows[i]], kt_ref[i].astype(BF16)) for i in range(nb)], axis=0) * scale
    k_new = jnp.concatenate([jnp.broadcast_to(kn_rows[i:i + 1], (H_A, kw)) for i in range(nb)], axis=0)
    v_new = jnp.concatenate([jnp.broadcast_to(v_rows[i:i + 1], (H_A, kw)) for i in range(nb)], axis=0)
    s_new = jnp.sum(qblk * k_new, axis=-1, keepdims=True) * scale
    sink_all = jnp.concatenate([sink] * nb, axis=0)
    m = jnp.maximum(jnp.maximum(jnp.max(s, axis=-1, keepdims=True), s_new), sink_all)
    e = jnp.exp(s - m)
    e_new = jnp.exp(s_new - m)
    den = jnp.sum(e, axis=-1, keepdims=True) + e_new + jnp.exp(sink_all - m)
    e_b = e.astype(BF16)
    o_all = jnp.concatenate([_dot_nt(e_b[rows[i]], vt_ref[i].astype(BF16)) for i in range(nb)], axis=0)
    o_all = jnp.where(own_all, o_all + e_new * v_new, 0.0)
    o = o_all[:, 0:HD_A]
    for kh in range(1, KVH_A):
        o = o + o_all[:, kh * HD_A:(kh + 1) * HD_A]
    o_ref[...] = (o / den).astype(BF16).reshape(nb, H_A, HD_A)
    for i in range(nb):
        kto_ref[i] = jnp.where(last, kn_cols[:, i:i + 1], pltpu.roll(kt_ref[i], buf_len - 1, 1))
        vto_ref[i] = jnp.where(last, v_cols[:, i:i + 1], pltpu.roll(vt_ref[i], buf_len - 1, 1))


def swa_decode(q, k, v, kt, vt, gq, gk, sinks):
    bsz, kw, buf_len = kt.shape
    nb = 8
    rows = pl.BlockSpec((nb, kw), lambda i: (i, 0))
    heads = pl.BlockSpec((nb, H_A, HD_A), lambda i: (i, 0, 0))
    buf = pl.BlockSpec((nb, kw, buf_len), lambda i: (i, 0, 0))
    vec = pl.BlockSpec((1, HD_A), lambda i: (0, 0))
    return pl.pallas_call(
        _swa_decode_kernel,
        out_shape=[
            jax.ShapeDtypeStruct((bsz, H_A, HD_A), BF16),
            jax.ShapeDtypeStruct(kt.shape, F32),
            jax.ShapeDtypeStruct(vt.shape, F32),
        ],
        grid=(bsz // nb,),
        in_specs=[heads, rows, rows, buf, buf, vec, vec, pl.BlockSpec((H_A, 1), lambda i: (0, 0))],
        out_specs=[heads, buf, buf],
        compiler_params=_cp(("parallel",)),
        name="swa_decode",
    )(q, k, v, kt, vt, gq.reshape(1, HD_A), gk.reshape(1, HD_A), sinks.reshape(H_A, 1))


def _gla_logdec(glr, wg2_ref, bg2_ref):
    gl = _dot(glr.astype(BF16), wg2_ref[...]) + bg2_ref[...]
    return _log_sigmoid(gl) * (1.0 / GLA_GATE_NORM)


def _col(row8, i):
    return row8.T[:, i:i + 1]


def _gla_out(o, g_go, gout):
    return (_rms_rows(o, g_go) * (gout * jax.nn.sigmoid(gout))).astype(BF16)


def _gla_prompt_kernel(q_ref, k_ref, v01_ref, v23_ref, gout01_ref, gout23_ref, glr_ref, wg2_ref, bg2_ref, ggo_ref,
                       o_ref, so_ref, s_ref):
    c_idx = pl.program_id(1)
    tb = q_ref.shape[0]
    ck = GLA_CHUNK

    @pl.when(c_idx == 0)
    def _():
        s_ref[...] = jnp.zeros_like(s_ref)

    nck = tb // ck
    ld = _gla_logdec(glr_ref[...], wg2_ref, bg2_ref)
    ti = lax.broadcasted_iota(jnp.int32, (tb, tb), 0)
    si = lax.broadcasted_iota(jnp.int32, (tb, tb), 1)
    causal = (si <= ti) & (si // ck == ti // ck)
    tri = jnp.where(causal, 1.0, 0.0).astype(BF16)
    hi, lo = _split_bf16(ld)
    b = _dot(tri, hi) + _dot(tri, lo)
    last_rows = [b[(c + 1) * ck - 1:(c + 1) * ck, :] for c in range(nck)]
    b_last = jnp.concatenate([jnp.broadcast_to(r, (ck, r.shape[1])) for r in last_rows], axis=0)
    k = k_ref[...]
    qt = ((q_ref[...] * (DK_B ** -0.5)) * jnp.exp(b)).astype(BF16)
    kt = (k * jnp.exp(-b)).astype(BF16)
    kd = (k * jnp.exp(b_last - b)).astype(BF16)
    g_go = ggo_ref[...]
    for h in range(H_B):
        ks = slice(h * DK_B, (h + 1) * DK_B)
        vs = slice(h * DV_B, (h + 1) * DV_B)
        hs = slice((h % 2) * DV_B, (h % 2 + 1) * DV_B)
        v = (v01_ref, v23_ref)[h // 2][:, hs].astype(BF16)
        att = jnp.where(causal, _dot_nt(qt[:, ks], kt[:, ks]), 0.0)
        o = _dot(att.astype(BF16), v)
        state = s_ref[h]
        o_state = []
        for c in range(nck):
            rs = slice(c * ck, (c + 1) * ck)
            o_state.append(_dot(qt[rs, ks], state.astype(BF16)))
            d_state = _dot_tn(kd[rs, ks], v[rs])
            decay = _col(jnp.broadcast_to(jnp.exp(last_rows[c][:, ks]), (8, DK_B)), 0)
            state = decay * state + d_state
        s_ref[h] = state
        o = o + jnp.concatenate(o_state, axis=0)
        o_ref[:, vs] = _gla_out(o, g_go, (gout01_ref, gout23_ref)[h // 2][:, hs])
    so_ref[0] = s_ref[...]


def gla_prompt(z0, glr, wg2, bg2, g_go, bsz, t):
    tb = 256
    nt = t // tb
    qk_w = H_B * DK_B
    v_w = H_B * DV_B
    return pl.pallas_call(
        _gla_prompt_kernel,
        out_shape=[
            jax.ShapeDtypeStruct((bsz * t, v_w), BF16),
            jax.ShapeDtypeStruct((bsz, H_B, DK_B, DV_B), F32),
        ],
        grid=(bsz, nt),
        in_specs=[
            pl.BlockSpec((tb, qk_w), lambda b, c: (b * nt + c, Z0_QB // qk_w)),
            pl.BlockSpec((tb, qk_w), lambda b, c: (b * nt + c, Z0_KB // qk_w)),
            pl.BlockSpec((tb, v_w // 2), lambda b, c: (b * nt + c, Z0_VB // (v_w // 2))),
            pl.BlockSpec((tb, v_w // 2), lambda b, c: (b * nt + c, Z0_VB // (v_w // 2) + 1)),
            pl.BlockSpec((tb, v_w // 2), lambda b, c: (b * nt + c, Z0_GOUT // (v_w // 2))),
            pl.BlockSpec((tb, v_w // 2), lambda b, c: (b * nt + c, Z0_GOUT // (v_w // 2) + 1)),
            pl.BlockSpec((tb, 128), lambda b, c: (b * nt + c, 0)),
            pl.BlockSpec((128, qk_w), lambda b, c: (0, 0)),
            pl.BlockSpec((1, qk_w), lambda b, c: (0, 0)),
            pl.BlockSpec((1, DV_B), lambda b, c: (0, 0)),
        ],
        out_specs=[
            pl.BlockSpec((tb, v_w), lambda b, c: (b * nt + c, 0)),
            pl.BlockSpec((1, H_B, DK_B, DV_B), lambda b, c: (b, 0, 0, 0)),
        ],
        scratch_shapes=[pltpu.VMEM((H_B, DK_B, DV_B), F32)],
        compiler_params=_cp(("parallel", "arbitrary")),
        name="gla_prompt",
    )(z0, z0, z0, z0, z0, z0, glr, wg2, bg2.reshape(1, qk_w), g_go.reshape(1, DV_B))


def _gla_decode_kernel(q_ref, k_ref, v_ref, gout_ref, glr_ref, wg2_ref, bg2_ref, ggo_ref, s_ref, o_ref, so_ref):
    nb = q_ref.shape[0]
    ld = _gla_logdec(glr_ref[...], wg2_ref, bg2_ref)
    g_go = ggo_ref[...]
    for h in range(H_B):
        ks = slice(h * DK_B, (h + 1) * DK_B)
        vs = slice(h * DV_B, (h + 1) * DV_B)
        eg_t = jnp.exp(ld[:, ks]).T
        k_t = k_ref[:, ks].T
        q_t = (q_ref[:, ks] * (DK_B ** -0.5)).T
        for i in range(nb):
            state = eg_t[:, i:i + 1] * s_ref[i, h] + k_t[:, i:i + 1] * v_ref[i:i + 1, vs]
            so_ref[i, h] = state
            o = jnp.sum(q_t[:, i:i + 1] * state, axis=0, keepdims=True)
            o_ref[i:i + 1, vs] = _gla_out(o, g_go, gout_ref[i:i + 1, vs])


def gla_decode(q, k, v, gout, glr, wg2, bg2, g_go, state):
    bsz = q.shape[0]
    nb = 8
    qk_w = H_B * DK_B
    v_w = H_B * DV_B
    rows = lambda w: pl.BlockSpec((nb, w), lambda i: (i, 0))
    st = pl.BlockSpec((nb, H_B, DK_B, DV_B), lambda i: (i, 0, 0, 0))
    return pl.pallas_call(
        _gla_decode_kernel,
        out_shape=[
            jax.ShapeDtypeStruct((bsz, v_w), BF16),
            jax.ShapeDtypeStruct(state.shape, F32),
        ],
        grid=(bsz // nb,),
        in_specs=[
            rows(qk_w), rows(qk_w), rows(v_w), rows(v_w), rows(128),
            pl.BlockSpec((128, qk_w), lambda i: (0, 0)),
            pl.BlockSpec((1, qk_w), lambda i: (0, 0)),
            pl.BlockSpec((1, DV_B), lambda i: (0, 0)),
            st,
        ],
        out_specs=[rows(v_w), st],
        compiler_params=_cp(("parallel",)),
        name="gla_decode",
    )(q, k, v, gout, glr, wg2, bg2.reshape(1, qk_w), g_go.reshape(1, DV_B), state)


Z1_CX, Z1_CY, Z1_QD, Z1_KD, Z1_VD = 0, 1024, 2048, 3072, 3584


def _rg_gates(xc, wa_ref, ba_ref, wx_ref, bx_ref, lam_ref):
    xb = xc.astype(BF16)
    ra = jnp.concatenate(
        [_dot(xb[:, n * C_BLOCK:(n + 1) * C_BLOCK], wa_ref[n]) for n in range(C_BLOCKS)], axis=-1) + ba_ref[...]
    rx = jnp.concatenate(
        [_dot(xb[:, n * C_BLOCK:(n + 1) * C_BLOCK], wx_ref[n]) for n in range(C_BLOCKS)], axis=-1) + bx_ref[...]
    r_gate = jax.nn.sigmoid(ra)
    i_gate = jax.nn.sigmoid(rx)
    log_a = (-RG_C * r_gate) * _softplus(-lam_ref[...])
    a = jnp.exp(log_a)
    one_minus_a2 = -jnp.tanh(log_a) * (a * a + 1.0)
    u = jnp.sqrt(one_minus_a2) * (i_gate * xc)
    return a, u


def _rglru_prompt_kernel(cx_ref, cy_ref, cw_ref, cb_ref, wa_ref, ba_ref, wx_ref, bx_ref, lam_ref,
                         o_ref, tail_ref, hl_ref, prev_ref, hc_ref):
    t_idx = pl.program_id(1)
    tt = cx_ref.shape[0]

    @pl.when(t_idx == 0)
    def _():
        prev_ref[...] = jnp.zeros_like(prev_ref)
        hc_ref[...] = jnp.zeros_like(hc_ref)

    x = cx_ref[...]
    xp = jnp.concatenate([prev_ref[...], x], axis=0)
    xc = cb_ref[...] + cw_ref[CONV_W - 1:CONV_W, :] * x
    for j in range(CONV_W - 1):
        d = CONV_W - 1 - j
        xc = xc + cw_ref[j:j + 1, :] * pltpu.roll(xp, d, 0)[8:]
    prev_ref[...] = x[tt - 8:]
    tail_ref[0] = x[tt - 8:]

    a, u = _rg_gates(xc, wa_ref, ba_ref, wx_ref, bx_ref, lam_ref)
    sub = lax.broadcasted_iota(jnp.int32, a.shape, 0) & 7
    for d in (1, 2, 4):
        inside = sub >= d
        u = a * jnp.where(inside, pltpu.roll(u, d, 0), 0.0) + u
        a = a * jnp.where(inside, pltpu.roll(a, d, 0), 1.0)
    h_prev = hc_ref[...]
    tiles = []
    for r in range(0, tt, 8):
        tiles.append(a[r:r + 8] * h_prev + u[r:r + 8])
        h_prev = tiles[-1][7:8]
    h = jnp.concatenate(tiles, axis=0)
    hc_ref[...] = h[tt - 1:]
    hl_ref[0] = h[tt - 1:]
    o_ref[...] = (h * jax.nn.gelu(cy_ref[...])).astype(BF16)


def rglru_prompt(z1, conv_w, conv_b, wa, ba, wx, bx, lam, bsz, t):
    tt = 256
    nt = t // tt
    c = C_WIDTH
    vec = pl.BlockSpec((1, c), lambda b, i: (0, 0))
    wblk = pl.BlockSpec((C_BLOCKS, C_BLOCK, C_BLOCK), lambda b, i: (0, 0, 0))
    return pl.pallas_call(
        _rglru_prompt_kernel,
        out_shape=[
            jax.ShapeDtypeStruct((bsz * t, c), BF16),
            jax.ShapeDtypeStruct((bsz, 8, c), F32),
            jax.ShapeDtypeStruct((bsz, 1, c), F32),
        ],
        grid=(bsz, nt),
        in_specs=[
            pl.BlockSpec((tt, c), lambda b, i: (b * nt + i, Z1_CX // c)),
            pl.BlockSpec((tt, c), lambda b, i: (b * nt + i, Z1_CY // c)),
            pl.BlockSpec((CONV_W, c), lambda b, i: (0, 0)),
            vec, wblk, vec, wblk, vec, vec,
        ],
        out_specs=[
            pl.BlockSpec((tt, c), lambda b, i: (b * nt + i, 0)),
            pl.BlockSpec((1, 8, c), lambda b, i: (b, 0, 0)),
            pl.BlockSpec((1, 1, c), lambda b, i: (b, 0, 0)),
        ],
        scratch_shapes=[pltpu.VMEM((8, c), F32), pltpu.VMEM((1, c), F32)],
        compiler_params=_cp(("parallel", "arbitrary")),
        name="rglru_prompt",
    )(z1, z1, conv_w, conv_b.reshape(1, c), wa, ba.reshape(1, c), wx, bx.reshape(1, c), lam.reshape(1, c))


def _rglru_decode_kernel(cx_ref, cy_ref, buf_ref, h0_ref, cw_ref, cb_ref, wa_ref, ba_ref, wx_ref, bx_ref, lam_ref,
                         o_ref, nbuf_ref, hl_ref):
    x = cx_ref[...]
    xc = cb_ref[...] + cw_ref[CONV_W - 1:CONV_W, :] * x
    for j in range(CONV_W - 1):
        xc = xc + cw_ref[j:j + 1, :] * buf_ref[j]
    for j in range(CONV_W - 2):
        nbuf_ref[j] = buf_ref[j + 1]
    nbuf_ref[CONV_W - 2] = x
    a, u = _rg_gates(xc, wa_ref, ba_ref, wx_ref, bx_ref, lam_ref)
    h = a * h0_ref[...] + u
    hl_ref[...] = h
    o_ref[...] = (h * jax.nn.gelu(cy_ref[...])).astype(BF16)


def rglru_decode(cx, cy, buf, h0, conv_w, conv_b, wa, ba, wx, bx, lam):
    bsz, c = cx.shape
    return pl.pallas_call(
        _rglru_decode_kernel,
        out_shape=[
            jax.ShapeDtypeStruct((bsz, c), BF16),
            jax.ShapeDtypeStruct(buf.shape, F32),
            jax.ShapeDtypeStruct((bsz, c), F32),
        ],
        compiler_params=pltpu.CompilerParams(vmem_limit_bytes=V7X_VMEM_LIMIT_BYTES),
        name="rglru_decode",
    )(cx, cy, buf, h0, conv_w, conv_b.reshape(1, c), wa, ba.reshape(1, c), wx, bx.reshape(1, c), lam.reshape(1, c))


def _strict_upper_ones(n):
    j = lax.broadcasted_iota(jnp.int32, (n, n), 0)
    s = lax.broadcasted_iota(jnp.int32, (n, n), 1)
    return jnp.where(j > s, 1.0, 0.0).astype(BF16)


def _sb_tile(z, later, surv, mask):
    sp = _softplus(z)
    l1m = -sp
    lsg = z - sp
    if mask is not None:
        l1m = jnp.where(mask, l1m, 0.0)
    hi, lo = _split_bf16(l1m)
    suffix = _dot(hi, later) + _dot(lo, later)
    w = jnp.exp(lsg + suffix + surv)
    if mask is not None:
        w = jnp.where(mask, w, 0.0)
    return w, jnp.sum(l1m, axis=-1, keepdims=True)


def _sb_prompt_kernel(bias_ref, q_ref, k_ref, v_ref, gq_ref, gk_ref, o_ref, ko_ref, kb_ref, vb_ref):
    kh = pl.program_id(1)
    i = pl.program_id(2)
    blk = q_ref.shape[0]
    grp = H_D // KVH_D

    @pl.when(i == 0)
    def _():
        kn = _rms_rows(k_ref[...], gk_ref[...])
        ko_ref[...] = kn
        kb_ref[...] = kn.astype(BF16)
        vb_ref[...] = v_ref[...].astype(BF16)

    gq = gq_ref[...]
    qs = jnp.concatenate(
        [_rms_rows(q_ref[:, g * HD_D:(g + 1) * HD_D], gq) for g in range(grp)], axis=0).astype(BF16)
    bias = jnp.concatenate(
        [jnp.full((blk, 1), bias_ref[kh * grp + g], F32) for g in range(grp)], axis=0)
    later = _strict_upper_ones(blk)
    scale = HD_D ** -0.5

    def tile(j, mask, acc, surv):
        off = pl.multiple_of(j * blk, blk)
        z = _dot_nt(qs, kb_ref[pl.ds(off, blk), :]) * scale + bias
        w, tot = _sb_tile(z, later, surv, mask)
        return acc + _dot(w.astype(BF16), vb_ref[pl.ds(off, blk), :]), surv + tot

    rows = lax.broadcasted_iota(jnp.int32, (grp * blk, blk), 0) & (blk - 1)
    cols = lax.broadcasted_iota(jnp.int32, (grp * blk, blk), 1)
    acc, surv = tile(i, cols < rows, jnp.zeros((grp * blk, HD_D), F32), jnp.zeros((grp * blk, 1), F32))

    def pair(step, carry):
        j = i - 1 - 2 * step
        return tile(j - 1, None, *tile(j, None, *carry))

    def last(step, carry):
        return tile(0, None, *carry)

    n_pairs = i // 2
    carry = lax.fori_loop(0, n_pairs, pair, (acc, surv))
    acc, _ = lax.fori_loop(0, i - 2 * n_pairs, last, carry)
    for g in range(grp):
        o_ref[:, g * HD_D:(g + 1) * HD_D] = acc[g * blk:(g + 1) * blk].astype(BF16)


def sb_prompt(z1, gq, gk, bias, bsz, t):
    blk = SB_PROMPT_TILE
    nb = t // blk
    grp = H_D // KVH_D
    qw = grp * HD_D
    grid_spec = pltpu.PrefetchScalarGridSpec(
        num_scalar_prefetch=1,
        grid=(bsz, KVH_D, nb),
        in_specs=[
            pl.BlockSpec((blk, qw), lambda b, k, i, s: (b * nb + i, Z1_QD // qw + k)),
            pl.BlockSpec((t, HD_D), lambda b, k, i, s: (b, Z1_KD // HD_D + k)),
            pl.BlockSpec((t, HD_D), lambda b, k, i, s: (b, Z1_VD // HD_D + k)),
            pl.BlockSpec((1, HD_D), lambda b, k, i, s: (0, 0)),
            pl.BlockSpec((1, HD_D), lambda b, k, i, s: (0, 0)),
        ],
        out_specs=[
            pl.BlockSpec((blk, qw), lambda b, k, i, s: (b * nb + i, k)),
            pl.BlockSpec((t, HD_D), lambda b, k, i, s: (b, k)),
        ],
        scratch_shapes=[pltpu.VMEM((t, HD_D), BF16), pltpu.VMEM((t, HD_D), BF16)],
    )
    return pl.pallas_call(
        _sb_prompt_kernel,
        out_shape=[
            jax.ShapeDtypeStruct((bsz * t, H_D * HD_D), BF16),
            jax.ShapeDtypeStruct((bsz * t, KVH_D * HD_D), F32),
        ],
        grid_spec=grid_spec,
        compiler_params=_cp(("parallel", "parallel", "arbitrary")),
        name="sb_prompt",
    )(bias, z1, z1, z1, gq.reshape(1, HD_D), gk.reshape(1, HD_D))


def _sb_decode_kernel(n_pages, pt_ref, bias_ref, q_ref, k_ref, v_ref, gq_ref, gk_ref, ck_hbm, cv_hbm,
                      o_ref, ko_ref, kbuf, vbuf, sem):
    b = pl.program_id(0)
    nb = pl.num_programs(0)
    npg = SB_PAGES_PER_STEP
    n_chunks = n_pages // npg
    prow = PAGE * KVH_D
    grp = H_D // KVH_D
    scale = HD_D ** -0.5

    def page_copies(bb, chunk, slot):
        cps = []
        for p in range(npg):
            page = pt_ref[bb * n_pages + chunk * npg + p]
            dst = pl.ds(p * prow, prow)
            cps.append(pltpu.make_async_copy(ck_hbm.at[page], kbuf.at[slot, dst], sem.at[0, slot]))
            cps.append(pltpu.make_async_copy(cv_hbm.at[page], vbuf.at[slot, dst], sem.at[1, slot]))
        return cps

    @pl.when(b == 0)
    def _():
        for cp in page_copies(0, n_chunks - 1, 0):
            cp.start()

    bias = jnp.concatenate([jnp.full((1, 1), bias_ref[h], F32) for h in range(H_D)], axis=0)
    qn = _rms_rows(q_ref[0], gq_ref[...])
    qb = qn.astype(BF16)
    kn = _rms_rows(k_ref[0], gk_ref[...])
    ko_ref[0] = kn
    kn_sel = jnp.concatenate([kn[h // grp:h // grp + 1] for h in range(H_D)], axis=0)
    v_sel = jnp.concatenate([v_ref[0, h // grp:h // grp + 1] for h in range(H_D)], axis=0)
    z0 = jnp.sum(qn * kn_sel, axis=-1, keepdims=True) * scale + bias
    visible = jnp.zeros((H_D, 1), jnp.int32) < jnp.zeros((H_D, 1), jnp.int32)
    sp0 = _softplus(z0)
    acc = jnp.where(visible, jnp.exp(z0 - sp0), 0.0) * v_sel
    surv = jnp.where(visible, -sp0, 0.0)

    later = _strict_upper_ones(prow)
    row = lax.broadcasted_iota(jnp.int32, (npg * H_D, prow), 0)
    col = lax.broadcasted_iota(jnp.int32, (npg * H_D, prow), 1)
    valid = (col & (KVH_D - 1)) == ((row & (H_D - 1)) // grp)
    bias_r = jnp.concatenate([bias] * npg, axis=0)

    for i in range(n_chunks):
        slot = i % 2
        if i + 1 < n_chunks:
            for cp in page_copies(b, n_chunks - 2 - i, 1 - slot):
                cp.start()
        else:
            @pl.when(b + 1 < nb)
            def _():
                for cp in page_copies(b + 1, n_chunks - 1, 1 - slot):
                    cp.start()
        for cp in page_copies(b, n_chunks - 1 - i, slot):
            cp.wait()

        z = _dot_nt(qb, kbuf[slot].astype(BF16))
        z = jnp.concatenate([z[:, p * prow:(p + 1) * prow] for p in range(npg)], axis=0) * scale + bias_r
        sp = _softplus(z)
        l1m = jnp.where(valid, -sp, 0.0)
        hi, lo = _split_bf16(l1m)
        suffix = _dot(hi, later) + _dot(lo, later)
        tot = jnp.sum(l1m, axis=-1, keepdims=True)
        survs = [None] * npg
        for p in reversed(range(npg)):
            survs[p] = surv
            surv = surv + tot[p * H_D:(p + 1) * H_D]
        w = jnp.where(valid, jnp.exp((z - sp) + suffix + jnp.concatenate(survs, axis=0)), 0.0)
        w = jnp.concatenate([w[p * H_D:(p + 1) * H_D] for p in range(npg)], axis=1).astype(BF16)
        acc = acc + _dot(w, vbuf[slot].astype(BF16))

    o_ref[0] = acc.astype(BF16)


def sb_decode(q, k, v, gq, gk, bias, cache_k, cache_v, page_table):
    bsz = q.shape[0]
    n_pages = page_table.shape[1]
    npg = SB_PAGES_PER_STEP
    assert n_pages % (2 * npg) == 0
    prow = PAGE * KVH_D
    grid_spec = pltpu.PrefetchScalarGridSpec(
        num_scalar_prefetch=2,
        grid=(bsz,),
        in_specs=[
            pl.BlockSpec((1, H_D, HD_D), lambda b, pt, bias_: (b, 0, 0)),
            pl.BlockSpec((1, KVH_D, HD_D), lambda b, pt, bias_: (b, 0, 0)),
            pl.BlockSpec((1, KVH_D, HD_D), lambda b, pt, bias_: (b, 0, 0)),
            pl.BlockSpec((1, HD_D), lambda b, pt, bias_: (0, 0)),
            pl.BlockSpec((1, HD_D), lambda b, pt, bias_: (0, 0)),
            pl.BlockSpec(memory_space=pl.ANY),
            pl.BlockSpec(memory_space=pl.ANY),
        ],
        out_specs=[
            pl.BlockSpec((1, H_D, HD_D), lambda b, pt, bias_: (b, 0, 0)),
            pl.BlockSpec((1, KVH_D, HD_D), lambda b, pt, bias_: (b, 0, 0)),
        ],
        scratch_shapes=[
            pltpu.VMEM((2, npg * prow, HD_D), F32),
            pltpu.VMEM((2, npg * prow, HD_D), F32),
            pltpu.SemaphoreType.DMA((2, 2)),
        ],
    )
    return pl.pallas_call(
        functools.partial(_sb_decode_kernel, n_pages),
        out_shape=[
            jax.ShapeDtypeStruct((bsz, H_D, HD_D), BF16),
            jax.ShapeDtypeStruct((bsz, KVH_D, HD_D), F32),
        ],
        grid_spec=grid_spec,
        compiler_params=_cp(("arbitrary",)),
        name="sb_decode",
    )(page_table.reshape(-1), bias, q, k, v, gq.reshape(1, HD_D), gk.reshape(1, HD_D), cache_k, cache_v)


def kernel(x_prompt, x_sample, state_swa_k, state_swa_v, state_gla, state_conv, state_lru, cache_sb_k, cache_sb_v, page_table, p_prompt, p_sample, g_mix, g_ffn, w_ffn_gate, w_ffn_up, w_ffn_down, g_ple, w_ple_gate, w_ple_proj, w_in_even, g_qnorm_a, g_knorm_a, sinks_a, w_gla_gate2, b_gla_gate2, g_gla_out, w_out_even, w_in_odd, conv_w, conv_b, w_rg_a, b_rg_a, w_rg_x, b_rg_x, lru_lambda, g_qnorm_d, g_knorm_d, sb_bias, w_out_odd):
    bsz, t, d = x_prompt.shape
    dbs = x_sample.shape[0]
    n_p = bsz * t
    depth = g_mix.shape[0]
    ple = p_prompt.shape[-1]

    x_p = x_prompt.reshape(n_p, d)
    x_s = x_sample.reshape(dbs, d)
    p_all = jnp.concatenate([p_prompt.reshape(depth, n_p, ple), p_sample.reshape(depth, dbs, ple)], axis=1)

    ffn_w = (w_ffn_gate.astype(BF16), w_ffn_up.astype(BF16), w_ffn_down.astype(BF16))
    ple_w = (w_ple_gate.astype(BF16), w_ple_proj.astype(BF16))

    def dense_tail(h, i, n_tail=0):
        h = ffn_residual(h, g_ffn[i], *ffn_w, i, 512, f"ffn_{i}")
        return ple_residual(h, g_ple[i], p_all, *ple_w, i, f"ple_{i}", n_tail)

    w = w_in_even[0]
    glr_col = Z0_GOUT
    w0 = jnp.concatenate([w[:, :glr_col], w[:, glr_col + GLA_RANK:]], axis=1).astype(BF16)
    w0_glr = jnp.pad(w[:, glr_col:glr_col + GLA_RANK], ((0, 0), (0, 128 - GLA_RANK))).astype(BF16)
    z0, glr = norm_matmul(x_p, g_mix[0], w0, 1536, "in_proj_even", w_side=w0_glr, x_tail=x_s)
    wg2 = jnp.pad(w_gla_gate2[0], ((0, 128 - GLA_RANK), (0, 0))).astype(BF16)

    oa_p, swa_k_p, swa_v_p = swa_prompt(z0, g_qnorm_a[0], g_knorm_a[0], sinks_a[0], bsz, t)
    ob_p, gla_p = gla_prompt(z0, glr, wg2, b_gla_gate2[0], g_gla_out[0], bsz, t)

    zd = z0[n_p:]
    kw_a = KVH_A * HD_A
    to_keys_last = lambda s: jnp.transpose(s, (0, 2, 3, 1)).reshape(dbs, kw_a, -1)
    from_keys_last = lambda s: jnp.transpose(s.reshape(dbs, KVH_A, HD_A, -1), (0, 3, 1, 2))[None]
    oa_d, swa_kt_d, swa_vt_d = swa_decode(
        zd[:, Z0_QA:Z0_QA + 1024].reshape(dbs, H_A, HD_A),
        zd[:, Z0_KA:Z0_KA + kw_a], zd[:, Z0_VA:Z0_VA + kw_a],
        to_keys_last(state_swa_k[0]), to_keys_last(state_swa_v[0]),
        g_qnorm_a[0], g_knorm_a[0], sinks_a[0])
    ob_d, gla_d = gla_decode(
        zd[:, Z0_QB:Z0_QB + 512], zd[:, Z0_KB:Z0_KB + 512], zd[:, Z0_VB:Z0_VB + 1024],
        zd[:, Z0_GOUT:Z0_GOUT + 1024], glr[n_p:], wg2, b_gla_gate2[0], g_gla_out[0], state_gla[0])

    wo = w_out_even[0].astype(BF16)
    h = proj_residual(x_p, oa_p, ob_p, oa_d.reshape(dbs, 1024), ob_d, wo[:1024], wo[1024:], "out_proj_even",
                      h_tail=x_s)
    h = dense_tail(h, 0)

    z1 = norm_matmul(h, g_mix[1], w_in_odd[0].astype(BF16), 1024, "in_proj_odd")
    wa = w_rg_a[0].astype(BF16)
    wx = w_rg_x[0].astype(BF16)
    oc_p, conv_tail, lru_p = rglru_prompt(z1, conv_w[0], conv_b[0], wa, b_rg_a[0], wx, b_rg_x[0], lru_lambda[0], bsz, t)
    od_p, sb_k_p = sb_prompt(z1, g_qnorm_d[0], g_knorm_d[0], sb_bias[0], bsz, t)

    zd = z1[n_p:]
    oc_d, conv_d, lru_d = rglru_decode(
        zd[:, Z1_CX:Z1_CX + 1024], zd[:, Z1_CY:Z1_CY + 1024],
        jnp.swapaxes(state_conv[0], 0, 1), state_lru[0],
        conv_w[0], conv_b[0], wa, b_rg_a[0], wx, b_rg_x[0], lru_lambda[0])
    kvw = KVH_D * HD_D
    n_phys = cache_sb_k.shape[1]
    od_d, sb_k_d = sb_decode(
        zd[:, Z1_QD:Z1_QD + 1024].reshape(dbs, H_D, HD_D),
        zd[:, Z1_KD:Z1_KD + kvw].reshape(dbs, KVH_D, HD_D),
        zd[:, Z1_VD:Z1_VD + kvw].reshape(dbs, KVH_D, HD_D),
        g_qnorm_d[0], g_knorm_d[0], sb_bias[0],
        cache_sb_k[0].reshape(n_phys, PAGE * KVH_D, HD_D), cache_sb_v[0].reshape(n_phys, PAGE * KVH_D, HD_D),
        page_table)

    wo = w_out_odd[0].astype(BF16)
    h = proj_residual(h, oc_p, od_p, oc_d, od_d.reshape(dbs, 1024), wo[:1024], wo[1024:], "out_proj_odd")
    y_prompt, y_sample = dense_tail(h, 1, n_tail=dbs)

    n_keep = min(WINDOW, t)
    return (
        y_prompt.reshape(bsz, t, d), y_sample.reshape(dbs, 1, d),
        swa_k_p.reshape(1, bsz, n_keep, KVH_A, HD_A), swa_v_p.reshape(1, bsz, n_keep, KVH_A, HD_A),
        gla_p[None],
        conv_tail[:, 8 - (CONV_W - 1):][None], lru_p.reshape(1, bsz, C_WIDTH),
        sb_k_p.reshape(1, bsz, t, KVH_D, HD_D), z1[:n_p, Z1_VD:].reshape(1, bsz, t, KVH_D, HD_D),
        from_keys_last(swa_kt_d), from_keys_last(swa_vt_d),
        gla_d[None],
        jnp.swapaxes(conv_d, 0, 1)[None], lru_d[None],
        sb_k_d.reshape(1, dbs, 1, KVH_D, HD_D), zd[:, Z1_VD:].reshape(1, dbs, 1, KVH_D, HD_D),
    )
```

```python
import functools

import jax
import jax.numpy as jnp
from jax import lax
from jax.experimental import pallas as pl
from jax.experimental.pallas import tpu as pltpu

F32 = jnp.float32
BF16 = jnp.bfloat16
EPS = 1e-6

V7X_VMEM_LIMIT_BYTES = 56 * 1024 * 1024

H_A, KVH_A, HD_A, WINDOW = 16, 4, 64, 128
H_B, DK_B, DV_B, GLA_RANK, GLA_GATE_NORM, GLA_CHUNK = 4, 128, 256, 16, 16.0, 64
C_WIDTH, C_BLOCKS, CONV_W, RG_C = 1024, 8, 4, 8.0
C_BLOCK = C_WIDTH // C_BLOCKS
H_D, KVH_D, HD_D = 8, 4, 128
PAGE = 128

ROW_TILE = 640
WIDE_ROW_TILE = 1040
SB_PROMPT_TILE = 256
SB_PAGES_PER_STEP = 16


def _cp(sem):
    return pltpu.CompilerParams(dimension_semantics=sem, vmem_limit_bytes=V7X_VMEM_LIMIT_BYTES)


def _rms_rows(x, g):
    r = lax.rsqrt(jnp.mean(x * x, axis=-1, keepdims=True) + EPS)
    return (x * r) * g


def _softplus(x):
    return jnp.maximum(x, 0.0) + jnp.log(1.0 + jnp.exp(-jnp.abs(x)))


def _log_sigmoid(x):
    return jnp.minimum(x, 0.0) - jnp.log(1.0 + jnp.exp(-jnp.abs(x)))


def _split_bf16(x):
    hi = x.astype(BF16)
    lo = (x - hi.astype(F32)).astype(BF16)
    return hi, lo


def _dot(a, b):
    return jnp.dot(a, b, preferred_element_type=F32)


def _dot_nt(a, b):
    return lax.dot_general(a, b, (((1,), (1,)), ((), ())), preferred_element_type=F32)


def _dot_tn(a, b):
    return lax.dot_general(a, b, (((0,), (0,)), ((), ())), preferred_element_type=F32)


def _norm_matmul_kernel(has_tail, has_side, x_ref, g_ref, w_ref, *rest):
    rest = list(rest)
    xt_ref = rest.pop(0) if has_tail else None
    ws_ref = rest.pop(0) if has_side else None
    o_ref = rest.pop(0)
    os_ref = rest.pop(0) if has_side else None
    xn_ref, = rest

    @pl.when(pl.program_id(1) == 0)
    def _():
        xn_ref[...] = _rms_rows(x_ref[...], g_ref[...]).astype(BF16)
        if has_tail:
            @pl.when(pl.program_id(0) == pl.num_programs(0) - 1)
            def _():
                nt = xt_ref.shape[0]
                xn_ref[xn_ref.shape[0] - nt:, :] = _rms_rows(xt_ref[...], g_ref[...]).astype(BF16)
        if has_side:
            os_ref[...] = _dot(xn_ref[...], ws_ref[...])

    o_ref[...] = _dot(xn_ref[...], w_ref[...])


def norm_matmul(x, g, w, tn, name, w_side=None, x_tail=None):
    d = x.shape[1]
    nt = 0 if x_tail is None else x_tail.shape[0]
    n = x.shape[0] + nt
    nout = w.shape[1]
    tm = WIDE_ROW_TILE
    assert n % tm == 0 and (nt == 0 or x.shape[0] % tm + nt == tm)
    in_specs = [
        pl.BlockSpec((tm, d), lambda i, j: (i, 0)),
        pl.BlockSpec((1, d), lambda i, j: (0, 0)),
        pl.BlockSpec((d, tn), lambda i, j: (0, j)),
    ]
    out_shape = [jax.ShapeDtypeStruct((n, nout), F32)]
    out_specs = [pl.BlockSpec((tm, tn), lambda i, j: (i, j))]
    args = [x, g.reshape(1, d), w]
    if x_tail is not None:
        in_specs.append(pl.BlockSpec((nt, d), lambda i, j: (0, 0)))
        args.append(x_tail)
    if w_side is not None:
        ns = w_side.shape[1]
        in_specs.append(pl.BlockSpec((d, ns), lambda i, j: (0, 0)))
        out_shape.append(jax.ShapeDtypeStruct((n, ns), F32))
        out_specs.append(pl.BlockSpec((tm, ns), lambda i, j: (i, 0)))
        args.append(w_side)
    out = pl.pallas_call(
        functools.partial(_norm_matmul_kernel, x_tail is not None, w_side is not None),
        out_shape=out_shape,
        grid=(n // tm, nout // tn),
        in_specs=in_specs,
        out_specs=out_specs,
        scratch_shapes=[pltpu.VMEM((tm, d), BF16)],
        compiler_params=_cp(("parallel", "arbitrary")),
        name=name,
    )(*args)
    return out if w_side is not None else out[0]


def _proj_res_kernel(h_ref, a_ref, b_ref, at_ref, bt_ref, wa_ref, wb_ref, *rest):
    ht_ref, o_ref = rest if len(rest) == 2 else (None, rest[0])
    tm = h_ref.shape[0]
    nt = at_ref.shape[0]
    o_ref[...] = h_ref[...] + (_dot(a_ref[...], wa_ref[...]) + _dot(b_ref[...], wb_ref[...]))

    @pl.when(pl.program_id(0) == pl.num_programs(0) - 1)
    def _():
        tail = _dot(at_ref[...], wa_ref[...]) + _dot(bt_ref[...], wb_ref[...])
        h_tail = h_ref[tm - nt:, :] if ht_ref is None else ht_ref[...]
        o_ref[tm - nt:, :] = h_tail + tail


def proj_residual(h, a, b, a_tail, b_tail, wa, wb, name, h_tail=None):
    d = h.shape[1]
    ka, kb = a.shape[1], b.shape[1]
    nt = a_tail.shape[0]
    n = a.shape[0] + nt
    tm = ROW_TILE
    assert h.shape[0] == (n if h_tail is None else a.shape[0]) and n % tm == 0 and a.shape[0] % tm + nt == tm
    in_specs = [
        pl.BlockSpec((tm, d), lambda i: (i, 0)),
        pl.BlockSpec((tm, ka), lambda i: (i, 0)),
        pl.BlockSpec((tm, kb), lambda i: (i, 0)),
        pl.BlockSpec((nt, ka), lambda i: (0, 0)),
        pl.BlockSpec((nt, kb), lambda i: (0, 0)),
        pl.BlockSpec((ka, d), lambda i: (0, 0)),
        pl.BlockSpec((kb, d), lambda i: (0, 0)),
    ]
    args = [h, a, b, a_tail, b_tail, wa, wb]
    if h_tail is not None:
        in_specs.append(pl.BlockSpec((nt, d), lambda i: (0, 0)))
        args.append(h_tail)
    return pl.pallas_call(
        _proj_res_kernel,
        out_shape=jax.ShapeDtypeStruct((n, d), F32),
        grid=(n // tm,),
        in_specs=in_specs,
        out_specs=pl.BlockSpec((tm, d), lambda i: (i, 0)),
        compiler_params=_cp(("parallel",)),
        name=name,
    )(*args)


def _ffn_kernel(h_ref, g_ref, wg_ref, wu_ref, wd_ref, o_ref, hn_ref):
    @pl.when(pl.program_id(1) == 0)
    def _():
        h = h_ref[...]
        hn_ref[...] = _rms_rows(h, g_ref[...]).astype(BF16)
        o_ref[...] = h

    hn = hn_ref[...]
    gate = _dot(hn, wg_ref[...])
    up = _dot(hn, wu_ref[...])
    act = (gate * jax.nn.sigmoid(gate)) * up
    o_ref[...] += _dot(act.astype(BF16), wd_ref[...])


def ffn_residual(h, g, wg, wu, wd, layer, tf, name):
    n, d = h.shape
    dff = wg.shape[2]
    tm = WIDE_ROW_TILE
    return pl.pallas_call(
        _ffn_kernel,
        out_shape=jax.ShapeDtypeStruct((n, d), F32),
        grid=(n // tm, dff // tf),
        in_specs=[
            pl.BlockSpec((tm, d), lambda i, f: (i, 0)),
            pl.BlockSpec((1, d), lambda i, f: (0, 0)),
            pl.BlockSpec((None, d, tf), lambda i, f: (layer, 0, f)),
            pl.BlockSpec((None, d, tf), lambda i, f: (layer, 0, f)),
            pl.BlockSpec((None, tf, d), lambda i, f: (layer, f, 0)),
        ],
        out_specs=pl.BlockSpec((tm, d), lambda i, f: (i, 0)),
        scratch_shapes=[pltpu.VMEM((tm, d), BF16)],
        compiler_params=_cp(("parallel", "arbitrary")),
        name=name,
    )(h, g.reshape(1, d), wg, wu, wd)


def _ple_kernel(h_ref, g_ref, p_ref, wg_ref, wp_ref, o_ref, *tail_ref):
    h = h_ref[...]
    hn = _rms_rows(h, g_ref[...]).astype(BF16)
    gate = jax.nn.sigmoid(_dot(hn, wg_ref[...]))
    proj = _dot(p_ref[...].astype(BF16), wp_ref[...])
    y = h + gate * proj
    o_ref[...] = y
    if tail_ref:
        @pl.when(pl.program_id(0) == pl.num_programs(0) - 1)
        def _():
            nt = tail_ref[0].shape[0]
            tail_ref[0][...] = y[y.shape[0] - nt:, :]


def ple_residual(h, g, p, wg, wp, layer, name, n_tail=0):
    n, d = h.shape
    dp = p.shape[2]
    tm = WIDE_ROW_TILE
    assert n % tm == 0 and (n_tail == 0 or (n - n_tail) % tm + n_tail == tm)
    once = pl.Buffered(1)
    out_shape = [jax.ShapeDtypeStruct((n - n_tail, d), F32)]
    out_specs = [pl.BlockSpec((tm, d), lambda i: (i, 0))]
    if n_tail:
        out_shape.append(jax.ShapeDtypeStruct((n_tail, d), F32))
        out_specs.append(pl.BlockSpec((n_tail, d), lambda i: (0, 0)))
    out = pl.pallas_call(
        _ple_kernel,
        out_shape=out_shape,
        grid=(n // tm,),
        in_specs=[
            pl.BlockSpec((tm, d), lambda i: (i, 0)),
            pl.BlockSpec((1, d), lambda i: (0, 0)),
            pl.BlockSpec((None, tm, dp), lambda i: (layer, i, 0)),
            pl.BlockSpec((None, d, d), lambda i: (layer, 0, 0), pipeline_mode=once),
            pl.BlockSpec((None, dp, d), lambda i: (layer, 0, 0), pipeline_mode=once),
        ],
        out_specs=out_specs,
        compiler_params=_cp(("arbitrary",)),
        name=name,
    )(h, g.reshape(1, d), p, wg, wp)
    return out if n_tail else out[0]


Z0_QA, Z0_KA, Z0_VA, Z0_QB, Z0_KB, Z0_VB, Z0_GOUT = 0, 1024, 1280, 1536, 2048, 2560, 3584
Z0_WIDTH = 4608


def _swa_prompt_kernel(sink_ref, q_ref, kvp_ref, kvc_ref, gq_ref, gk_ref, o_ref, ko_ref, vo_ref):
    n = pl.program_id(1)
    blk = q_ref.shape[0]
    grp = H_A // KVH_A
    kw = KVH_A * HD_A
    lanes = 2 * HD_A
    gq = gq_ref[...]
    gk = gk_ref[...]
    kv = jnp.concatenate([kvp_ref[...], kvc_ref[...]], axis=0)
    rows = lax.broadcasted_iota(jnp.int32, (grp * blk, 2 * blk), 0)
    cols = lax.broadcasted_iota(jnp.int32, (grp * blk, 2 * blk), 1)
    diff = (rows & (blk - 1)) - cols + blk
    lo = jnp.where(n == 0, blk, 0)
    mask = (diff >= 0) & (diff <= WINDOW) & (cols >= lo)

    r = lax.broadcasted_iota(jnp.int32, (lanes, lanes), 0)
    c = lax.broadcasted_iota(jnp.int32, (lanes, lanes), 1)
    head_sum = jnp.where(r // HD_A == c // HD_A, 1.0, 0.0).astype(BF16)
    spread = [jnp.where(r == (c & (HD_A - 1)) + half * HD_A, 1.0, 0.0).astype(BF16) for half in range(2)]
    one = jnp.ones((), BF16)
    low_half = lax.broadcasted_iota(jnp.int32, (blk, lanes), 1) < HD_A
    low_kv = lax.broadcasted_iota(jnp.int32, (2 * blk, lanes), 1) < HD_A

    def head_norm(x, g):
        hi, lo_ = _split_bf16(x * x)
        ss = _dot(hi, head_sum) + _dot(lo_, head_sum)
        return (x * lax.rsqrt(ss * (1.0 / HD_A) + EPS)) * g

    for pair in range(KVH_A // 2):
        ps = slice(pair * lanes, (pair + 1) * lanes)
        kn_pair = head_norm(kv[:, ps], gk)
        ko_ref[0, :, ps] = kn_pair[blk:]
        kn_b = kn_pair.astype(BF16)
        v_b = kv[:, kw + pair * lanes: kw + (pair + 1) * lanes].astype(BF16)
        for half in range(2):
            kh = 2 * pair + half
            kd = _dot(kn_b, spread[half]).astype(BF16)
            vd = _dot(v_b, spread[half]).astype(BF16)
            s_low, s_high = [], []
            for j in range(grp // 2):
                qs = slice(kh * grp * HD_A + j * lanes, kh * grp * HD_A + (j + 1) * lanes)
                qn = head_norm(q_ref[:, qs], gq)
                s_low.append(_dot_nt(jnp.where(low_half, qn, 0.0).astype(BF16), kd))
                s_high.append(_dot_nt(jnp.where(low_half, 0.0, qn).astype(BF16), kd))
            s = jnp.concatenate(s_low + s_high, axis=0) * (HD_A ** -0.5)
            s = jnp.where(mask, s, -jnp.inf)
            order = [2 * j for j in range(grp // 2)] + [2 * j + 1 for j in range(grp // 2)]
            sink = jnp.concatenate([jnp.full((blk, 1), sink_ref[kh * grp + g], F32) for g in order], axis=0)
            m = jnp.maximum(jnp.max(s, axis=-1, keepdims=True), sink)
            e = jnp.exp(s - m).astype(BF16)
            e_sink = jnp.exp(sink - m)
            nh = (grp // 2) * blk
            o_low = _dot(e[:nh], jnp.where(low_kv, vd, one))
            o_high = _dot(e[nh:], jnp.where(low_kv, one, vd))
            o_low = o_low / (pltpu.roll(o_low, HD_A, 1) + e_sink[:nh])
            o_high = o_high / (pltpu.roll(o_high, HD_A, 1) + e_sink[nh:])
            for j in range(grp // 2):
                o_pair = jnp.where(low_half, o_low[j * blk:(j + 1) * blk], o_high[j * blk:(j + 1) * blk])
                o_ref[:, kh * grp * HD_A + j * lanes: kh * grp * HD_A + (j + 1) * lanes] = o_pair.astype(BF16)
    vo_ref[0] = kvc_ref[:, kw:]


def swa_prompt(z0, gq, gk, sinks, bsz, t):
    blk = 128
    nb = t // blk
    kvw = 2 * KVH_A * HD_A
    kv_col = Z0_KA // kvw
    grid_spec = pltpu.PrefetchScalarGridSpec(
        num_scalar_prefetch=1,
        grid=(bsz, nb),
        in_specs=[
            pl.BlockSpec((blk, H_A * HD_A), lambda b, n, s: (b * nb + n, 0)),
            pl.BlockSpec((blk, kvw), lambda b, n, s: (b * nb + jnp.maximum(n - 1, 0), kv_col)),
            pl.BlockSpec((blk, kvw), lambda b, n, s: (b * nb + n, kv_col)),
            pl.BlockSpec((1, 2 * HD_A), lambda b, n, s: (0, 0)),
            pl.BlockSpec((1, 2 * HD_A), lambda b, n, s: (0, 0)),
        ],
        out_specs=[
            pl.BlockSpec((blk, H_A * HD_A), lambda b, n, s: (b * nb + n, 0)),
            pl.BlockSpec((1, blk, KVH_A * HD_A), lambda b, n, s: (b, 0, 0)),
            pl.BlockSpec((1, blk, KVH_A * HD_A), lambda b, n, s: (b, 0, 0)),
        ],
    )
    return pl.pallas_call(
        _swa_prompt_kernel,
        out_shape=[
            jax.ShapeDtypeStruct((bsz * t, H_A * HD_A), BF16),
            jax.ShapeDtypeStruct((bsz, blk, KVH_A * HD_A), F32),
            jax.ShapeDtypeStruct((bsz, blk, KVH_A * HD_A), F32),
        ],
        grid_spec=grid_spec,
        compiler_params=_cp(("parallel", "arbitrary")),
        name="swa_prompt",
    )(sinks, z0, z0, z0, jnp.tile(gq, 2).reshape(1, 2 * HD_A), jnp.tile(gk, 2).reshape(1, 2 * HD_A))


def _swa_decode_kernel(q_ref, k_ref, v_ref, kt_ref, vt_ref, gq_ref, gk_ref, sink_ref, o_ref, kto_ref, vto_ref):
    nb = q_ref.shape[0]
    grp = H_A // KVH_A
    kw = KVH_A * HD_A
    buf_len = kt_ref.shape[2]
    gq = gq_ref[...]
    gk = gk_ref[...]
    sink = sink_ref[...]
    head = lax.broadcasted_iota(jnp.int32, (H_A, kw), 0)
    lane_blk = lax.broadcasted_iota(jnp.int32, (H_A, kw), 1) // HD_A
    own = lane_blk == head // grp
    last = lax.broadcasted_iota(jnp.int32, (kw, buf_len), 1) == buf_len - 1
    kn_rows = jnp.concatenate(
        [_rms_rows(k_ref[:, kh * HD_A:(kh + 1) * HD_A], gk) for kh in range(KVH_A)], axis=-1)
    v_rows = v_ref[...]
    kn_cols = kn_rows.T
    v_cols = v_rows.T
    scale = HD_A ** -0.5
    qn = _rms_rows(q_ref[...].reshape(nb * H_A, HD_A), gq)
    own_all = jnp.concatenate([own] * nb, axis=0)
    qblk = jnp.where(own_all, jnp.concatenate([qn] * KVH_A, axis=-1), 0.0)
    qblk_b = qblk.astype(BF16)
    rows = [slice(i * H_A, (i + 1) * H_A) for i in range(nb)]
    s = jnp.concatenate([_dot(qblk_b[rows[i]], kt_ref[i].astype(BF16)) for i in range(nb)], axis=0) * scale
    k_new = jnp.concatenate([jnp.broadcast_to(kn_rows[i:i + 1], (H_A, kw)) for i in range(nb)], axis=0)
    v_new = jnp.concatenate([jnp.broadcast_to(v_rows[i:i + 1], (H_A, kw)) for i in range(nb)], axis=0)
    s_new = jnp.sum(qblk * k_new, axis=-1, keepdims=True) * scale
    sink_all = jnp.concatenate([sink] * nb, axis=0)
    m = jnp.maximum(jnp.maximum(jnp.max(s, axis=-1, keepdims=True), s_new), sink_all)
    e = jnp.exp(s - m)
    e_new = jnp.exp(s_new - m)
    den = jnp.sum(e, axis=-1, keepdims=True) + e_new + jnp.exp(sink_all - m)
    e_b = e.astype(BF16)
    o_all = jnp.concatenate([_dot_nt(e_b[rows[i]], vt_ref[i].astype(BF16)) for i in range(nb)], axis=0)
    o_all = jnp.where(own_all, o_all + e_new * v_new, 0.0)
    o = o_all[:, 0:HD_A]
    for kh in range(1, KVH_A):
        o = o + o_all[:, kh * HD_A:(kh + 1) * HD_A]
    o_ref[...] = (o / den).astype(BF16).reshape(nb, H_A, HD_A)
    for i in range(nb):
        kto_ref[i] = jnp.where(last, kn_cols[:, i:i + 1], pltpu.roll(kt_ref[i], buf_len - 1, 1))
        vto_ref[i] = jnp.where(last, v_cols[:, i:i + 1], pltpu.roll(vt_ref[i], buf_len - 1, 1))


def swa_decode(q, k, v, kt, vt, gq, gk, sinks):
    bsz, kw, buf_len = kt.shape
    nb = 8
    rows = pl.BlockSpec((nb, kw), lambda i: (i, 0))
    heads = pl.BlockSpec((nb, H_A, HD_A), lambda i: (i, 0, 0))
    buf = pl.BlockSpec((nb, kw, buf_len), lambda i: (i, 0, 0))
    vec = pl.BlockSpec((1, HD_A), lambda i: (0, 0))
    return pl.pallas_call(
        _swa_decode_kernel,
        out_shape=[
            jax.ShapeDtypeStruct((bsz, H_A, HD_A), BF16),
            jax.ShapeDtypeStruct(kt.shape, F32),
            jax.ShapeDtypeStruct(vt.shape, F32),
        ],
        grid=(bsz // nb,),
        in_specs=[heads, rows, rows, buf, buf, vec, vec, pl.BlockSpec((H_A, 1), lambda i: (0, 0))],
        out_specs=[heads, buf, buf],
        compiler_params=_cp(("parallel",)),
        name="swa_decode",
    )(q, k, v, kt, vt, gq.reshape(1, HD_A), gk.reshape(1, HD_A), sinks.reshape(H_A, 1))


def _gla_logdec(glr, wg2_ref, bg2_ref):
    gl = _dot(glr.astype(BF16), wg2_ref[...]) + bg2_ref[...]
    return _log_sigmoid(gl) * (1.0 / GLA_GATE_NORM)


def _col(row8, i):
    return row8.T[:, i:i + 1]


def _gla_out(o, g_go, gout):
    return (_rms_rows(o, g_go) * (gout * jax.nn.sigmoid(gout))).astype(BF16)


def _gla_prompt_kernel(q_ref, k_ref, v01_ref, v23_ref, gout01_ref, gout23_ref, glr_ref, wg2_ref, bg2_ref, ggo_ref,
                       o_ref, so_ref, s_ref):
    c_idx = pl.program_id(1)
    tb = q_ref.shape[0]
    ck = GLA_CHUNK

    @pl.when(c_idx == 0)
    def _():
        s_ref[...] = jnp.zeros_like(s_ref)

    nck = tb // ck
    ld = _gla_logdec(glr_ref[...], wg2_ref, bg2_ref)
    ti = lax.broadcasted_iota(jnp.int32, (tb, tb), 0)
    si = lax.broadcasted_iota(jnp.int32, (tb, tb), 1)
    causal = (si <= ti) & (si // ck == ti // ck)
    tri = jnp.where(causal, 1.0, 0.0).astype(BF16)
    hi, lo = _split_bf16(ld)
    b = _dot(tri, hi) + _dot(tri, lo)
    last_rows = [b[(c + 1) * ck - 1:(c + 1) * ck, :] for c in range(nck)]
    b_last = jnp.concatenate([jnp.broadcast_to(r, (ck, r.shape[1])) for r in last_rows], axis=0)
    k = k_ref[...]
    qt = ((q_ref[...] * (DK_B ** -0.5)) * jnp.exp(b)).astype(BF16)
    kt = (k * jnp.exp(-b)).astype(BF16)
    kd = (k * jnp.exp(b_last - b)).astype(BF16)
    g_go = ggo_ref[...]
    for h in range(H_B):
        ks = slice(h * DK_B, (h + 1) * DK_B)
        vs = slice(h * DV_B, (h + 1) * DV_B)
        hs = slice((h % 2) * DV_B, (h % 2 + 1) * DV_B)
        v = (v01_ref, v23_ref)[h // 2][:, hs].astype(BF16)
        att = jnp.where(causal, _dot_nt(qt[:, ks], kt[:, ks]), 0.0)
        o = _dot(att.astype(BF16), v)
        state = s_ref[h]
        o_state = []
        for c in range(nck):
            rs = slice(c * ck, (c + 1) * ck)
            o_state.append(_dot(qt[rs, ks], state.astype(BF16)))
            d_state = _dot_tn(kd[rs, ks], v[rs])
            decay = _col(jnp.broadcast_to(jnp.exp(last_rows[c][:, ks]), (8, DK_B)), 0)
            state = decay * state + d_state
        s_ref[h] = state
        o = o + jnp.concatenate(o_state, axis=0)
        o_ref[:, vs] = _gla_out(o, g_go, (gout01_ref, gout23_ref)[h // 2][:, hs])
    so_ref[0] = s_ref[...]


def gla_prompt(z0, glr, wg2, bg2, g_go, bsz, t):
    tb = 256
    nt = t // tb
    qk_w = H_B * DK_B
    v_w = H_B * DV_B
    return pl.pallas_call(
        _gla_prompt_kernel,
        out_shape=[
            jax.ShapeDtypeStruct((bsz * t, v_w), BF16),
            jax.ShapeDtypeStruct((bsz, H_B, DK_B, DV_B), F32),
        ],
        grid=(bsz, nt),
        in_specs=[
            pl.BlockSpec((tb, qk_w), lambda b, c: (b * nt + c, Z0_QB // qk_w)),
            pl.BlockSpec((tb, qk_w), lambda b, c: (b * nt + c, Z0_KB // qk_w)),
            pl.BlockSpec((tb, v_w // 2), lambda b, c: (b * nt + c, Z0_VB // (v_w // 2))),
            pl.BlockSpec((tb, v_w // 2), lambda b, c: (b * nt + c, Z0_VB // (v_w // 2) + 1)),
            pl.BlockSpec((tb, v_w // 2), lambda b, c: (b * nt + c, Z0_GOUT // (v_w // 2))),
            pl.BlockSpec((tb, v_w // 2), lambda b, c: (b * nt + c, Z0_GOUT // (v_w // 2) + 1)),
            pl.BlockSpec((tb, 128), lambda b, c: (b * nt + c, 0)),
            pl.BlockSpec((128, qk_w), lambda b, c: (0, 0)),
            pl.BlockSpec((1, qk_w), lambda b, c: (0, 0)),
            pl.BlockSpec((1, DV_B), lambda b, c: (0, 0)),
        ],
        out_specs=[
            pl.BlockSpec((tb, v_w), lambda b, c: (b * nt + c, 0)),
            pl.BlockSpec((1, H_B, DK_B, DV_B), lambda b, c: (b, 0, 0, 0)),
        ],
        scratch_shapes=[pltpu.VMEM((H_B, DK_B, DV_B), F32)],
        compiler_params=_cp(("parallel", "arbitrary")),
        name="gla_prompt",
    )(z0, z0, z0, z0, z0, z0, glr, wg2, bg2.reshape(1, qk_w), g_go.reshape(1, DV_B))


def _gla_decode_kernel(q_ref, k_ref, v_ref, gout_ref, glr_ref, wg2_ref, bg2_ref, ggo_ref, s_ref, o_ref, so_ref):
    nb = q_ref.shape[0]
    ld = _gla_logdec(glr_ref[...], wg2_ref, bg2_ref)
    g_go = ggo_ref[...]
    for h in range(H_B):
        ks = slice(h * DK_B, (h + 1) * DK_B)
        vs = slice(h * DV_B, (h + 1) * DV_B)
        eg_t = jnp.exp(ld[:, ks]).T
        k_t = k_ref[:, ks].T
        q_b = (q_ref[:, ks] * (DK_B ** -0.5)).astype(BF16)
        outs = []
        for i in range(nb):
            state = eg_t[:, i:i + 1] * s_ref[i, h] + k_t[:, i:i + 1] * v_ref[i:i + 1, vs]
            so_ref[i, h] = state
            outs.append(_dot(q_b, state.astype(BF16))[i:i + 1])
        o = jnp.concatenate(outs, axis=0)
        o_ref[:, vs] = _gla_out(o, g_go, gout_ref[:, vs])


def gla_decode(q, k, v, gout, glr, wg2, bg2, g_go, state):
    bsz = q.shape[0]
    nb = 8
    qk_w = H_B * DK_B
    v_w = H_B * DV_B
    rows = lambda w: pl.BlockSpec((nb, w), lambda i: (i, 0))
    st = pl.BlockSpec((nb, H_B, DK_B, DV_B), lambda i: (i, 0, 0, 0))
    return pl.pallas_call(
        _gla_decode_kernel,
        out_shape=[
            jax.ShapeDtypeStruct((bsz, v_w), BF16),
            jax.ShapeDtypeStruct(state.shape, F32),
        ],
        grid=(bsz // nb,),
        in_specs=[
            rows(qk_w), rows(qk_w), rows(v_w), rows(v_w), rows(128),
            pl.BlockSpec((128, qk_w), lambda i: (0, 0)),
            pl.BlockSpec((1, qk_w), lambda i: (0, 0)),
            pl.BlockSpec((1, DV_B), lambda i: (0, 0)),
            st,
        ],
        out_specs=[rows(v_w), st],
        compiler_params=_cp(("parallel",)),
        name="gla_decode",
    )(q, k, v, gout, glr, wg2, bg2.reshape(1, qk_w), g_go.reshape(1, DV_B), state)


Z1_CX, Z1_CY, Z1_QD, Z1_KD, Z1_VD = 0, 1024, 2048, 3072, 3584


def _rg_gates(xc, wa_ref, ba_ref, wx_ref, bx_ref, lam_ref):
    xb = xc.astype(BF16)
    ra = jnp.concatenate(
        [_dot(xb[:, n * C_BLOCK:(n + 1) * C_BLOCK], wa_ref[n]) for n in range(C_BLOCKS)], axis=-1) + ba_ref[...]
    rx = jnp.concatenate(
        [_dot(xb[:, n * C_BLOCK:(n + 1) * C_BLOCK], wx_ref[n]) for n in range(C_BLOCKS)], axis=-1) + bx_ref[...]
    r_gate = jax.nn.sigmoid(ra)
    i_gate = jax.nn.sigmoid(rx)
    log_a = (-RG_C * r_gate) * _softplus(-lam_ref[...])
    a = jnp.exp(log_a)
    one_minus_a2 = -jnp.tanh(log_a) * (a * a + 1.0)
    u = jnp.sqrt(one_minus_a2) * (i_gate * xc)
    return a, u


def _rglru_prompt_kernel(cx_ref, cy_ref, cw_ref, cb_ref, wa_ref, ba_ref, wx_ref, bx_ref, lam_ref,
                         o_ref, tail_ref, hl_ref, prev_ref, hc_ref):
    t_idx = pl.program_id(1)
    tt = cx_ref.shape[0]

    @pl.when(t_idx == 0)
    def _():
        prev_ref[...] = jnp.zeros_like(prev_ref)
        hc_ref[...] = jnp.zeros_like(hc_ref)

    x = cx_ref[...]
    xp = jnp.concatenate([prev_ref[...], x], axis=0)
    xc = cb_ref[...] + cw_ref[CONV_W - 1:CONV_W, :] * x
    for j in range(CONV_W - 1):
        d = CONV_W - 1 - j
        xc = xc + cw_ref[j:j + 1, :] * pltpu.roll(xp, d, 0)[8:]
    prev_ref[...] = x[tt - 8:]
    tail_ref[0] = x[tt - 8:]

    a, u = _rg_gates(xc, wa_ref, ba_ref, wx_ref, bx_ref, lam_ref)
    sub = lax.broadcasted_iota(jnp.int32, a.shape, 0) & 7
    for d in (1, 2, 4):
        inside = sub >= d
        u = a * jnp.where(inside, pltpu.roll(u, d, 0), 0.0) + u
        a = a * jnp.where(inside, pltpu.roll(a, d, 0), 1.0)
    h_prev = hc_ref[...]
    tiles = []
    for r in range(0, tt, 8):
        tiles.append(a[r:r + 8] * h_prev + u[r:r + 8])
        h_prev = tiles[-1][7:8]
    h = jnp.concatenate(tiles, axis=0)
    hc_ref[...] = h[tt - 1:]
    hl_ref[0] = h[tt - 1:]
    o_ref[...] = (h * jax.nn.gelu(cy_ref[...])).astype(BF16)


def rglru_prompt(z1, conv_w, conv_b, wa, ba, wx, bx, lam, bsz, t):
    tt = 256
    nt = t // tt
    c = C_WIDTH
    vec = pl.BlockSpec((1, c), lambda b, i: (0, 0))
    wblk = pl.BlockSpec((C_BLOCKS, C_BLOCK, C_BLOCK), lambda b, i: (0, 0, 0))
    return pl.pallas_call(
        _rglru_prompt_kernel,
        out_shape=[
            jax.ShapeDtypeStruct((bsz * t, c), BF16),
            jax.ShapeDtypeStruct((bsz, 8, c), F32),
            jax.ShapeDtypeStruct((bsz, 1, c), F32),
        ],
        grid=(bsz, nt),
        in_specs=[
            pl.BlockSpec((tt, c), lambda b, i: (b * nt + i, Z1_CX // c)),
            pl.BlockSpec((tt, c), lambda b, i: (b * nt + i, Z1_CY // c)),
            pl.BlockSpec((CONV_W, c), lambda b, i: (0, 0)),
            vec, wblk, vec, wblk, vec, vec,
        ],
        out_specs=[
            pl.BlockSpec((tt, c), lambda b, i: (b * nt + i, 0)),
            pl.BlockSpec((1, 8, c), lambda b, i: (b, 0, 0)),
            pl.BlockSpec((1, 1, c), lambda b, i: (b, 0, 0)),
        ],
        scratch_shapes=[pltpu.VMEM((8, c), F32), pltpu.VMEM((1, c), F32)],
        compiler_params=_cp(("parallel", "arbitrary")),
        name="rglru_prompt",
    )(z1, z1, conv_w, conv_b.reshape(1, c), wa, ba.reshape(1, c), wx, bx.reshape(1, c), lam.reshape(1, c))


def _rglru_decode_kernel(cx_ref, cy_ref, buf_ref, h0_ref, cw_ref, cb_ref, wa_ref, ba_ref, wx_ref, bx_ref, lam_ref,
                         o_ref, nbuf_ref, hl_ref):
    x = cx_ref[...]
    xc = cb_ref[...] + cw_ref[CONV_W - 1:CONV_W, :] * x
    for j in range(CONV_W - 1):
        xc = xc + cw_ref[j:j + 1, :] * buf_ref[j]
    for j in range(CONV_W - 2):
        nbuf_ref[j] = buf_ref[j + 1]
    nbuf_ref[CONV_W - 2] = x
    a, u = _rg_gates(xc, wa_ref, ba_ref, wx_ref, bx_ref, lam_ref)
    h = a * h0_ref[...] + u
    hl_ref[...] = h
    o_ref[...] = (h * jax.nn.gelu(cy_ref[...])).astype(BF16)


def rglru_decode(cx, cy, buf, h0, conv_w, conv_b, wa, ba, wx, bx, lam):
    bsz, c = cx.shape
    return pl.pallas_call(
        _rglru_decode_kernel,
        out_shape=[
            jax.ShapeDtypeStruct((bsz, c), BF16),
            jax.ShapeDtypeStruct(buf.shape, F32),
            jax.ShapeDtypeStruct((bsz, c), F32),
        ],
        compiler_params=pltpu.CompilerParams(vmem_limit_bytes=V7X_VMEM_LIMIT_BYTES),
        name="rglru_decode",
    )(cx, cy, buf, h0, conv_w, conv_b.reshape(1, c), wa, ba.reshape(1, c), wx, bx.reshape(1, c), lam.reshape(1, c))


def _strict_upper_ones(n):
    j = lax.broadcasted_iota(jnp.int32, (n, n), 0)
    s = lax.broadcasted_iota(jnp.int32, (n, n), 1)
    return jnp.where(j > s, 1.0, 0.0).astype(BF16)


def _from_here_ones(n):
    j = lax.broadcasted_iota(jnp.int32, (n, n), 0)
    s = lax.broadcasted_iota(jnp.int32, (n, n), 1)
    return jnp.where(j >= s, 1.0, 0.0).astype(BF16)


def _sb_tile(z, from_here, dead, mask):
    sp = _softplus(z)
    if mask is not None:
        sp = jnp.where(mask, sp, 0.0)
    hi, lo = _split_bf16(sp)
    cum = _dot(hi, from_here) + _dot(lo, from_here)
    w = jnp.exp(z - cum - dead)
    if mask is not None:
        w = jnp.where(mask, w, 0.0)
    return w, dead + cum[:, 0:1]


def _sb_prompt_kernel(bias_ref, q_ref, k_ref, v_ref, gq_ref, gk_ref, o_ref, ko_ref, kb_ref, vb_ref):
    kh = pl.program_id(1)
    i = pl.program_id(2)
    blk = q_ref.shape[0]
    grp = H_D // KVH_D

    @pl.when(i == 0)
    def _():
        kn = _rms_rows(k_ref[...], gk_ref[...])
        ko_ref[...] = kn
        kb_ref[...] = kn.astype(BF16)
        vb_ref[...] = v_ref[...].astype(BF16)

    gq = gq_ref[...]
    qs = jnp.concatenate(
        [_rms_rows(q_ref[:, g * HD_D:(g + 1) * HD_D], gq) for g in range(grp)], axis=0).astype(BF16)
    bias = jnp.concatenate(
        [jnp.full((blk, 1), bias_ref[kh * grp + g], F32) for g in range(grp)], axis=0)
    from_here = _from_here_ones(blk)
    scale = HD_D ** -0.5

    def tile(j, mask, acc, dead):
        off = pl.multiple_of(j * blk, blk)
        z = _dot_nt(qs, kb_ref[pl.ds(off, blk), :]) * scale + bias
        w, dead = _sb_tile(z, from_here, dead, mask)
        return acc + _dot(w.astype(BF16), vb_ref[pl.ds(off, blk), :]), dead

    rows = lax.broadcasted_iota(jnp.int32, (grp * blk, blk), 0) & (blk - 1)
    cols = lax.broadcasted_iota(jnp.int32, (grp * blk, blk), 1)
    acc, dead = tile(i, cols < rows, jnp.zeros((grp * blk, HD_D), F32), jnp.zeros((grp * blk, 1), F32))

    def pair(step, carry):
        j = i - 1 - 2 * step
        return tile(j - 1, None, *tile(j, None, *carry))

    def last(step, carry):
        return tile(0, None, *carry)

    n_pairs = i // 2
    carry = lax.fori_loop(0, n_pairs, pair, (acc, dead))
    acc, _ = lax.fori_loop(0, i - 2 * n_pairs, last, carry)
    for g in range(grp):
        o_ref[:, g * HD_D:(g + 1) * HD_D] = acc[g * blk:(g + 1) * blk].astype(BF16)


def sb_prompt(z1, gq, gk, bias, bsz, t):
    blk = SB_PROMPT_TILE
    nb = t // blk
    grp = H_D // KVH_D
    qw = grp * HD_D
    grid_spec = pltpu.PrefetchScalarGridSpec(
        num_scalar_prefetch=1,
        grid=(bsz, KVH_D, nb),
        in_specs=[
            pl.BlockSpec((blk, qw), lambda b, k, i, s: (b * nb + i, Z1_QD // qw + k)),
            pl.BlockSpec((t, HD_D), lambda b, k, i, s: (b, Z1_KD // HD_D + k)),
            pl.BlockSpec((t, HD_D), lambda b, k, i, s: (b, Z1_VD // HD_D + k)),
            pl.BlockSpec((1, HD_D), lambda b, k, i, s: (0, 0)),
            pl.BlockSpec((1, HD_D), lambda b, k, i, s: (0, 0)),
        ],
        out_specs=[
            pl.BlockSpec((blk, qw), lambda b, k, i, s: (b * nb + i, k)),
            pl.BlockSpec((t, HD_D), lambda b, k, i, s: (b, k)),
        ],
        scratch_shapes=[pltpu.VMEM((t, HD_D), BF16), pltpu.VMEM((t, HD_D), BF16)],
    )
    return pl.pallas_call(
        _sb_prompt_kernel,
        out_shape=[
            jax.ShapeDtypeStruct((bsz * t, H_D * HD_D), BF16),
            jax.ShapeDtypeStruct((bsz * t, KVH_D * HD_D), F32),
        ],
        grid_spec=grid_spec,
        compiler_params=_cp(("parallel", "parallel", "arbitrary")),
        name="sb_prompt",
    )(bias, z1, z1, z1, gq.reshape(1, HD_D), gk.reshape(1, HD_D))


def _sb_decode_kernel(n_pages, pt_ref, bias_ref, q_ref, k_ref, v_ref, gq_ref, gk_ref, ck_hbm, cv_hbm,
                      o_ref, ko_ref, kbuf, vbuf, sem):
    b = pl.program_id(0)
    nb = pl.num_programs(0)
    npg = SB_PAGES_PER_STEP
    n_chunks = n_pages // npg
    prow = PAGE * KVH_D
    grp = H_D // KVH_D
    scale = HD_D ** -0.5

    def page_copies(bb, chunk, slot):
        cps = []
        for p in range(npg):
            page = pt_ref[bb * n_pages + chunk * npg + p]
            dst = pl.ds(p * prow, prow)
            cps.append(pltpu.make_async_copy(ck_hbm.at[page], kbuf.at[slot, dst], sem.at[0, slot]))
            cps.append(pltpu.make_async_copy(cv_hbm.at[page], vbuf.at[slot, dst], sem.at[1, slot]))
        return cps

    @pl.when(b == 0)
    def _():
        for cp in page_copies(0, n_chunks - 1, 0):
            cp.start()

    bias = jnp.concatenate([jnp.full((1, 1), bias_ref[h], F32) for h in range(H_D)], axis=0)
    qn = _rms_rows(q_ref[0], gq_ref[...])
    qb = qn.astype(BF16)
    kn = _rms_rows(k_ref[0], gk_ref[...])
    ko_ref[0] = kn
    kn_sel = jnp.concatenate([kn[h // grp:h // grp + 1] for h in range(H_D)], axis=0)
    v_sel = jnp.concatenate([v_ref[0, h // grp:h // grp + 1] for h in range(H_D)], axis=0)
    z0 = jnp.sum(qn * kn_sel, axis=-1, keepdims=True) * scale + bias
    visible = jnp.zeros((H_D, 1), jnp.int32) < jnp.zeros((H_D, 1), jnp.int32)
    sp0 = _softplus(z0)
    acc = jnp.where(visible, jnp.exp(z0 - sp0), 0.0) * v_sel
    surv = jnp.where(visible, -sp0, 0.0)

    later = _strict_upper_ones(prow)
    row = lax.broadcasted_iota(jnp.int32, (npg * H_D, prow), 0)
    col = lax.broadcasted_iota(jnp.int32, (npg * H_D, prow), 1)
    valid = (col & (KVH_D - 1)) == ((row & (H_D - 1)) // grp)
    bias_r = jnp.concatenate([bias] * npg, axis=0)

    for i in range(n_chunks):
        slot = i % 2
        if i + 1 < n_chunks:
            for cp in page_copies(b, n_chunks - 2 - i, 1 - slot):
                cp.start()
        else:
            @pl.when(b + 1 < nb)
            def _():
                for cp in page_copies(b + 1, n_chunks - 1, 1 - slot):
                    cp.start()
        for cp in page_copies(b, n_chunks - 1 - i, slot):
            cp.wait()

        z = _dot_nt(qb, kbuf[slot].astype(BF16))
        z = jnp.concatenate([z[:, p * prow:(p + 1) * prow] for p in range(npg)], axis=0) * scale + bias_r
        sp = _softplus(z)
        l1m = jnp.where(valid, -sp, 0.0)
        hi, lo = _split_bf16(l1m)
        suffix = _dot(hi, later) + _dot(lo, later)
        tot = jnp.sum(l1m, axis=-1, keepdims=True)
        survs = [None] * npg
        for p in reversed(range(npg)):
            survs[p] = surv
            surv = surv + tot[p * H_D:(p + 1) * H_D]
        w = jnp.where(valid, jnp.exp((z - sp) + suffix + jnp.concatenate(survs, axis=0)), 0.0)
        w = jnp.concatenate([w[p * H_D:(p + 1) * H_D] for p in range(npg)], axis=1).astype(BF16)
        acc = acc + _dot(w, vbuf[slot].astype(BF16))

    o_ref[0] = acc.astype(BF16)


def sb_decode(q, k, v, gq, gk, bias, cache_k, cache_v, page_table):
    bsz = q.shape[0]
    n_pages = page_table.shape[1]
    npg = SB_PAGES_PER_STEP
    assert n_pages % (2 * npg) == 0
    prow = PAGE * KVH_D
    grid_spec = pltpu.PrefetchScalarGridSpec(
        num_scalar_prefetch=2,
        grid=(bsz,),
        in_specs=[
            pl.BlockSpec((1, H_D, HD_D), lambda b, pt, bias_: (b, 0, 0)),
            pl.BlockSpec((1, KVH_D, HD_D), lambda b, pt, bias_: (b, 0, 0)),
            pl.BlockSpec((1, KVH_D, HD_D), lambda b, pt, bias_: (b, 0, 0)),
            pl.BlockSpec((1, HD_D), lambda b, pt, bias_: (0, 0)),
            pl.BlockSpec((1, HD_D), lambda b, pt, bias_: (0, 0)),
            pl.BlockSpec(memory_space=pl.ANY),
            pl.BlockSpec(memory_space=pl.ANY),
        ],
        out_specs=[
            pl.BlockSpec((1, H_D, HD_D), lambda b, pt, bias_: (b, 0, 0)),
            pl.BlockSpec((1, KVH_D, HD_D), lambda b, pt, bias_: (b, 0, 0)),
        ],
        scratch_shapes=[
            pltpu.VMEM((2, npg * prow, HD_D), F32),
            pltpu.VMEM((2, npg * prow, HD_D), F32),
            pltpu.SemaphoreType.DMA((2, 2)),
        ],
    )
    return pl.pallas_call(
        functools.partial(_sb_decode_kernel, n_pages),
        out_shape=[
            jax.ShapeDtypeStruct((bsz, H_D, HD_D), BF16),
            jax.ShapeDtypeStruct((bsz, KVH_D, HD_D), F32),
        ],
        grid_spec=grid_spec,
        compiler_params=_cp(("arbitrary",)),
        name="sb_decode",
    )(page_table.reshape(-1), bias, q, k, v, gq.reshape(1, HD_D), gk.reshape(1, HD_D), cache_k, cache_v)


def kernel(x_prompt, x_sample, state_swa_k, state_swa_v, state_gla, state_conv, state_lru, cache_sb_k, cache_sb_v, page_table, p_prompt, p_sample, g_mix, g_ffn, w_ffn_gate, w_ffn_up, w_ffn_down, g_ple, w_ple_gate, w_ple_proj, w_in_even, g_qnorm_a, g_knorm_a, sinks_a, w_gla_gate2, b_gla_gate2, g_gla_out, w_out_even, w_in_odd, conv_w, conv_b, w_rg_a, b_rg_a, w_rg_x, b_rg_x, lru_lambda, g_qnorm_d, g_knorm_d, sb_bias, w_out_odd):
    bsz, t, d = x_prompt.shape
    dbs = x_sample.shape[0]
    n_p = bsz * t
    depth = g_mix.shape[0]
    ple = p_prompt.shape[-1]

    x_p = x_prompt.reshape(n_p, d)
    x_s = x_sample.reshape(dbs, d)
    p_all = jnp.concatenate([p_prompt.reshape(depth, n_p, ple), p_sample.reshape(depth, dbs, ple)], axis=1)

    ffn_w = (w_ffn_gate.astype(BF16), w_ffn_up.astype(BF16), w_ffn_down.astype(BF16))
    ple_w = (w_ple_gate.astype(BF16), w_ple_proj.astype(BF16))

    def dense_tail(h, i, n_tail=0):
        h = ffn_residual(h, g_ffn[i], *ffn_w, i, 512, f"ffn_{i}")
        return ple_residual(h, g_ple[i], p_all, *ple_w, i, f"ple_{i}", n_tail)

    w = w_in_even[0]
    glr_col = Z0_GOUT
    w0 = jnp.concatenate([w[:, :glr_col], w[:, glr_col + GLA_RANK:]], axis=1).astype(BF16)
    w0_glr = jnp.pad(w[:, glr_col:glr_col + GLA_RANK], ((0, 0), (0, 128 - GLA_RANK))).astype(BF16)
    z0, glr = norm_matmul(x_p, g_mix[0], w0, 1536, "in_proj_even", w_side=w0_glr, x_tail=x_s)
    wg2 = jnp.pad(w_gla_gate2[0], ((0, 128 - GLA_RANK), (0, 0))).astype(BF16)

    oa_p, swa_k_p, swa_v_p = swa_prompt(z0, g_qnorm_a[0], g_knorm_a[0], sinks_a[0], bsz, t)
    ob_p, gla_p = gla_prompt(z0, glr, wg2, b_gla_gate2[0], g_gla_out[0], bsz, t)

    zd = z0[n_p:]
    kw_a = KVH_A * HD_A
    to_keys_last = lambda s: jnp.transpose(s, (0, 2, 3, 1)).reshape(dbs, kw_a, -1)
    from_keys_last = lambda s: jnp.transpose(s.reshape(dbs, KVH_A, HD_A, -1), (0, 3, 1, 2))[None]
    oa_d, swa_kt_d, swa_vt_d = swa_decode(
        zd[:, Z0_QA:Z0_QA + 1024].reshape(dbs, H_A, HD_A),
        zd[:, Z0_KA:Z0_KA + kw_a], zd[:, Z0_VA:Z0_VA + kw_a],
        to_keys_last(state_swa_k[0]), to_keys_last(state_swa_v[0]),
        g_qnorm_a[0], g_knorm_a[0], sinks_a[0])
    ob_d, gla_d = gla_decode(
        zd[:, Z0_QB:Z0_QB + 512], zd[:, Z0_KB:Z0_KB + 512], zd[:, Z0_VB:Z0_VB + 1024],
        zd[:, Z0_GOUT:Z0_GOUT + 1024], glr[n_p:], wg2, b_gla_gate2[0], g_gla_out[0], state_gla[0])

    wo = w_out_even[0].astype(BF16)
    h = proj_residual(x_p, oa_p, ob_p, oa_d.reshape(dbs, 1024), ob_d, wo[:1024], wo[1024:], "out_proj_even",
                      h_tail=x_s)
    h = dense_tail(h, 0)

    z1 = norm_matmul(h, g_mix[1], w_in_odd[0].astype(BF16), 1024, "in_proj_odd")
    wa = w_rg_a[0].astype(BF16)
    wx = w_rg_x[0].astype(BF16)
    oc_p, conv_tail, lru_p = rglru_prompt(z1, conv_w[0], conv_b[0], wa, b_rg_a[0], wx, b_rg_x[0], lru_lambda[0], bsz, t)
    od_p, sb_k_p = sb_prompt(z1, g_qnorm_d[0], g_knorm_d[0], sb_bias[0], bsz, t)

    zd = z1[n_p:]
    oc_d, conv_d, lru_d = rglru_decode(
        zd[:, Z1_CX:Z1_CX + 1024], zd[:, Z1_CY:Z1_CY + 1024],
        jnp.swapaxes(state_conv[0], 0, 1), state_lru[0],
        conv_w[0], conv_b[0], wa, b_rg_a[0], wx, b_rg_x[0], lru_lambda[0])
    kvw = KVH_D * HD_D
    n_phys = cache_sb_k.shape[1]
    od_d, sb_k_d = sb_decode(
        zd[:, Z1_QD:Z1_QD + 1024].reshape(dbs, H_D, HD_D),
        zd[:, Z1_KD:Z1_KD + kvw].reshape(dbs, KVH_D, HD_D),
        zd[:, Z1_VD:Z1_VD + kvw].reshape(dbs, KVH_D, HD_D),
        g_qnorm_d[0], g_knorm_d[0], sb_bias[0],
        cache_sb_k[0].reshape(n_phys, PAGE * KVH_D, HD_D), cache_sb_v[0].reshape(n_phys, PAGE * KVH_D, HD_D),
        page_table)

    wo = w_out_odd[0].astype(BF16)
    h = proj_residual(h, oc_p, od_p, oc_d, od_d.reshape(dbs, 1024), wo[:1024], wo[1024:], "out_proj_odd")
    y_prompt, y_sample = dense_tail(h, 1, n_tail=dbs)

    n_keep = min(WINDOW, t)
    return (
        y_prompt.reshape(bsz, t, d), y_sample.reshape(dbs, 1, d),
        swa_k_p.reshape(1, bsz, n_keep, KVH_A, HD_A), swa_v_p.reshape(1, bsz, n_keep, KVH_A, HD_A),
        gla_p[None],
        conv_tail[:, 8 - (CONV_W - 1):][None], lru_p.reshape(1, bsz, C_WIDTH),
        sb_k_p.reshape(1, bsz, t, KVH_D, HD_D), z1[:n_p, Z1_VD:].reshape(1, bsz, t, KVH_D, HD_D),
        from_keys_last(swa_kt_d), from_keys_last(swa_vt_d),
        gla_d[None],
        jnp.swapaxes(conv_d, 0, 1)[None], lru_d[None],
        sb_k_d.reshape(1, dbs, 1, KVH_D, HD_D), zd[:, Z1_VD:].reshape(1, dbs, 1, KVH_D, HD_D),
    )
```

```python
import functools

import jax
import jax.numpy as jnp
from jax import lax
from jax.experimental import pallas as pl
from jax.experimental.pallas import tpu as pltpu

F32 = jnp.float32
BF16 = jnp.bfloat16
EPS = 1e-6

V7X_VMEM_LIMIT_BYTES = 56 * 1024 * 1024

H_A, KVH_A, HD_A, WINDOW = 16, 4, 64, 128
H_B, DK_B, DV_B, GLA_RANK, GLA_GATE_NORM, GLA_CHUNK = 4, 128, 256, 16, 16.0, 64
C_WIDTH, C_BLOCKS, CONV_W, RG_C = 1024, 8, 4, 8.0
C_BLOCK = C_WIDTH // C_BLOCKS
H_D, KVH_D, HD_D = 8, 4, 128
PAGE = 128

ROW_TILE = 640
WIDE_ROW_TILE = 1040
SB_PROMPT_TILE = 256
SB_PAGES_PER_STEP = 16


def _cp(sem):
    return pltpu.CompilerParams(dimension_semantics=sem, vmem_limit_bytes=V7X_VMEM_LIMIT_BYTES)


def _rms_rows(x, g):
    r = lax.rsqrt(jnp.mean(x * x, axis=-1, keepdims=True) + EPS)
    return (x * r) * g


def _softplus(x):
    return jnp.maximum(x, 0.0) + jnp.log(1.0 + jnp.exp(-jnp.abs(x)))


def _log_sigmoid(x):
    return jnp.minimum(x, 0.0) - jnp.log(1.0 + jnp.exp(-jnp.abs(x)))


def _split_bf16(x):
    hi = x.astype(BF16)
    lo = (x - hi.astype(F32)).astype(BF16)
    return hi, lo


def _dot(a, b):
    return jnp.dot(a, b, preferred_element_type=F32)


def _dot_nt(a, b):
    return lax.dot_general(a, b, (((1,), (1,)), ((), ())), preferred_element_type=F32)


def _dot_tn(a, b):
    return lax.dot_general(a, b, (((0,), (0,)), ((), ())), preferred_element_type=F32)


def _norm_matmul_kernel(has_tail, has_side, x_ref, g_ref, w_ref, *rest):
    rest = list(rest)
    xt_ref = rest.pop(0) if has_tail else None
    ws_ref = rest.pop(0) if has_side else None
    o_ref = rest.pop(0)
    os_ref = rest.pop(0) if has_side else None
    xn_ref, = rest

    @pl.when(pl.program_id(1) == 0)
    def _():
        xn_ref[...] = _rms_rows(x_ref[...], g_ref[...]).astype(BF16)
        if has_tail:
            @pl.when(pl.program_id(0) == pl.num_programs(0) - 1)
            def _():
                nt = xt_ref.shape[0]
                xn_ref[xn_ref.shape[0] - nt:, :] = _rms_rows(xt_ref[...], g_ref[...]).astype(BF16)
        if has_side:
            os_ref[...] = _dot(xn_ref[...], ws_ref[...])

    o_ref[...] = _dot(xn_ref[...], w_ref[...])


def norm_matmul(x, g, w, tn, name, w_side=None, x_tail=None):
    d = x.shape[1]
    nt = 0 if x_tail is None else x_tail.shape[0]
    n = x.shape[0] + nt
    nout = w.shape[1]
    tm = WIDE_ROW_TILE
    assert n % tm == 0 and (nt == 0 or x.shape[0] % tm + nt == tm)
    in_specs = [
        pl.BlockSpec((tm, d), lambda i, j: (i, 0)),
        pl.BlockSpec((1, d), lambda i, j: (0, 0)),
        pl.BlockSpec((d, tn), lambda i, j: (0, j)),
    ]
    out_shape = [jax.ShapeDtypeStruct((n, nout), F32)]
    out_specs = [pl.BlockSpec((tm, tn), lambda i, j: (i, j))]
    args = [x, g.reshape(1, d), w]
    if x_tail is not None:
        in_specs.append(pl.BlockSpec((nt, d), lambda i, j: (0, 0)))
        args.append(x_tail)
    if w_side is not None:
        ns = w_side.shape[1]
        in_specs.append(pl.BlockSpec((d, ns), lambda i, j: (0, 0)))
        out_shape.append(jax.ShapeDtypeStruct((n, ns), F32))
        out_specs.append(pl.BlockSpec((tm, ns), lambda i, j: (i, 0)))
        args.append(w_side)
    out = pl.pallas_call(
        functools.partial(_norm_matmul_kernel, x_tail is not None, w_side is not None),
        out_shape=out_shape,
        grid=(n // tm, nout // tn),
        in_specs=in_specs,
        out_specs=out_specs,
        scratch_shapes=[pltpu.VMEM((tm, d), BF16)],
        compiler_params=_cp(("parallel", "arbitrary")),
        name=name,
    )(*args)
    return out if w_side is not None else out[0]


def _proj_res_kernel(h_ref, a_ref, b_ref, at_ref, bt_ref, wa_ref, wb_ref, *rest):
    ht_ref, o_ref = rest if len(rest) == 2 else (None, rest[0])
    tm = h_ref.shape[0]
    nt = at_ref.shape[0]
    o_ref[...] = h_ref[...] + (_dot(a_ref[...], wa_ref[...]) + _dot(b_ref[...], wb_ref[...]))

    @pl.when(pl.program_id(0) == pl.num_programs(0) - 1)
    def _():
        tail = _dot(at_ref[...], wa_ref[...]) + _dot(bt_ref[...], wb_ref[...])
        h_tail = h_ref[tm - nt:, :] if ht_ref is None else ht_ref[...]
        o_ref[tm - nt:, :] = h_tail + tail


def proj_residual(h, a, b, a_tail, b_tail, w, name, h_tail=None):
    d = h.shape[1]
    ka, kb = a.shape[1], b.shape[1]
    nt = a_tail.shape[0]
    n = a.shape[0] + nt
    tm = ROW_TILE
    assert h.shape[0] == (n if h_tail is None else a.shape[0]) and n % tm == 0 and a.shape[0] % tm + nt == tm
    assert ka == kb and w.shape[0] == ka + kb
    in_specs = [
        pl.BlockSpec((tm, d), lambda i: (i, 0)),
        pl.BlockSpec((tm, ka), lambda i: (i, 0)),
        pl.BlockSpec((tm, kb), lambda i: (i, 0)),
        pl.BlockSpec((nt, ka), lambda i: (0, 0)),
        pl.BlockSpec((nt, kb), lambda i: (0, 0)),
        pl.BlockSpec((ka, d), lambda i: (0, 0)),
        pl.BlockSpec((kb, d), lambda i: (1, 0)),
    ]
    args = [h, a, b, a_tail, b_tail, w, w]
    if h_tail is not None:
        in_specs.append(pl.BlockSpec((nt, d), lambda i: (0, 0)))
        args.append(h_tail)
    return pl.pallas_call(
        _proj_res_kernel,
        out_shape=jax.ShapeDtypeStruct((n, d), F32),
        grid=(n // tm,),
        in_specs=in_specs,
        out_specs=pl.BlockSpec((tm, d), lambda i: (i, 0)),
        compiler_params=_cp(("parallel",)),
        name=name,
    )(*args)


def _ffn_kernel(h_ref, g_ref, wg_ref, wu_ref, wd_ref, o_ref, hn_ref):
    @pl.when(pl.program_id(1) == 0)
    def _():
        h = h_ref[...]
        hn_ref[...] = _rms_rows(h, g_ref[...]).astype(BF16)
        o_ref[...] = h

    hn = hn_ref[...]
    gate = _dot(hn, wg_ref[...])
    up = _dot(hn, wu_ref[...])
    act = (gate * jax.nn.sigmoid(gate)) * up
    o_ref[...] += _dot(act.astype(BF16), wd_ref[...])


def ffn_residual(h, g, wg, wu, wd, layer, tf, name):
    n, d = h.shape
    dff = wg.shape[2]
    tm = WIDE_ROW_TILE
    return pl.pallas_call(
        _ffn_kernel,
        out_shape=jax.ShapeDtypeStruct((n, d), F32),
        grid=(n // tm, dff // tf),
        in_specs=[
            pl.BlockSpec((tm, d), lambda i, f: (i, 0)),
            pl.BlockSpec((1, d), lambda i, f: (0, 0)),
            pl.BlockSpec((None, d, tf), lambda i, f: (layer, 0, f)),
            pl.BlockSpec((None, d, tf), lambda i, f: (layer, 0, f)),
            pl.BlockSpec((None, tf, d), lambda i, f: (layer, f, 0)),
        ],
        out_specs=pl.BlockSpec((tm, d), lambda i, f: (i, 0)),
        scratch_shapes=[pltpu.VMEM((tm, d), BF16)],
        compiler_params=_cp(("parallel", "arbitrary")),
        name=name,
    )(h, g.reshape(1, d), wg, wu, wd)


def _ple_kernel(h_ref, g_ref, p_ref, wg_ref, wp_ref, o_ref, *tail_ref):
    h = h_ref[...]
    hn = _rms_rows(h, g_ref[...]).astype(BF16)
    gate = jax.nn.sigmoid(_dot(hn, wg_ref[...]))
    proj = _dot(p_ref[...].astype(BF16), wp_ref[...])
    y = h + gate * proj
    o_ref[...] = y
    if tail_ref:
        @pl.when(pl.program_id(0) == pl.num_programs(0) - 1)
        def _():
            nt = tail_ref[0].shape[0]
            tail_ref[0][...] = y[y.shape[0] - nt:, :]


def ple_residual(h, g, p, wg, wp, layer, name, n_tail=0):
    n, d = h.shape
    dp = p.shape[2]
    tm = ROW_TILE
    assert n % tm == 0 and (n_tail == 0 or (n - n_tail) % tm + n_tail == tm)
    out_shape = [jax.ShapeDtypeStruct((n - n_tail, d), F32)]
    out_specs = [pl.BlockSpec((tm, d), lambda i: (i, 0))]
    if n_tail:
        out_shape.append(jax.ShapeDtypeStruct((n_tail, d), F32))
        out_specs.append(pl.BlockSpec((n_tail, d), lambda i: (0, 0)))
    out = pl.pallas_call(
        _ple_kernel,
        out_shape=out_shape,
        grid=(n // tm,),
        in_specs=[
            pl.BlockSpec((tm, d), lambda i: (i, 0)),
            pl.BlockSpec((1, d), lambda i: (0, 0)),
            pl.BlockSpec((None, tm, dp), lambda i: (layer, i, 0)),
            pl.BlockSpec((None, d, d), lambda i: (layer, 0, 0)),
            pl.BlockSpec((None, dp, d), lambda i: (layer, 0, 0)),
        ],
        out_specs=out_specs,
        compiler_params=_cp(("arbitrary",)),
        name=name,
    )(h, g.reshape(1, d), p, wg, wp)
    return out if n_tail else out[0]


Z0_QA, Z0_KA, Z0_VA, Z0_QB, Z0_KB, Z0_VB, Z0_GOUT = 0, 1024, 1280, 1536, 2048, 2560, 3584
Z0_WIDTH = 4608


def _swa_prompt_kernel(sink_ref, q_ref, kvp_ref, kvc_ref, gq_ref, gk_ref, o_ref, ko_ref, vo_ref):
    n = pl.program_id(1)
    blk = q_ref.shape[0]
    grp = H_A // KVH_A
    kw = KVH_A * HD_A
    lanes = 2 * HD_A
    gq = gq_ref[...]
    gk = gk_ref[...]
    kv = jnp.concatenate([kvp_ref[...], kvc_ref[...]], axis=0)
    rows = lax.broadcasted_iota(jnp.int32, (grp * blk, 2 * blk), 0)
    cols = lax.broadcasted_iota(jnp.int32, (grp * blk, 2 * blk), 1)
    diff = (rows & (blk - 1)) - cols + blk
    lo = jnp.where(n == 0, blk, 0)
    mask = (diff >= 0) & (diff <= WINDOW) & (cols >= lo)

    r = lax.broadcasted_iota(jnp.int32, (lanes, lanes), 0)
    c = lax.broadcasted_iota(jnp.int32, (lanes, lanes), 1)
    head_sum = jnp.where(r // HD_A == c // HD_A, 1.0, 0.0).astype(BF16)
    spread = [jnp.where(r == (c & (HD_A - 1)) + half * HD_A, 1.0, 0.0).astype(BF16) for half in range(2)]
    one = jnp.ones((), BF16)
    low_half = lax.broadcasted_iota(jnp.int32, (blk, lanes), 1) < HD_A
    low_kv = lax.broadcasted_iota(jnp.int32, (2 * blk, lanes), 1) < HD_A

    def head_norm(x, g):
        hi, lo_ = _split_bf16(x * x)
        ss = _dot(hi, head_sum) + _dot(lo_, head_sum)
        return (x * lax.rsqrt(ss * (1.0 / HD_A) + EPS)) * g

    for pair in range(KVH_A // 2):
        ps = slice(pair * lanes, (pair + 1) * lanes)
        kn_pair = head_norm(kv[:, ps], gk)
        ko_ref[0, :, ps] = kn_pair[blk:]
        kn_b = kn_pair.astype(BF16)
        v_b = kv[:, kw + pair * lanes: kw + (pair + 1) * lanes].astype(BF16)
        for half in range(2):
            kh = 2 * pair + half
            kd = _dot(kn_b, spread[half]).astype(BF16)
            vd = _dot(v_b, spread[half]).astype(BF16)
            s_low, s_high = [], []
            for j in range(grp // 2):
                qs = slice(kh * grp * HD_A + j * lanes, kh * grp * HD_A + (j + 1) * lanes)
                qn = head_norm(q_ref[:, qs], gq)
                s_low.append(_dot_nt(jnp.where(low_half, qn, 0.0).astype(BF16), kd))
                s_high.append(_dot_nt(jnp.where(low_half, 0.0, qn).astype(BF16), kd))
            s = jnp.concatenate(s_low + s_high, axis=0) * (HD_A ** -0.5)
            s = jnp.where(mask, s, -jnp.inf)
            order = [2 * j for j in range(grp // 2)] + [2 * j + 1 for j in range(grp // 2)]
            sink = jnp.concatenate([jnp.full((blk, 1), sink_ref[kh * grp + g], F32) for g in order], axis=0)
            m = jnp.maximum(jnp.max(s, axis=-1, keepdims=True), sink)
            e = jnp.exp(s - m).astype(BF16)
            e_sink = jnp.exp(sink - m)
            nh = (grp // 2) * blk
            o_low = _dot(e[:nh], jnp.where(low_kv, vd, one))
            o_high = _dot(e[nh:], jnp.where(low_kv, one, vd))
            o_low = o_low / (pltpu.roll(o_low, HD_A, 1) + e_sink[:nh])
            o_high = o_high / (pltpu.roll(o_high, HD_A, 1) + e_sink[nh:])
            for j in range(grp // 2):
                o_pair = jnp.where(low_half, o_low[j * blk:(j + 1) * blk], o_high[j * blk:(j + 1) * blk])
                o_ref[:, kh * grp * HD_A + j * lanes: kh * grp * HD_A + (j + 1) * lanes] = o_pair.astype(BF16)
    vo_ref[0] = kvc_ref[:, kw:]


def swa_prompt(z0, gq, gk, sinks, bsz, t):
    blk = 128
    nb = t // blk
    kvw = 2 * KVH_A * HD_A
    kv_col = Z0_KA // kvw
    grid_spec = pltpu.PrefetchScalarGridSpec(
        num_scalar_prefetch=1,
        grid=(bsz, nb),
        in_specs=[
            pl.BlockSpec((blk, H_A * HD_A), lambda b, n, s: (b * nb + n, 0)),
            pl.BlockSpec((blk, kvw), lambda b, n, s: (b * nb + jnp.maximum(n - 1, 0), kv_col)),
            pl.BlockSpec((blk, kvw), lambda b, n, s: (b * nb + n, kv_col)),
            pl.BlockSpec((1, 2 * HD_A), lambda b, n, s: (0, 0)),
            pl.BlockSpec((1, 2 * HD_A), lambda b, n, s: (0, 0)),
        ],
        out_specs=[
            pl.BlockSpec((blk, H_A * HD_A), lambda b, n, s: (b * nb + n, 0)),
            pl.BlockSpec((1, blk, KVH_A * HD_A), lambda b, n, s: (b, 0, 0)),
            pl.BlockSpec((1, blk, KVH_A * HD_A), lambda b, n, s: (b, 0, 0)),
        ],
    )
    return pl.pallas_call(
        _swa_prompt_kernel,
        out_shape=[
            jax.ShapeDtypeStruct((bsz * t, H_A * HD_A), BF16),
            jax.ShapeDtypeStruct((bsz, blk, KVH_A * HD_A), F32),
            jax.ShapeDtypeStruct((bsz, blk, KVH_A * HD_A), F32),
        ],
        grid_spec=grid_spec,
        compiler_params=_cp(("parallel", "arbitrary")),
        name="swa_prompt",
    )(sinks, z0, z0, z0, jnp.tile(gq, 2).reshape(1, 2 * HD_A), jnp.tile(gk, 2).reshape(1, 2 * HD_A))


def _swa_decode_kernel(q_ref, k_ref, v_ref, kt_ref, vt_ref, gq_ref, gk_ref, sink_ref, o_ref, kto_ref, vto_ref):
    nb = q_ref.shape[0]
    grp = H_A // KVH_A
    kw = KVH_A * HD_A
    buf_len = kt_ref.shape[2]
    gq = gq_ref[...]
    gk = gk_ref[...]
    sink = sink_ref[...]
    head = lax.broadcasted_iota(jnp.int32, (H_A, kw), 0)
    lane_blk = lax.broadcasted_iota(jnp.int32, (H_A, kw), 1) // HD_A
    own = lane_blk == head // grp
    last = lax.broadcasted_iota(jnp.int32, (kw, buf_len), 1) == buf_len - 1
    kn_rows = jnp.concatenate(
        [_rms_rows(k_ref[:, kh * HD_A:(kh + 1) * HD_A], gk) for kh in range(KVH_A)], axis=-1)
    v_rows = v_ref[...]
    kn_cols = kn_rows.T
    v_cols = v_rows.T
    scale = HD_A ** -0.5
    qn = _rms_rows(q_ref[...].reshape(nb * H_A, HD_A), gq)
    own_all = jnp.concatenate([own] * nb, axis=0)
    qblk = jnp.where(own_all, jnp.concatenate([qn] * KVH_A, axis=-1), 0.0)
    qblk_b = qblk.astype(BF16)
    rows = [slice(i * H_A, (i + 1) * H_A) for i in range(nb)]
    s = jnp.concatenate([_dot(qblk_b[rows[i]], kt_ref[i].astype(BF16)) for i in range(nb)], axis=0) * scale
    k_new = jnp.concatenate([jnp.broadcast_to(kn_rows[i:i + 1], (H_A, kw)) for i in range(nb)], axis=0)
    v_new = jnp.concatenate([jnp.broadcast_to(v_rows[i:i + 1], (H_A, kw)) for i in range(nb)], axis=0)
    s_new = jnp.sum(qblk * k_new, axis=-1, keepdims=True) * scale
    sink_all = jnp.concatenate([sink] * nb, axis=0)
    m = jnp.maximum(jnp.maximum(jnp.max(s, axis=-1, keepdims=True), s_new), sink_all)
    e = jnp.exp(s - m)
    e_new = jnp.exp(s_new - m)
    den = jnp.sum(e, axis=-1, keepdims=True) + e_new + jnp.exp(sink_all - m)
    e_b = e.astype(BF16)
    o_all = jnp.concatenate([_dot_nt(e_b[rows[i]], vt_ref[i].astype(BF16)) for i in range(nb)], axis=0)
    o_all = jnp.where(own_all, o_all + e_new * v_new, 0.0)
    o = o_all[:, 0:HD_A]
    for kh in range(1, KVH_A):
        o = o + o_all[:, kh * HD_A:(kh + 1) * HD_A]
    o_ref[...] = (o / den).astype(BF16).reshape(nb, H_A, HD_A)
    for i in range(nb):
        kto_ref[i] = jnp.where(last, kn_cols[:, i:i + 1], pltpu.roll(kt_ref[i], buf_len - 1, 1))
        vto_ref[i] = jnp.where(last, v_cols[:, i:i + 1], pltpu.roll(vt_ref[i], buf_len - 1, 1))


def swa_decode(q, k, v, kt, vt, gq, gk, sinks):
    bsz, kw, buf_len = kt.shape
    nb = 8
    rows = pl.BlockSpec((nb, kw), lambda i: (i, 0))
    heads = pl.BlockSpec((nb, H_A, HD_A), lambda i: (i, 0, 0))
    buf = pl.BlockSpec((nb, kw, buf_len), lambda i: (i, 0, 0))
    vec = pl.BlockSpec((1, HD_A), lambda i: (0, 0))
    return pl.pallas_call(
        _swa_decode_kernel,
        out_shape=[
            jax.ShapeDtypeStruct((bsz, H_A, HD_A), BF16),
            jax.ShapeDtypeStruct(kt.shape, F32),
            jax.ShapeDtypeStruct(vt.shape, F32),
        ],
        grid=(bsz // nb,),
        in_specs=[heads, rows, rows, buf, buf, vec, vec, pl.BlockSpec((H_A, 1), lambda i: (0, 0))],
        out_specs=[heads, buf, buf],
        compiler_params=_cp(("parallel",)),
        name="swa_decode",
    )(q, k, v, kt, vt, gq.reshape(1, HD_A), gk.reshape(1, HD_A), sinks.reshape(H_A, 1))


def _gla_logdec(glr, wg2_ref, bg2_ref):
    gl = _dot(glr.astype(BF16), wg2_ref[...]) + bg2_ref[...]
    return _log_sigmoid(gl) * (1.0 / GLA_GATE_NORM)


def _col(row8, i):
    return row8.T[:, i:i + 1]


def _gla_out(o, g_go, gout):
    return (_rms_rows(o, g_go) * (gout * jax.nn.sigmoid(gout))).astype(BF16)


def _gla_prompt_kernel(q_ref, k_ref, v01_ref, v23_ref, gout01_ref, gout23_ref, glr_ref, wg2_ref, bg2_ref, ggo_ref,
                       o_ref, so_ref, s_ref):
    c_idx = pl.program_id(1)
    tb = q_ref.shape[0]
    ck = GLA_CHUNK

    @pl.when(c_idx == 0)
    def _():
        s_ref[...] = jnp.zeros_like(s_ref)

    nck = tb // ck
    ld = _gla_logdec(glr_ref[...], wg2_ref, bg2_ref)
    ti = lax.broadcasted_iota(jnp.int32, (tb, tb), 0)
    si = lax.broadcasted_iota(jnp.int32, (tb, tb), 1)
    causal = (si <= ti) & (si // ck == ti // ck)
    tri = jnp.where(causal, 1.0, 0.0).astype(BF16)
    hi, lo = _split_bf16(ld)
    b = _dot(tri, hi) + _dot(tri, lo)
    last_rows = [b[(c + 1) * ck - 1:(c + 1) * ck, :] for c in range(nck)]
    b_last = jnp.concatenate([jnp.broadcast_to(r, (ck, r.shape[1])) for r in last_rows], axis=0)
    k = k_ref[...]
    qt = ((q_ref[...] * (DK_B ** -0.5)) * jnp.exp(b)).astype(BF16)
    kt = (k * jnp.exp(-b)).astype(BF16)
    kd = (k * jnp.exp(b_last - b)).astype(BF16)
    g_go = ggo_ref[...]
    for h in range(H_B):
        ks = slice(h * DK_B, (h + 1) * DK_B)
        vs = slice(h * DV_B, (h + 1) * DV_B)
        hs = slice((h % 2) * DV_B, (h % 2 + 1) * DV_B)
        v = (v01_ref, v23_ref)[h // 2][:, hs].astype(BF16)
        att = jnp.where(causal, _dot_nt(qt[:, ks], kt[:, ks]), 0.0)
        o = _dot(att.astype(BF16), v)
        state = s_ref[h]
        o_state = []
        for c in range(nck):
            rs = slice(c * ck, (c + 1) * ck)
            o_state.append(_dot(qt[rs, ks], state.astype(BF16)))
            d_state = _dot_tn(kd[rs, ks], v[rs])
            decay = _col(jnp.broadcast_to(jnp.exp(last_rows[c][:, ks]), (8, DK_B)), 0)
            state = decay * state + d_state
        s_ref[h] = state
        o = o + jnp.concatenate(o_state, axis=0)
        o_ref[:, vs] = _gla_out(o, g_go, (gout01_ref, gout23_ref)[h // 2][:, hs])
    so_ref[0] = s_ref[...]


def gla_prompt(z0, glr, wg2, bg2, g_go, bsz, t):
    tb = 256
    nt = t // tb
    qk_w = H_B * DK_B
    v_w = H_B * DV_B
    return pl.pallas_call(
        _gla_prompt_kernel,
        out_shape=[
            jax.ShapeDtypeStruct((bsz * t, v_w), BF16),
            jax.ShapeDtypeStruct((bsz, H_B, DK_B, DV_B), F32),
        ],
        grid=(bsz, nt),
        in_specs=[
            pl.BlockSpec((tb, qk_w), lambda b, c: (b * nt + c, Z0_QB // qk_w)),
            pl.BlockSpec((tb, qk_w), lambda b, c: (b * nt + c, Z0_KB // qk_w)),
            pl.BlockSpec((tb, v_w // 2), lambda b, c: (b * nt + c, Z0_VB // (v_w // 2))),
            pl.BlockSpec((tb, v_w // 2), lambda b, c: (b * nt + c, Z0_VB // (v_w // 2) + 1)),
            pl.BlockSpec((tb, v_w // 2), lambda b, c: (b * nt + c, Z0_GOUT // (v_w // 2))),
            pl.BlockSpec((tb, v_w // 2), lambda b, c: (b * nt + c, Z0_GOUT // (v_w // 2) + 1)),
            pl.BlockSpec((tb, 128), lambda b, c: (b * nt + c, 0)),
            pl.BlockSpec((128, qk_w), lambda b, c: (0, 0)),
            pl.BlockSpec((1, qk_w), lambda b, c: (0, 0)),
            pl.BlockSpec((1, DV_B), lambda b, c: (0, 0)),
        ],
        out_specs=[
            pl.BlockSpec((tb, v_w), lambda b, c: (b * nt + c, 0)),
            pl.BlockSpec((1, H_B, DK_B, DV_B), lambda b, c: (b, 0, 0, 0)),
        ],
        scratch_shapes=[pltpu.VMEM((H_B, DK_B, DV_B), F32)],
        compiler_params=_cp(("parallel", "arbitrary")),
        name="gla_prompt",
    )(z0, z0, z0, z0, z0, z0, glr, wg2, bg2.reshape(1, qk_w), g_go.reshape(1, DV_B))


def _gla_decode_kernel(q_ref, k_ref, v_ref, gout_ref, glr_ref, wg2_ref, bg2_ref, ggo_ref, s_ref, o_ref, so_ref):
    nb = q_ref.shape[0]
    ld = _gla_logdec(glr_ref[...], wg2_ref, bg2_ref)
    g_go = ggo_ref[...]
    for h in range(H_B):
        ks = slice(h * DK_B, (h + 1) * DK_B)
        vs = slice(h * DV_B, (h + 1) * DV_B)
        eg_t = jnp.exp(ld[:, ks]).T
        k_t = k_ref[:, ks].T
        q_b = (q_ref[:, ks] * (DK_B ** -0.5)).astype(BF16)
        outs = []
        for i in range(nb):
            state = eg_t[:, i:i + 1] * s_ref[i, h] + k_t[:, i:i + 1] * v_ref[i:i + 1, vs]
            so_ref[i, h] = state
            outs.append(_dot(q_b, state.astype(BF16))[i:i + 1])
        o = jnp.concatenate(outs, axis=0)
        o_ref[:, vs] = _gla_out(o, g_go, gout_ref[:, vs])


def gla_decode(q, k, v, gout, glr, wg2, bg2, g_go, state):
    bsz = q.shape[0]
    nb = 8
    qk_w = H_B * DK_B
    v_w = H_B * DV_B
    rows = lambda w: pl.BlockSpec((nb, w), lambda i: (i, 0))
    st = pl.BlockSpec((nb, H_B, DK_B, DV_B), lambda i: (i, 0, 0, 0))
    return pl.pallas_call(
        _gla_decode_kernel,
        out_shape=[
            jax.ShapeDtypeStruct((bsz, v_w), BF16),
            jax.ShapeDtypeStruct(state.shape, F32),
        ],
        grid=(bsz // nb,),
        in_specs=[
            rows(qk_w), rows(qk_w), rows(v_w), rows(v_w), rows(128),
            pl.BlockSpec((128, qk_w), lambda i: (0, 0)),
            pl.BlockSpec((1, qk_w), lambda i: (0, 0)),
            pl.BlockSpec((1, DV_B), lambda i: (0, 0)),
            st,
        ],
        out_specs=[rows(v_w), st],
        compiler_params=_cp(("parallel",)),
        name="gla_decode",
    )(q, k, v, gout, glr, wg2, bg2.reshape(1, qk_w), g_go.reshape(1, DV_B), state)


Z1_CX, Z1_CY, Z1_QD, Z1_KD, Z1_VD = 0, 1024, 2048, 3072, 3584


def _rg_gates(xc, wa_ref, ba_ref, wx_ref, bx_ref, lam_ref):
    xb = xc.astype(BF16)
    ra = jnp.concatenate(
        [_dot(xb[:, n * C_BLOCK:(n + 1) * C_BLOCK], wa_ref[n]) for n in range(C_BLOCKS)], axis=-1) + ba_ref[...]
    rx = jnp.concatenate(
        [_dot(xb[:, n * C_BLOCK:(n + 1) * C_BLOCK], wx_ref[n]) for n in range(C_BLOCKS)], axis=-1) + bx_ref[...]
    r_gate = jax.nn.sigmoid(ra)
    i_gate = jax.nn.sigmoid(rx)
    log_a = (-RG_C * r_gate) * _softplus(-lam_ref[...])
    a = jnp.exp(log_a)
    one_minus_a2 = -jnp.tanh(log_a) * (a * a + 1.0)
    u = jnp.sqrt(one_minus_a2) * (i_gate * xc)
    return a, u


def _rglru_prompt_kernel(cx_ref, cy_ref, cw_ref, cb_ref, wa_ref, ba_ref, wx_ref, bx_ref, lam_ref,
                         o_ref, tail_ref, hl_ref, prev_ref, hc_ref):
    t_idx = pl.program_id(1)
    tt = cx_ref.shape[0]

    @pl.when(t_idx == 0)
    def _():
        prev_ref[...] = jnp.zeros_like(prev_ref)
        hc_ref[...] = jnp.zeros_like(hc_ref)

    x = cx_ref[...]
    xp = jnp.concatenate([prev_ref[...], x], axis=0)
    xc = cb_ref[...] + cw_ref[CONV_W - 1:CONV_W, :] * x
    for j in range(CONV_W - 1):
        d = CONV_W - 1 - j
        xc = xc + cw_ref[j:j + 1, :] * pltpu.roll(xp, d, 0)[8:]
    prev_ref[...] = x[tt - 8:]
    tail_ref[0] = x[tt - 8:]

    a, u = _rg_gates(xc, wa_ref, ba_ref, wx_ref, bx_ref, lam_ref)
    sub = lax.broadcasted_iota(jnp.int32, a.shape, 0) & 7
    for d in (1, 2, 4):
        inside = sub >= d
        u = a * jnp.where(inside, pltpu.roll(u, d, 0), 0.0) + u
        a = a * jnp.where(inside, pltpu.roll(a, d, 0), 1.0)
    h_prev = hc_ref[...]
    tiles = []
    for r in range(0, tt, 8):
        tiles.append(a[r:r + 8] * h_prev + u[r:r + 8])
        h_prev = tiles[-1][7:8]
    h = jnp.concatenate(tiles, axis=0)
    hc_ref[...] = h[tt - 1:]
    hl_ref[0] = h[tt - 1:]
    o_ref[...] = (h * jax.nn.gelu(cy_ref[...])).astype(BF16)


def rglru_prompt(z1, conv_w, conv_b, wa, ba, wx, bx, lam, bsz, t):
    tt = 256
    nt = t // tt
    c = C_WIDTH
    vec = pl.BlockSpec((1, c), lambda b, i: (0, 0))
    wblk = pl.BlockSpec((C_BLOCKS, C_BLOCK, C_BLOCK), lambda b, i: (0, 0, 0))
    return pl.pallas_call(
        _rglru_prompt_kernel,
        out_shape=[
            jax.ShapeDtypeStruct((bsz * t, c), BF16),
            jax.ShapeDtypeStruct((bsz, 8, c), F32),
            jax.ShapeDtypeStruct((bsz, 1, c), F32),
        ],
        grid=(bsz, nt),
        in_specs=[
            pl.BlockSpec((tt, c), lambda b, i: (b * nt + i, Z1_CX // c)),
            pl.BlockSpec((tt, c), lambda b, i: (b * nt + i, Z1_CY // c)),
            pl.BlockSpec((CONV_W, c), lambda b, i: (0, 0)),
            vec, wblk, vec, wblk, vec, vec,
        ],
        out_specs=[
            pl.BlockSpec((tt, c), lambda b, i: (b * nt + i, 0)),
            pl.BlockSpec((1, 8, c), lambda b, i: (b, 0, 0)),
            pl.BlockSpec((1, 1, c), lambda b, i: (b, 0, 0)),
        ],
        scratch_shapes=[pltpu.VMEM((8, c), F32), pltpu.VMEM((1, c), F32)],
        compiler_params=_cp(("parallel", "arbitrary")),
        name="rglru_prompt",
    )(z1, z1, conv_w, conv_b.reshape(1, c), wa, ba.reshape(1, c), wx, bx.reshape(1, c), lam.reshape(1, c))


def _rglru_decode_kernel(cx_ref, cy_ref, buf_ref, h0_ref, cw_ref, cb_ref, wa_ref, ba_ref, wx_ref, bx_ref, lam_ref,
                         o_ref, nbuf_ref, hl_ref):
    x = cx_ref[...]
    xc = cb_ref[...] + cw_ref[CONV_W - 1:CONV_W, :] * x
    for j in range(CONV_W - 1):
        xc = xc + cw_ref[j:j + 1, :] * buf_ref[j]
    for j in range(CONV_W - 2):
        nbuf_ref[j] = buf_ref[j + 1]
    nbuf_ref[CONV_W - 2] = x
    a, u = _rg_gates(xc, wa_ref, ba_ref, wx_ref, bx_ref, lam_ref)
    h = a * h0_ref[...] + u
    hl_ref[...] = h
    o_ref[...] = (h * jax.nn.gelu(cy_ref[...])).astype(BF16)


def rglru_decode(cx, cy, buf, h0, conv_w, conv_b, wa, ba, wx, bx, lam):
    bsz, c = cx.shape
    return pl.pallas_call(
        _rglru_decode_kernel,
        out_shape=[
            jax.ShapeDtypeStruct((bsz, c), BF16),
            jax.ShapeDtypeStruct(buf.shape, F32),
            jax.ShapeDtypeStruct((bsz, c), F32),
        ],
        compiler_params=pltpu.CompilerParams(vmem_limit_bytes=V7X_VMEM_LIMIT_BYTES),
        name="rglru_decode",
    )(cx, cy, buf, h0, conv_w, conv_b.reshape(1, c), wa, ba.reshape(1, c), wx, bx.reshape(1, c), lam.reshape(1, c))


def _strict_upper_ones(n):
    j = lax.broadcasted_iota(jnp.int32, (n, n), 0)
    s = lax.broadcasted_iota(jnp.int32, (n, n), 1)
    return jnp.where(j > s, 1.0, 0.0).astype(BF16)


def _from_here_ones(n):
    j = lax.broadcasted_iota(jnp.int32, (n, n), 0)
    s = lax.broadcasted_iota(jnp.int32, (n, n), 1)
    return jnp.where(j >= s, 1.0, 0.0).astype(BF16)


def _sb_tile(z, from_here, dead, mask):
    sp = _softplus(z)
    if mask is not None:
        sp = jnp.where(mask, sp, 0.0)
    hi, lo = _split_bf16(sp)
    cum = _dot(hi, from_here) + _dot(lo, from_here)
    w = jnp.exp(z - cum - dead)
    if mask is not None:
        w = jnp.where(mask, w, 0.0)
    return w, dead + cum[:, 0:1]


def _sb_prompt_kernel(bias_ref, q_ref, k_ref, v_ref, gq_ref, gk_ref, o_ref, ko_ref, kb_ref, vb_ref):
    kh = pl.program_id(1)
    i = pl.program_id(2)
    blk = q_ref.shape[0]
    grp = H_D // KVH_D

    @pl.when(i == 0)
    def _():
        kn = _rms_rows(k_ref[...], gk_ref[...])
        ko_ref[...] = kn
        kb_ref[...] = kn.astype(BF16)
        vb_ref[...] = v_ref[...].astype(BF16)

    gq = gq_ref[...]
    qs = jnp.concatenate(
        [_rms_rows(q_ref[:, g * HD_D:(g + 1) * HD_D], gq) for g in range(grp)], axis=0).astype(BF16)
    bias = jnp.concatenate(
        [jnp.full((blk, 1), bias_ref[kh * grp + g], F32) for g in range(grp)], axis=0)
    from_here = _from_here_ones(blk)
    scale = HD_D ** -0.5

    def tile(j, mask, acc, dead):
        off = pl.multiple_of(j * blk, blk)
        z = _dot_nt(qs, kb_ref[pl.ds(off, blk), :]) * scale + bias
        w, dead = _sb_tile(z, from_here, dead, mask)
        return acc + _dot(w.astype(BF16), vb_ref[pl.ds(off, blk), :]), dead

    rows = lax.broadcasted_iota(jnp.int32, (grp * blk, blk), 0) & (blk - 1)
    cols = lax.broadcasted_iota(jnp.int32, (grp * blk, blk), 1)
    acc, dead = tile(i, cols < rows, jnp.zeros((grp * blk, HD_D), F32), jnp.zeros((grp * blk, 1), F32))

    def pair(step, carry):
        j = i - 1 - 2 * step
        return tile(j - 1, None, *tile(j, None, *carry))

    def last(step, carry):
        return tile(0, None, *carry)

    n_pairs = i // 2
    carry = lax.fori_loop(0, n_pairs, pair, (acc, dead))
    acc, _ = lax.fori_loop(0, i - 2 * n_pairs, last, carry)
    for g in range(grp):
        o_ref[:, g * HD_D:(g + 1) * HD_D] = acc[g * blk:(g + 1) * blk].astype(BF16)


def sb_prompt(z1, gq, gk, bias, bsz, t):
    blk = SB_PROMPT_TILE
    nb = t // blk
    grp = H_D // KVH_D
    qw = grp * HD_D
    grid_spec = pltpu.PrefetchScalarGridSpec(
        num_scalar_prefetch=1,
        grid=(bsz, KVH_D, nb),
        in_specs=[
            pl.BlockSpec((blk, qw), lambda b, k, i, s: (b * nb + i, Z1_QD // qw + k)),
            pl.BlockSpec((t, HD_D), lambda b, k, i, s: (b, Z1_KD // HD_D + k)),
            pl.BlockSpec((t, HD_D), lambda b, k, i, s: (b, Z1_VD // HD_D + k)),
            pl.BlockSpec((1, HD_D), lambda b, k, i, s: (0, 0)),
            pl.BlockSpec((1, HD_D), lambda b, k, i, s: (0, 0)),
        ],
        out_specs=[
            pl.BlockSpec((blk, qw), lambda b, k, i, s: (b * nb + i, k)),
            pl.BlockSpec((t, HD_D), lambda b, k, i, s: (b, k)),
        ],
        scratch_shapes=[pltpu.VMEM((t, HD_D), BF16), pltpu.VMEM((t, HD_D), BF16)],
    )
    return pl.pallas_call(
        _sb_prompt_kernel,
        out_shape=[
            jax.ShapeDtypeStruct((bsz * t, H_D * HD_D), BF16),
            jax.ShapeDtypeStruct((bsz * t, KVH_D * HD_D), F32),
        ],
        grid_spec=grid_spec,
        compiler_params=_cp(("parallel", "parallel", "arbitrary")),
        name="sb_prompt",
    )(bias, z1, z1, z1, gq.reshape(1, HD_D), gk.reshape(1, HD_D))


def _sb_decode_kernel(n_pages, pt_ref, bias_ref, q_ref, k_ref, v_ref, gq_ref, gk_ref, ck_hbm, cv_hbm,
                      o_ref, ko_ref, kbuf, vbuf, sem):
    b = pl.program_id(0)
    nb = pl.num_programs(0)
    npg = SB_PAGES_PER_STEP
    n_chunks = n_pages // npg
    prow = PAGE * KVH_D
    grp = H_D // KVH_D
    scale = HD_D ** -0.5

    def page_copies(bb, chunk, slot):
        cps = []
        for p in range(npg):
            page = pt_ref[bb * n_pages + chunk * npg + p]
            dst = pl.ds(p * prow, prow)
            cps.append(pltpu.make_async_copy(ck_hbm.at[page], kbuf.at[slot, dst], sem.at[0, slot]))
            cps.append(pltpu.make_async_copy(cv_hbm.at[page], vbuf.at[slot, dst], sem.at[1, slot]))
        return cps

    @pl.when(b == 0)
    def _():
        for cp in page_copies(0, n_chunks - 1, 0):
            cp.start()

    bias = jnp.concatenate([jnp.full((1, 1), bias_ref[h], F32) for h in range(H_D)], axis=0)
    qn = _rms_rows(q_ref[0], gq_ref[...])
    qb = qn.astype(BF16)
    kn = _rms_rows(k_ref[0], gk_ref[...])
    ko_ref[0] = kn
    kn_sel = jnp.concatenate([kn[h // grp:h // grp + 1] for h in range(H_D)], axis=0)
    v_sel = jnp.concatenate([v_ref[0, h // grp:h // grp + 1] for h in range(H_D)], axis=0)
    z0 = jnp.sum(qn * kn_sel, axis=-1, keepdims=True) * scale + bias
    visible = jnp.zeros((H_D, 1), jnp.int32) < jnp.zeros((H_D, 1), jnp.int32)
    sp0 = _softplus(z0)
    acc = jnp.where(visible, jnp.exp(z0 - sp0), 0.0) * v_sel
    surv = jnp.where(visible, -sp0, 0.0)

    later = _strict_upper_ones(prow)
    row = lax.broadcasted_iota(jnp.int32, (npg * H_D, prow), 0)
    col = lax.broadcasted_iota(jnp.int32, (npg * H_D, prow), 1)
    valid = (col & (KVH_D - 1)) == ((row & (H_D - 1)) // grp)
    bias_r = jnp.concatenate([bias] * npg, axis=0)

    for i in range(n_chunks):
        slot = i % 2
        if i + 1 < n_chunks:
            for cp in page_copies(b, n_chunks - 2 - i, 1 - slot):
                cp.start()
        else:
            @pl.when(b + 1 < nb)
            def _():
                for cp in page_copies(b + 1, n_chunks - 1, 1 - slot):
                    cp.start()
        for cp in page_copies(b, n_chunks - 1 - i, slot):
            cp.wait()

        z = _dot_nt(qb, kbuf[slot].astype(BF16))
        z = jnp.concatenate([z[:, p * prow:(p + 1) * prow] for p in range(npg)], axis=0) * scale + bias_r
        sp = _softplus(z)
        l1m = jnp.where(valid, -sp, 0.0)
        hi, lo = _split_bf16(l1m)
        suffix = _dot(hi, later) + _dot(lo, later)
        tot = jnp.sum(l1m, axis=-1, keepdims=True)
        survs = [None] * npg
        for p in reversed(range(npg)):
            survs[p] = surv
            surv = surv + tot[p * H_D:(p + 1) * H_D]
        w = jnp.where(valid, jnp.exp((z - sp) + suffix + jnp.concatenate(survs, axis=0)), 0.0)
        w = jnp.concatenate([w[p * H_D:(p + 1) * H_D] for p in range(npg)], axis=1).astype(BF16)
        acc = acc + _dot(w, vbuf[slot].astype(BF16))

    o_ref[0] = acc.astype(BF16)


def sb_decode(q, k, v, gq, gk, bias, cache_k, cache_v, page_table):
    bsz = q.shape[0]
    n_pages = page_table.shape[1]
    npg = SB_PAGES_PER_STEP
    assert n_pages % (2 * npg) == 0
    prow = PAGE * KVH_D
    grid_spec = pltpu.PrefetchScalarGridSpec(
        num_scalar_prefetch=2,
        grid=(bsz,),
        in_specs=[
            pl.BlockSpec((1, H_D, HD_D), lambda b, pt, bias_: (b, 0, 0)),
            pl.BlockSpec((1, KVH_D, HD_D), lambda b, pt, bias_: (b, 0, 0)),
            pl.BlockSpec((1, KVH_D, HD_D), lambda b, pt, bias_: (b, 0, 0)),
            pl.BlockSpec((1, HD_D), lambda b, pt, bias_: (0, 0)),
            pl.BlockSpec((1, HD_D), lambda b, pt, bias_: (0, 0)),
            pl.BlockSpec(memory_space=pl.ANY),
            pl.BlockSpec(memory_space=pl.ANY),
        ],
        out_specs=[
            pl.BlockSpec((1, H_D, HD_D), lambda b, pt, bias_: (b, 0, 0)),
            pl.BlockSpec((1, KVH_D, HD_D), lambda b, pt, bias_: (b, 0, 0)),
        ],
        scratch_shapes=[
            pltpu.VMEM((2, npg * prow, HD_D), F32),
            pltpu.VMEM((2, npg * prow, HD_D), F32),
            pltpu.SemaphoreType.DMA((2, 2)),
        ],
    )
    return pl.pallas_call(
        functools.partial(_sb_decode_kernel, n_pages),
        out_shape=[
            jax.ShapeDtypeStruct((bsz, H_D, HD_D), BF16),
            jax.ShapeDtypeStruct((bsz, KVH_D, HD_D), F32),
        ],
        grid_spec=grid_spec,
        compiler_params=_cp(("arbitrary",)),
        name="sb_decode",
    )(page_table.reshape(-1), bias, q, k, v, gq.reshape(1, HD_D), gk.reshape(1, HD_D), cache_k, cache_v)


def kernel(x_prompt, x_sample, state_swa_k, state_swa_v, state_gla, state_conv, state_lru, cache_sb_k, cache_sb_v, page_table, p_prompt, p_sample, g_mix, g_ffn, w_ffn_gate, w_ffn_up, w_ffn_down, g_ple, w_ple_gate, w_ple_proj, w_in_even, g_qnorm_a, g_knorm_a, sinks_a, w_gla_gate2, b_gla_gate2, g_gla_out, w_out_even, w_in_odd, conv_w, conv_b, w_rg_a, b_rg_a, w_rg_x, b_rg_x, lru_lambda, g_qnorm_d, g_knorm_d, sb_bias, w_out_odd):
    bsz, t, d = x_prompt.shape
    dbs = x_sample.shape[0]
    n_p = bsz * t
    depth = g_mix.shape[0]
    ple = p_prompt.shape[-1]

    x_p = x_prompt.reshape(n_p, d)
    x_s = x_sample.reshape(dbs, d)
    p_all = jnp.concatenate([p_prompt.reshape(depth, n_p, ple), p_sample.reshape(depth, dbs, ple)], axis=1)

    ffn_w = (w_ffn_gate.astype(BF16), w_ffn_up.astype(BF16), w_ffn_down.astype(BF16))
    ple_w = (w_ple_gate.astype(BF16), w_ple_proj.astype(BF16))

    def dense_tail(h, i, n_tail=0):
        h = ffn_residual(h, g_ffn[i], *ffn_w, i, 512, f"ffn_{i}")
        return ple_residual(h, g_ple[i], p_all, *ple_w, i, f"ple_{i}", n_tail)

    w = w_in_even[0]
    glr_col = Z0_GOUT
    w0 = jnp.concatenate([w[:, :glr_col].astype(BF16), w[:, glr_col + GLA_RANK:].astype(BF16)], axis=1)
    w0_glr = jnp.pad(w[:, glr_col:glr_col + GLA_RANK], ((0, 0), (0, 128 - GLA_RANK))).astype(BF16)
    z0, glr = norm_matmul(x_p, g_mix[0], w0, 1536, "in_proj_even", w_side=w0_glr, x_tail=x_s)
    wg2 = jnp.pad(w_gla_gate2[0], ((0, 128 - GLA_RANK), (0, 0))).astype(BF16)

    oa_p, swa_k_p, swa_v_p = swa_prompt(z0, g_qnorm_a[0], g_knorm_a[0], sinks_a[0], bsz, t)
    ob_p, gla_p = gla_prompt(z0, glr, wg2, b_gla_gate2[0], g_gla_out[0], bsz, t)

    zd = z0[n_p:]
    kw_a = KVH_A * HD_A
    to_keys_last = lambda s: jnp.transpose(s, (0, 2, 3, 1)).reshape(dbs, kw_a, -1)
    from_keys_last = lambda s: jnp.transpose(s.reshape(dbs, KVH_A, HD_A, -1), (0, 3, 1, 2))[None]
    oa_d, swa_kt_d, swa_vt_d = swa_decode(
        zd[:, Z0_QA:Z0_QA + 1024].reshape(dbs, H_A, HD_A),
        zd[:, Z0_KA:Z0_KA + kw_a], zd[:, Z0_VA:Z0_VA + kw_a],
        to_keys_last(state_swa_k[0]), to_keys_last(state_swa_v[0]),
        g_qnorm_a[0], g_knorm_a[0], sinks_a[0])
    ob_d, gla_d = gla_decode(
        zd[:, Z0_QB:Z0_QB + 512], zd[:, Z0_KB:Z0_KB + 512], zd[:, Z0_VB:Z0_VB + 1024],
        zd[:, Z0_GOUT:Z0_GOUT + 1024], glr[n_p:], wg2, b_gla_gate2[0], g_gla_out[0], state_gla[0])

    wo = w_out_even[0].astype(BF16)
    h = proj_residual(x_p, oa_p, ob_p, oa_d.reshape(dbs, 1024), ob_d, wo, "out_proj_even",
                      h_tail=x_s)
    h = dense_tail(h, 0)

    z1 = norm_matmul(h, g_mix[1], w_in_odd[0].astype(BF16), 1024, "in_proj_odd")
    wa = w_rg_a[0].astype(BF16)
    wx = w_rg_x[0].astype(BF16)
    oc_p, conv_tail, lru_p = rglru_prompt(z1, conv_w[0], conv_b[0], wa, b_rg_a[0], wx, b_rg_x[0], lru_lambda[0], bsz, t)
    od_p, sb_k_p = sb_prompt(z1, g_qnorm_d[0], g_knorm_d[0], sb_bias[0], bsz, t)

    zd = z1[n_p:]
    oc_d, conv_d, lru_d = rglru_decode(
        zd[:, Z1_CX:Z1_CX + 1024], zd[:, Z1_CY:Z1_CY + 1024],
        jnp.swapaxes(state_conv[0], 0, 1), state_lru[0],
        conv_w[0], conv_b[0], wa, b_rg_a[0], wx, b_rg_x[0], lru_lambda[0])
    kvw = KVH_D * HD_D
    n_phys = cache_sb_k.shape[1]
    od_d, sb_k_d = sb_decode(
        zd[:, Z1_QD:Z1_QD + 1024].reshape(dbs, H_D, HD_D),
        zd[:, Z1_KD:Z1_KD + kvw].reshape(dbs, KVH_D, HD_D),
        zd[:, Z1_VD:Z1_VD + kvw].reshape(dbs, KVH_D, HD_D),
        g_qnorm_d[0], g_knorm_d[0], sb_bias[0],
        cache_sb_k[0].reshape(n_phys, PAGE * KVH_D, HD_D), cache_sb_v[0].reshape(n_phys, PAGE * KVH_D, HD_D),
        page_table)

    wo = w_out_odd[0].astype(BF16)
    h = proj_residual(h, oc_p, od_p, oc_d, od_d.reshape(dbs, 1024), wo, "out_proj_odd")
    y_prompt, y_sample = dense_tail(h, 1, n_tail=dbs)

    n_keep = min(WINDOW, t)
    return (
        y_prompt.reshape(bsz, t, d), y_sample.reshape(dbs, 1, d),
        swa_k_p.reshape(1, bsz, n_keep, KVH_A, HD_A), swa_v_p.reshape(1, bsz, n_keep, KVH_A, HD_A),
        gla_p[None],
        conv_tail[:, 8 - (CONV_W - 1):][None], lru_p.reshape(1, bsz, C_WIDTH),
        sb_k_p.reshape(1, bsz, t, KVH_D, HD_D), z1[:n_p, Z1_VD:].reshape(1, bsz, t, KVH_D, HD_D),
        from_keys_last(swa_kt_d), from_keys_last(swa_vt_d),
        gla_d[None],
        jnp.swapaxes(conv_d, 0, 1)[None], lru_d[None],
        sb_k_d.reshape(1, dbs, 1, KVH_D, HD_D), zd[:, Z1_VD:].reshape(1, dbs, 1, KVH_D, HD_D),
    )
```

```python
import functools

import jax
import jax.numpy as jnp
from jax import lax
from jax.experimental import pallas as pl
from jax.experimental.pallas import tpu as pltpu

F32 = jnp.float32
BF16 = jnp.bfloat16
EPS = 1e-6

V7X_VMEM_LIMIT_BYTES = 56 * 1024 * 1024

H_A, KVH_A, HD_A, WINDOW = 16, 4, 64, 128
H_B, DK_B, DV_B, GLA_RANK, GLA_GATE_NORM, GLA_CHUNK = 4, 128, 256, 16, 16.0, 64
C_WIDTH, C_BLOCKS, CONV_W, RG_C = 1024, 8, 4, 8.0
C_BLOCK = C_WIDTH // C_BLOCKS
H_D, KVH_D, HD_D = 8, 4, 128
PAGE = 128

ROW_TILE = 640
WIDE_ROW_TILE = 1040
SB_PROMPT_TILE = 256
SB_PAGES_PER_STEP = 16


def _cp(sem):
    return pltpu.CompilerParams(dimension_semantics=sem, vmem_limit_bytes=V7X_VMEM_LIMIT_BYTES)


def _rms_rows(x, g):
    r = lax.rsqrt(jnp.mean(x * x, axis=-1, keepdims=True) + EPS)
    return (x * r) * g


def _softplus(x):
    return jnp.maximum(x, 0.0) + jnp.log(1.0 + jnp.exp(-jnp.abs(x)))


def _log_sigmoid(x):
    return jnp.minimum(x, 0.0) - jnp.log(1.0 + jnp.exp(-jnp.abs(x)))


def _split_bf16(x):
    hi = x.astype(BF16)
    lo = (x - hi.astype(F32)).astype(BF16)
    return hi, lo


def _dot(a, b):
    return jnp.dot(a, b, preferred_element_type=F32)


def _dot_nt(a, b):
    return lax.dot_general(a, b, (((1,), (1,)), ((), ())), preferred_element_type=F32)


def _dot_tn(a, b):
    return lax.dot_general(a, b, (((0,), (0,)), ((), ())), preferred_element_type=F32)


def _norm_matmul_kernel(has_tail, has_side, x_ref, g_ref, w_ref, *rest):
    rest = list(rest)
    xt_ref = rest.pop(0) if has_tail else None
    ws_ref = rest.pop(0) if has_side else None
    o_ref = rest.pop(0)
    os_ref = rest.pop(0) if has_side else None
    xn_ref, = rest

    @pl.when(pl.program_id(1) == 0)
    def _():
        xn_ref[...] = _rms_rows(x_ref[...], g_ref[...]).astype(BF16)
        if has_tail:
            @pl.when(pl.program_id(0) == pl.num_programs(0) - 1)
            def _():
                nt = xt_ref.shape[0]
                xn_ref[xn_ref.shape[0] - nt:, :] = _rms_rows(xt_ref[...], g_ref[...]).astype(BF16)
        if has_side:
            os_ref[...] = _dot(xn_ref[...], ws_ref[...])

    o_ref[...] = _dot(xn_ref[...], w_ref[...])


def norm_matmul(x, g, w, tn, name, w_side=None, x_tail=None):
    d = x.shape[1]
    nt = 0 if x_tail is None else x_tail.shape[0]
    n = x.shape[0] + nt
    nout = w.shape[1]
    tm = WIDE_ROW_TILE
    assert n % tm == 0 and (nt == 0 or x.shape[0] % tm + nt == tm)
    in_specs = [
        pl.BlockSpec((tm, d), lambda i, j: (i, 0)),
        pl.BlockSpec((1, d), lambda i, j: (0, 0)),
        pl.BlockSpec((d, tn), lambda i, j: (0, j)),
    ]
    out_shape = [jax.ShapeDtypeStruct((n, nout), F32)]
    out_specs = [pl.BlockSpec((tm, tn), lambda i, j: (i, j))]
    args = [x, g.reshape(1, d), w]
    if x_tail is not None:
        in_specs.append(pl.BlockSpec((nt, d), lambda i, j: (0, 0)))
        args.append(x_tail)
    if w_side is not None:
        ns = w_side.shape[1]
        in_specs.append(pl.BlockSpec((d, ns), lambda i, j: (0, 0)))
        out_shape.append(jax.ShapeDtypeStruct((n, ns), F32))
        out_specs.append(pl.BlockSpec((tm, ns), lambda i, j: (i, 0)))
        args.append(w_side)
    out = pl.pallas_call(
        functools.partial(_norm_matmul_kernel, x_tail is not None, w_side is not None),
        out_shape=out_shape,
        grid=(n // tm, nout // tn),
        in_specs=in_specs,
        out_specs=out_specs,
        scratch_shapes=[pltpu.VMEM((tm, d), BF16)],
        compiler_params=_cp(("parallel", "arbitrary")),
        name=name,
    )(*args)
    return out if w_side is not None else out[0]


def _proj_res_kernel(h_ref, a_ref, b_ref, at_ref, bt_ref, wa_ref, wb_ref, *rest):
    ht_ref, o_ref = rest if len(rest) == 2 else (None, rest[0])
    tm = h_ref.shape[0]
    nt = at_ref.shape[0]
    o_ref[...] = h_ref[...] + (_dot(a_ref[...], wa_ref[...]) + _dot(b_ref[...], wb_ref[...]))

    @pl.when(pl.program_id(0) == pl.num_programs(0) - 1)
    def _():
        tail = _dot(at_ref[...], wa_ref[...]) + _dot(bt_ref[...], wb_ref[...])
        h_tail = h_ref[tm - nt:, :] if ht_ref is None else ht_ref[...]
        o_ref[tm - nt:, :] = h_tail + tail


def proj_residual(h, a, b, a_tail, b_tail, w, name, h_tail=None):
    d = h.shape[1]
    ka, kb = a.shape[1], b.shape[1]
    nt = a_tail.shape[0]
    n = a.shape[0] + nt
    tm = ROW_TILE
    assert h.shape[0] == (n if h_tail is None else a.shape[0]) and n % tm == 0 and a.shape[0] % tm + nt == tm
    assert ka == kb and w.shape[0] == ka + kb
    in_specs = [
        pl.BlockSpec((tm, d), lambda i: (i, 0)),
        pl.BlockSpec((tm, ka), lambda i: (i, 0)),
        pl.BlockSpec((tm, kb), lambda i: (i, 0)),
        pl.BlockSpec((nt, ka), lambda i: (0, 0)),
        pl.BlockSpec((nt, kb), lambda i: (0, 0)),
        pl.BlockSpec((ka, d), lambda i: (0, 0)),
        pl.BlockSpec((kb, d), lambda i: (1, 0)),
    ]
    args = [h, a, b, a_tail, b_tail, w, w]
    if h_tail is not None:
        in_specs.append(pl.BlockSpec((nt, d), lambda i: (0, 0)))
        args.append(h_tail)
    return pl.pallas_call(
        _proj_res_kernel,
        out_shape=jax.ShapeDtypeStruct((n, d), F32),
        grid=(n // tm,),
        in_specs=in_specs,
        out_specs=pl.BlockSpec((tm, d), lambda i: (i, 0)),
        compiler_params=_cp(("parallel",)),
        name=name,
    )(*args)


def _ffn_kernel(h_ref, g_ref, wg_ref, wu_ref, wd_ref, o_ref, hn_ref):
    @pl.when(pl.program_id(1) == 0)
    def _():
        h = h_ref[...]
        hn_ref[...] = _rms_rows(h, g_ref[...]).astype(BF16)
        o_ref[...] = h

    hn = hn_ref[...]
    gate = _dot(hn, wg_ref[...])
    up = _dot(hn, wu_ref[...])
    act = (gate * jax.nn.sigmoid(gate)) * up
    o_ref[...] += _dot(act.astype(BF16), wd_ref[...])


def ffn_residual(h, g, wg, wu, wd, layer, tf, name):
    n, d = h.shape
    dff = wg.shape[2]
    tm = WIDE_ROW_TILE
    return pl.pallas_call(
        _ffn_kernel,
        out_shape=jax.ShapeDtypeStruct((n, d), F32),
        grid=(n // tm, dff // tf),
        in_specs=[
            pl.BlockSpec((tm, d), lambda i, f: (i, 0)),
            pl.BlockSpec((1, d), lambda i, f: (0, 0)),
            pl.BlockSpec((None, d, tf), lambda i, f: (layer, 0, f)),
            pl.BlockSpec((None, d, tf), lambda i, f: (layer, 0, f)),
            pl.BlockSpec((None, tf, d), lambda i, f: (layer, f, 0)),
        ],
        out_specs=pl.BlockSpec((tm, d), lambda i, f: (i, 0)),
        scratch_shapes=[pltpu.VMEM((tm, d), BF16)],
        compiler_params=_cp(("parallel", "arbitrary")),
        name=name,
    )(h, g.reshape(1, d), wg, wu, wd)


def _ple_kernel(h_ref, g_ref, p_ref, wg_ref, wp_ref, o_ref, *tail_ref):
    h = h_ref[...]
    hn = _rms_rows(h, g_ref[...]).astype(BF16)
    gate = jax.nn.sigmoid(_dot(hn, wg_ref[...]))
    proj = _dot(p_ref[...].astype(BF16), wp_ref[...])
    y = h + gate * proj
    o_ref[...] = y
    if tail_ref:
        @pl.when(pl.program_id(0) == pl.num_programs(0) - 1)
        def _():
            nt = tail_ref[0].shape[0]
            tail_ref[0][...] = y[y.shape[0] - nt:, :]


def ple_residual(h, g, p, wg, wp, layer, name, n_tail=0):
    n, d = h.shape
    dp = p.shape[2]
    tm = ROW_TILE
    assert n % tm == 0 and (n_tail == 0 or (n - n_tail) % tm + n_tail == tm)
    out_shape = [jax.ShapeDtypeStruct((n - n_tail, d), F32)]
    out_specs = [pl.BlockSpec((tm, d), lambda i: (i, 0))]
    if n_tail:
        out_shape.append(jax.ShapeDtypeStruct((n_tail, d), F32))
        out_specs.append(pl.BlockSpec((n_tail, d), lambda i: (0, 0)))
    out = pl.pallas_call(
        _ple_kernel,
        out_shape=out_shape,
        grid=(n // tm,),
        in_specs=[
            pl.BlockSpec((tm, d), lambda i: (i, 0)),
            pl.BlockSpec((1, d), lambda i: (0, 0)),
            pl.BlockSpec((None, tm, dp), lambda i: (layer, i, 0)),
            pl.BlockSpec((None, d, d), lambda i: (layer, 0, 0)),
            pl.BlockSpec((None, dp, d), lambda i: (layer, 0, 0)),
        ],
        out_specs=out_specs,
        compiler_params=_cp(("arbitrary",)),
        name=name,
    )(h, g.reshape(1, d), p, wg, wp)
    return out if n_tail else out[0]


Z0_QA, Z0_KA, Z0_VA, Z0_QB, Z0_KB, Z0_VB, Z0_GOUT = 0, 1024, 1280, 1536, 2048, 2560, 3584
Z0_WIDTH = 4608


def _swa_prompt_kernel(sink_ref, q_ref, kvp_ref, kvc_ref, gq_ref, gk_ref, o_ref, ko_ref, vo_ref):
    n = pl.program_id(1)
    blk = q_ref.shape[0]
    grp = H_A // KVH_A
    kw = KVH_A * HD_A
    lanes = 2 * HD_A
    gq = gq_ref[...]
    gk = gk_ref[...]
    kv = jnp.concatenate([kvp_ref[...], kvc_ref[...]], axis=0)
    rows = lax.broadcasted_iota(jnp.int32, (grp * blk, 2 * blk), 0)
    cols = lax.broadcasted_iota(jnp.int32, (grp * blk, 2 * blk), 1)
    diff = (rows & (blk - 1)) - cols + blk
    lo = jnp.where(n == 0, blk, 0)
    mask = (diff >= 0) & (diff <= WINDOW) & (cols >= lo)

    r = lax.broadcasted_iota(jnp.int32, (lanes, lanes), 0)
    c = lax.broadcasted_iota(jnp.int32, (lanes, lanes), 1)
    head_sum = jnp.where(r // HD_A == c // HD_A, 1.0, 0.0).astype(BF16)
    spread = [jnp.where(r == (c & (HD_A - 1)) + half * HD_A, 1.0, 0.0).astype(BF16) for half in range(2)]
    one = jnp.ones((), BF16)
    low_half = lax.broadcasted_iota(jnp.int32, (blk, lanes), 1) < HD_A
    low_kv = lax.broadcasted_iota(jnp.int32, (2 * blk, lanes), 1) < HD_A

    def head_norm(x, g):
        hi, lo_ = _split_bf16(x * x)
        ss = _dot(hi, head_sum) + _dot(lo_, head_sum)
        return (x * lax.rsqrt(ss * (1.0 / HD_A) + EPS)) * g

    for pair in range(KVH_A // 2):
        ps = slice(pair * lanes, (pair + 1) * lanes)
        kn_pair = head_norm(kv[:, ps], gk)
        ko_ref[0, :, ps] = kn_pair[blk:]
        kn_b = kn_pair.astype(BF16)
        v_b = kv[:, kw + pair * lanes: kw + (pair + 1) * lanes].astype(BF16)
        for half in range(2):
            kh = 2 * pair + half
            kd = _dot(kn_b, spread[half]).astype(BF16)
            vd = _dot(v_b, spread[half]).astype(BF16)
            s_low, s_high = [], []
            for j in range(grp // 2):
                qs = slice(kh * grp * HD_A + j * lanes, kh * grp * HD_A + (j + 1) * lanes)
                qn = head_norm(q_ref[:, qs], gq)
                s_low.append(_dot_nt(jnp.where(low_half, qn, 0.0).astype(BF16), kd))
                s_high.append(_dot_nt(jnp.where(low_half, 0.0, qn).astype(BF16), kd))
            s = jnp.concatenate(s_low + s_high, axis=0) * (HD_A ** -0.5)
            s = jnp.where(mask, s, -jnp.inf)
            order = [2 * j for j in range(grp // 2)] + [2 * j + 1 for j in range(grp // 2)]
            sink = jnp.concatenate([jnp.full((blk, 1), sink_ref[kh * grp + g], F32) for g in order], axis=0)
            m = jnp.maximum(jnp.max(s, axis=-1, keepdims=True), sink)
            e = jnp.exp(s - m).astype(BF16)
            e_sink = jnp.exp(sink - m)
            nh = (grp // 2) * blk
            o_low = _dot(e[:nh], jnp.where(low_kv, vd, one))
            o_high = _dot(e[nh:], jnp.where(low_kv, one, vd))
            o_low = o_low / (pltpu.roll(o_low, HD_A, 1) + e_sink[:nh])
            o_high = o_high / (pltpu.roll(o_high, HD_A, 1) + e_sink[nh:])
            for j in range(grp // 2):
                o_pair = jnp.where(low_half, o_low[j * blk:(j + 1) * blk], o_high[j * blk:(j + 1) * blk])
                o_ref[:, kh * grp * HD_A + j * lanes: kh * grp * HD_A + (j + 1) * lanes] = o_pair.astype(BF16)
    vo_ref[0] = kvc_ref[:, kw:]


def swa_prompt(z0, gq, gk, sinks, bsz, t):
    blk = 128
    nb = t // blk
    kvw = 2 * KVH_A * HD_A
    kv_col = Z0_KA // kvw
    grid_spec = pltpu.PrefetchScalarGridSpec(
        num_scalar_prefetch=1,
        grid=(bsz, nb),
        in_specs=[
            pl.BlockSpec((blk, H_A * HD_A), lambda b, n, s: (b * nb + n, 0)),
            pl.BlockSpec((blk, kvw), lambda b, n, s: (b * nb + jnp.maximum(n - 1, 0), kv_col)),
            pl.BlockSpec((blk, kvw), lambda b, n, s: (b * nb + n, kv_col)),
            pl.BlockSpec((1, 2 * HD_A), lambda b, n, s: (0, 0)),
            pl.BlockSpec((1, 2 * HD_A), lambda b, n, s: (0, 0)),
        ],
        out_specs=[
            pl.BlockSpec((blk, H_A * HD_A), lambda b, n, s: (b * nb + n, 0)),
            pl.BlockSpec((1, blk, KVH_A * HD_A), lambda b, n, s: (b, 0, 0)),
            pl.BlockSpec((1, blk, KVH_A * HD_A), lambda b, n, s: (b, 0, 0)),
        ],
    )
    return pl.pallas_call(
        _swa_prompt_kernel,
        out_shape=[
            jax.ShapeDtypeStruct((bsz * t, H_A * HD_A), BF16),
            jax.ShapeDtypeStruct((bsz, blk, KVH_A * HD_A), F32),
            jax.ShapeDtypeStruct((bsz, blk, KVH_A * HD_A), F32),
        ],
        grid_spec=grid_spec,
        compiler_params=_cp(("parallel", "arbitrary")),
        name="swa_prompt",
    )(sinks, z0, z0, z0, jnp.tile(gq, 2).reshape(1, 2 * HD_A), jnp.tile(gk, 2).reshape(1, 2 * HD_A))


def _swa_decode_kernel(q_ref, k_ref, v_ref, kt_ref, vt_ref, gq_ref, gk_ref, sink_ref, o_ref, kto_ref, vto_ref):
    nb = q_ref.shape[0]
    grp = H_A // KVH_A
    kw = KVH_A * HD_A
    buf_len = kt_ref.shape[2]
    gq = gq_ref[...]
    gk = gk_ref[...]
    sink = sink_ref[...]
    head = lax.broadcasted_iota(jnp.int32, (H_A, kw), 0)
    lane_blk = lax.broadcasted_iota(jnp.int32, (H_A, kw), 1) // HD_A
    own = lane_blk == head // grp
    last = lax.broadcasted_iota(jnp.int32, (kw, buf_len), 1) == buf_len - 1
    kn_rows = jnp.concatenate(
        [_rms_rows(k_ref[:, kh * HD_A:(kh + 1) * HD_A], gk) for kh in range(KVH_A)], axis=-1)
    v_rows = v_ref[...]
    kn_cols = kn_rows.T
    v_cols = v_rows.T
    scale = HD_A ** -0.5
    qn = _rms_rows(q_ref[...].reshape(nb * H_A, HD_A), gq)
    own_all = jnp.concatenate([own] * nb, axis=0)
    qblk = jnp.where(own_all, jnp.concatenate([qn] * KVH_A, axis=-1), 0.0)
    qblk_b = qblk.astype(BF16)
    rows = [slice(i * H_A, (i + 1) * H_A) for i in range(nb)]
    s = jnp.concatenate([_dot(qblk_b[rows[i]], kt_ref[i].astype(BF16)) for i in range(nb)], axis=0) * scale
    k_new = jnp.concatenate([jnp.broadcast_to(kn_rows[i:i + 1], (H_A, kw)) for i in range(nb)], axis=0)
    v_new = jnp.concatenate([jnp.broadcast_to(v_rows[i:i + 1], (H_A, kw)) for i in range(nb)], axis=0)
    s_new = jnp.sum(qblk * k_new, axis=-1, keepdims=True) * scale
    sink_all = jnp.concatenate([sink] * nb, axis=0)
    m = jnp.maximum(jnp.maximum(jnp.max(s, axis=-1, keepdims=True), s_new), sink_all)
    e = jnp.exp(s - m)
    e_new = jnp.exp(s_new - m)
    den = jnp.sum(e, axis=-1, keepdims=True) + e_new + jnp.exp(sink_all - m)
    e_b = e.astype(BF16)
    o_all = jnp.concatenate([_dot_nt(e_b[rows[i]], vt_ref[i].astype(BF16)) for i in range(nb)], axis=0)
    o_all = jnp.where(own_all, o_all + e_new * v_new, 0.0)
    o = o_all[:, 0:HD_A]
    for kh in range(1, KVH_A):
        o = o + o_all[:, kh * HD_A:(kh + 1) * HD_A]
    o_ref[...] = (o / den).astype(BF16).reshape(nb, H_A, HD_A)
    for i in range(nb):
        kto_ref[i] = jnp.where(last, kn_cols[:, i:i + 1], pltpu.roll(kt_ref[i], buf_len - 1, 1))
        vto_ref[i] = jnp.where(last, v_cols[:, i:i + 1], pltpu.roll(vt_ref[i], buf_len - 1, 1))


def swa_decode(q, k, v, kt, vt, gq, gk, sinks):
    bsz, kw, buf_len = kt.shape
    nb = 8
    rows = pl.BlockSpec((nb, kw), lambda i: (i, 0))
    heads = pl.BlockSpec((nb, H_A, HD_A), lambda i: (i, 0, 0))
    buf = pl.BlockSpec((nb, kw, buf_len), lambda i: (i, 0, 0))
    vec = pl.BlockSpec((1, HD_A), lambda i: (0, 0))
    return pl.pallas_call(
        _swa_decode_kernel,
        out_shape=[
            jax.ShapeDtypeStruct((bsz, H_A, HD_A), BF16),
            jax.ShapeDtypeStruct(kt.shape, F32),
            jax.ShapeDtypeStruct(vt.shape, F32),
        ],
        grid=(bsz // nb,),
        in_specs=[heads, rows, rows, buf, buf, vec, vec, pl.BlockSpec((H_A, 1), lambda i: (0, 0))],
        out_specs=[heads, buf, buf],
        compiler_params=_cp(("parallel",)),
        name="swa_decode",
    )(q, k, v, kt, vt, gq.reshape(1, HD_A), gk.reshape(1, HD_A), sinks.reshape(H_A, 1))


def _gla_logdec(glr, wg2_ref, bg2_ref):
    gl = _dot(glr.astype(BF16), wg2_ref[...]) + bg2_ref[...]
    return _log_sigmoid(gl) * (1.0 / GLA_GATE_NORM)


def _col(row8, i):
    return row8.T[:, i:i + 1]


def _gla_out(o, g_go, gout):
    return (_rms_rows(o, g_go) * (gout * jax.nn.sigmoid(gout))).astype(BF16)


def _gla_prompt_kernel(q_ref, k_ref, v01_ref, v23_ref, gout01_ref, gout23_ref, glr_ref, wg2_ref, bg2_ref, ggo_ref,
                       o_ref, so_ref, s_ref):
    c_idx = pl.program_id(1)
    tb = q_ref.shape[0]
    ck = GLA_CHUNK

    @pl.when(c_idx == 0)
    def _():
        s_ref[...] = jnp.zeros_like(s_ref)

    nck = tb // ck
    ld = _gla_logdec(glr_ref[...], wg2_ref, bg2_ref)
    ti = lax.broadcasted_iota(jnp.int32, (tb, tb), 0)
    si = lax.broadcasted_iota(jnp.int32, (tb, tb), 1)
    causal = (si <= ti) & (si // ck == ti // ck)
    tri = jnp.where(causal, 1.0, 0.0).astype(BF16)
    hi, lo = _split_bf16(ld)
    b = _dot(tri, hi) + _dot(tri, lo)
    last_rows = [b[(c + 1) * ck - 1:(c + 1) * ck, :] for c in range(nck)]
    b_last = jnp.concatenate([jnp.broadcast_to(r, (ck, r.shape[1])) for r in last_rows], axis=0)
    k = k_ref[...]
    qt = ((q_ref[...] * (DK_B ** -0.5)) * jnp.exp(b)).astype(BF16)
    kt = (k * jnp.exp(-b)).astype(BF16)
    kd = (k * jnp.exp(b_last - b)).astype(BF16)
    g_go = ggo_ref[...]
    for h in range(H_B):
        ks = slice(h * DK_B, (h + 1) * DK_B)
        vs = slice(h * DV_B, (h + 1) * DV_B)
        hs = slice((h % 2) * DV_B, (h % 2 + 1) * DV_B)
        v = (v01_ref, v23_ref)[h // 2][:, hs].astype(BF16)
        att = jnp.where(causal, _dot_nt(qt[:, ks], kt[:, ks]), 0.0)
        o = _dot(att.astype(BF16), v)
        state = s_ref[h]
        o_state = []
        for c in range(nck):
            rs = slice(c * ck, (c + 1) * ck)
            o_state.append(_dot(qt[rs, ks], state.astype(BF16)))
            d_state = _dot_tn(kd[rs, ks], v[rs])
            decay = _col(jnp.broadcast_to(jnp.exp(last_rows[c][:, ks]), (8, DK_B)), 0)
            state = decay * state + d_state
        s_ref[h] = state
        o = o + jnp.concatenate(o_state, axis=0)
        o_ref[:, vs] = _gla_out(o, g_go, (gout01_ref, gout23_ref)[h // 2][:, hs])
    so_ref[0] = s_ref[...]


def gla_prompt(z0, glr, wg2, bg2, g_go, bsz, t):
    tb = 256
    nt = t // tb
    qk_w = H_B * DK_B
    v_w = H_B * DV_B
    return pl.pallas_call(
        _gla_prompt_kernel,
        out_shape=[
            jax.ShapeDtypeStruct((bsz * t, v_w), BF16),
            jax.ShapeDtypeStruct((bsz, H_B, DK_B, DV_B), F32),
        ],
        grid=(bsz, nt),
        in_specs=[
            pl.BlockSpec((tb, qk_w), lambda b, c: (b * nt + c, Z0_QB // qk_w)),
            pl.BlockSpec((tb, qk_w), lambda b, c: (b * nt + c, Z0_KB // qk_w)),
            pl.BlockSpec((tb, v_w // 2), lambda b, c: (b * nt + c, Z0_VB // (v_w // 2))),
            pl.BlockSpec((tb, v_w // 2), lambda b, c: (b * nt + c, Z0_VB // (v_w // 2) + 1)),
            pl.BlockSpec((tb, v_w // 2), lambda b, c: (b * nt + c, Z0_GOUT // (v_w // 2))),
            pl.BlockSpec((tb, v_w // 2), lambda b, c: (b * nt + c, Z0_GOUT // (v_w // 2) + 1)),
            pl.BlockSpec((tb, 128), lambda b, c: (b * nt + c, 0)),
            pl.BlockSpec((128, qk_w), lambda b, c: (0, 0)),
            pl.BlockSpec((1, qk_w), lambda b, c: (0, 0)),
            pl.BlockSpec((1, DV_B), lambda b, c: (0, 0)),
        ],
        out_specs=[
            pl.BlockSpec((tb, v_w), lambda b, c: (b * nt + c, 0)),
            pl.BlockSpec((1, H_B, DK_B, DV_B), lambda b, c: (b, 0, 0, 0)),
        ],
        scratch_shapes=[pltpu.VMEM((H_B, DK_B, DV_B), F32)],
        compiler_params=_cp(("parallel", "arbitrary")),
        name="gla_prompt",
    )(z0, z0, z0, z0, z0, z0, glr, wg2, bg2.reshape(1, qk_w), g_go.reshape(1, DV_B))


def _gla_decode_kernel(q_ref, k_ref, v_ref, gout_ref, glr_ref, wg2_ref, bg2_ref, ggo_ref, s_ref, o_ref, so_ref):
    nb = q_ref.shape[0]
    ld = _gla_logdec(glr_ref[...], wg2_ref, bg2_ref)
    g_go = ggo_ref[...]
    for h in range(H_B):
        ks = slice(h * DK_B, (h + 1) * DK_B)
        vs = slice(h * DV_B, (h + 1) * DV_B)
        eg_t = jnp.exp(ld[:, ks]).T
        k_t = k_ref[:, ks].T
        q_b = (q_ref[:, ks] * (DK_B ** -0.5)).astype(BF16)
        outs = []
        for i in range(nb):
            state = eg_t[:, i:i + 1] * s_ref[i, h] + k_t[:, i:i + 1] * v_ref[i:i + 1, vs]
            so_ref[i, h] = state
            outs.append(_dot(q_b, state.astype(BF16))[i:i + 1])
        o = jnp.concatenate(outs, axis=0)
        o_ref[:, vs] = _gla_out(o, g_go, gout_ref[:, vs])


def gla_decode(q, k, v, gout, glr, wg2, bg2, g_go, state):
    bsz = q.shape[0]
    nb = 8
    qk_w = H_B * DK_B
    v_w = H_B * DV_B
    rows = lambda w: pl.BlockSpec((nb, w), lambda i: (i, 0))
    st = pl.BlockSpec((nb, H_B, DK_B, DV_B), lambda i: (i, 0, 0, 0))
    return pl.pallas_call(
        _gla_decode_kernel,
        out_shape=[
            jax.ShapeDtypeStruct((bsz, v_w), BF16),
            jax.ShapeDtypeStruct(state.shape, F32),
        ],
        grid=(bsz // nb,),
        in_specs=[
            rows(qk_w), rows(qk_w), rows(v_w), rows(v_w), rows(128),
            pl.BlockSpec((128, qk_w), lambda i: (0, 0)),
            pl.BlockSpec((1, qk_w), lambda i: (0, 0)),
            pl.BlockSpec((1, DV_B), lambda i: (0, 0)),
            st,
        ],
        out_specs=[rows(v_w), st],
        compiler_params=_cp(("parallel",)),
        name="gla_decode",
    )(q, k, v, gout, glr, wg2, bg2.reshape(1, qk_w), g_go.reshape(1, DV_B), state)


Z1_CX, Z1_CY, Z1_QD, Z1_KD, Z1_VD = 0, 1024, 2048, 3072, 3584


def _rg_gates(xc, wa_ref, ba_ref, wx_ref, bx_ref, lam_ref):
    xb = xc.astype(BF16)
    ra = jnp.concatenate(
        [_dot(xb[:, n * C_BLOCK:(n + 1) * C_BLOCK], wa_ref[n]) for n in range(C_BLOCKS)], axis=-1) + ba_ref[...]
    rx = jnp.concatenate(
        [_dot(xb[:, n * C_BLOCK:(n + 1) * C_BLOCK], wx_ref[n]) for n in range(C_BLOCKS)], axis=-1) + bx_ref[...]
    r_gate = jax.nn.sigmoid(ra)
    i_gate = jax.nn.sigmoid(rx)
    log_a = (-RG_C * r_gate) * _softplus(-lam_ref[...])
    a = jnp.exp(log_a)
    one_minus_a2 = -jnp.tanh(log_a) * (a * a + 1.0)
    u = jnp.sqrt(one_minus_a2) * (i_gate * xc)
    return a, u


def _rglru_prompt_kernel(cx_ref, cy_ref, cw_ref, cb_ref, wa_ref, ba_ref, wx_ref, bx_ref, lam_ref,
                         o_ref, tail_ref, hl_ref, prev_ref, hc_ref):
    t_idx = pl.program_id(1)
    tt = cx_ref.shape[0]

    @pl.when(t_idx == 0)
    def _():
        prev_ref[...] = jnp.zeros_like(prev_ref)
        hc_ref[...] = jnp.zeros_like(hc_ref)

    x = cx_ref[...]
    xp = jnp.concatenate([prev_ref[...], x], axis=0)
    xc = cb_ref[...] + cw_ref[CONV_W - 1:CONV_W, :] * x
    for j in range(CONV_W - 1):
        d = CONV_W - 1 - j
        xc = xc + cw_ref[j:j + 1, :] * pltpu.roll(xp, d, 0)[8:]
    prev_ref[...] = x[tt - 8:]
    tail_ref[0] = x[tt - 8:]

    a, u = _rg_gates(xc, wa_ref, ba_ref, wx_ref, bx_ref, lam_ref)
    sub = lax.broadcasted_iota(jnp.int32, a.shape, 0) & 7
    for d in (1, 2, 4):
        inside = sub >= d
        u = a * jnp.where(inside, pltpu.roll(u, d, 0), 0.0) + u
        a = a * jnp.where(inside, pltpu.roll(a, d, 0), 1.0)
    h_prev = hc_ref[...]
    tiles = []
    for r in range(0, tt, 8):
        tiles.append(a[r:r + 8] * h_prev + u[r:r + 8])
        h_prev = tiles[-1][7:8]
    h = jnp.concatenate(tiles, axis=0)
    hc_ref[...] = h[tt - 1:]
    hl_ref[0] = h[tt - 1:]
    o_ref[...] = (h * jax.nn.gelu(cy_ref[...])).astype(BF16)


def rglru_prompt(z1, conv_w, conv_b, wa, ba, wx, bx, lam, bsz, t):
    tt = 256
    nt = t // tt
    c = C_WIDTH
    vec = pl.BlockSpec((1, c), lambda b, i: (0, 0))
    wblk = pl.BlockSpec((C_BLOCKS, C_BLOCK, C_BLOCK), lambda b, i: (0, 0, 0))
    return pl.pallas_call(
        _rglru_prompt_kernel,
        out_shape=[
            jax.ShapeDtypeStruct((bsz * t, c), BF16),
            jax.ShapeDtypeStruct((bsz, 8, c), F32),
            jax.ShapeDtypeStruct((bsz, 1, c), F32),
        ],
        grid=(bsz, nt),
        in_specs=[
            pl.BlockSpec((tt, c), lambda b, i: (b * nt + i, Z1_CX // c)),
            pl.BlockSpec((tt, c), lambda b, i: (b * nt + i, Z1_CY // c)),
            pl.BlockSpec((CONV_W, c), lambda b, i: (0, 0)),
            vec, wblk, vec, wblk, vec, vec,
        ],
        out_specs=[
            pl.BlockSpec((tt, c), lambda b, i: (b * nt + i, 0)),
            pl.BlockSpec((1, 8, c), lambda b, i: (b, 0, 0)),
            pl.BlockSpec((1, 1, c), lambda b, i: (b, 0, 0)),
        ],
        scratch_shapes=[pltpu.VMEM((8, c), F32), pltpu.VMEM((1, c), F32)],
        compiler_params=_cp(("parallel", "arbitrary")),
        name="rglru_prompt",
    )(z1, z1, conv_w, conv_b.reshape(1, c), wa, ba.reshape(1, c), wx, bx.reshape(1, c), lam.reshape(1, c))


def _rglru_decode_kernel(cx_ref, cy_ref, buf_ref, h0_ref, cw_ref, cb_ref, wa_ref, ba_ref, wx_ref, bx_ref, lam_ref,
                         o_ref, nbuf_ref, hl_ref):
    x = cx_ref[...]
    xc = cb_ref[...] + cw_ref[CONV_W - 1:CONV_W, :] * x
    for j in range(CONV_W - 1):
        xc = xc + cw_ref[j:j + 1, :] * buf_ref[j]
    for j in range(CONV_W - 2):
        nbuf_ref[j] = buf_ref[j + 1]
    nbuf_ref[CONV_W - 2] = x
    a, u = _rg_gates(xc, wa_ref, ba_ref, wx_ref, bx_ref, lam_ref)
    h = a * h0_ref[...] + u
    hl_ref[...] = h
    o_ref[...] = (h * jax.nn.gelu(cy_ref[...])).astype(BF16)


def rglru_decode(cx, cy, buf, h0, conv_w, conv_b, wa, ba, wx, bx, lam):
    bsz, c = cx.shape
    return pl.pallas_call(
        _rglru_decode_kernel,
        out_shape=[
            jax.ShapeDtypeStruct((bsz, c), BF16),
            jax.ShapeDtypeStruct(buf.shape, F32),
            jax.ShapeDtypeStruct((bsz, c), F32),
        ],
        compiler_params=pltpu.CompilerParams(vmem_limit_bytes=V7X_VMEM_LIMIT_BYTES),
        name="rglru_decode",
    )(cx, cy, buf, h0, conv_w, conv_b.reshape(1, c), wa, ba.reshape(1, c), wx, bx.reshape(1, c), lam.reshape(1, c))


def _strict_upper_ones(n):
    j = lax.broadcasted_iota(jnp.int32, (n, n), 0)
    s = lax.broadcasted_iota(jnp.int32, (n, n), 1)
    return jnp.where(j > s, 1.0, 0.0).astype(BF16)


def _from_here_ones(n):
    j = lax.broadcasted_iota(jnp.int32, (n, n), 0)
    s = lax.broadcasted_iota(jnp.int32, (n, n), 1)
    return jnp.where(j >= s, 1.0, 0.0).astype(BF16)


def _sb_tile(z, from_here, dead, mask):
    sp = _softplus(z)
    if mask is not None:
        sp = jnp.where(mask, sp, 0.0)
    hi, lo = _split_bf16(sp)
    cum = _dot(hi, from_here) + _dot(lo, from_here)
    w = jnp.exp(z - cum - dead)
    if mask is not None:
        w = jnp.where(mask, w, 0.0)
    return w, dead + cum[:, 0:1]


def _sb_prompt_kernel(bias_ref, q_ref, k_ref, v_ref, gq_ref, gk_ref, o_ref, ko_ref, vo_ref, kb_ref, vb_ref):
    kh = pl.program_id(1)
    i = pl.program_id(2)
    blk = q_ref.shape[0]
    grp = H_D // KVH_D

    @pl.when(i == 0)
    def _():
        kn = _rms_rows(k_ref[...], gk_ref[...])
        v = v_ref[...]
        for k in range(KVH_D):
            @pl.when(kh == k)
            def _():
                ko_ref[:, k, :] = kn
                vo_ref[:, k, :] = v
        kb_ref[...] = kn.astype(BF16)
        vb_ref[...] = v.astype(BF16)

    gq = gq_ref[...]
    qs = jnp.concatenate(
        [_rms_rows(q_ref[:, g * HD_D:(g + 1) * HD_D], gq) for g in range(grp)], axis=0).astype(BF16)
    bias = jnp.concatenate(
        [jnp.full((blk, 1), bias_ref[kh * grp + g], F32) for g in range(grp)], axis=0)
    from_here = _from_here_ones(blk)
    scale = HD_D ** -0.5

    def tile(j, mask, acc, dead):
        off = pl.multiple_of(j * blk, blk)
        z = _dot_nt(qs, kb_ref[pl.ds(off, blk), :]) * scale + bias
        w, dead = _sb_tile(z, from_here, dead, mask)
        return acc + _dot(w.astype(BF16), vb_ref[pl.ds(off, blk), :]), dead

    rows = lax.broadcasted_iota(jnp.int32, (grp * blk, blk), 0) & (blk - 1)
    cols = lax.broadcasted_iota(jnp.int32, (grp * blk, blk), 1)
    acc, dead = tile(i, cols < rows, jnp.zeros((grp * blk, HD_D), F32), jnp.zeros((grp * blk, 1), F32))

    def pair(step, carry):
        j = i - 1 - 2 * step
        return tile(j - 1, None, *tile(j, None, *carry))

    def last(step, carry):
        return tile(0, None, *carry)

    n_pairs = i // 2
    carry = lax.fori_loop(0, n_pairs, pair, (acc, dead))
    acc, _ = lax.fori_loop(0, i - 2 * n_pairs, last, carry)
    for g in range(grp):
        o_ref[:, g * HD_D:(g + 1) * HD_D] = acc[g * blk:(g + 1) * blk].astype(BF16)


def sb_prompt(z1, gq, gk, bias, bsz, t):
    blk = SB_PROMPT_TILE
    nb = t // blk
    grp = H_D // KVH_D
    qw = grp * HD_D
    grid_spec = pltpu.PrefetchScalarGridSpec(
        num_scalar_prefetch=1,
        grid=(bsz, KVH_D, nb),
        in_specs=[
            pl.BlockSpec((blk, qw), lambda b, k, i, s: (b * nb + i, Z1_QD // qw + k)),
            pl.BlockSpec((t, HD_D), lambda b, k, i, s: (b, Z1_KD // HD_D + k)),
            pl.BlockSpec((t, HD_D), lambda b, k, i, s: (b, Z1_VD // HD_D + k)),
            pl.BlockSpec((1, HD_D), lambda b, k, i, s: (0, 0)),
            pl.BlockSpec((1, HD_D), lambda b, k, i, s: (0, 0)),
        ],
        out_specs=[
            pl.BlockSpec((blk, qw), lambda b, k, i, s: (b * nb + i, k)),
            pl.BlockSpec((t, KVH_D, HD_D), lambda b, k, i, s: (b, 0, 0)),
            pl.BlockSpec((t, KVH_D, HD_D), lambda b, k, i, s: (b, 0, 0)),
        ],
        scratch_shapes=[pltpu.VMEM((t, HD_D), BF16), pltpu.VMEM((t, HD_D), BF16)],
    )
    return pl.pallas_call(
        _sb_prompt_kernel,
        out_shape=[
            jax.ShapeDtypeStruct((bsz * t, H_D * HD_D), BF16),
            jax.ShapeDtypeStruct((bsz * t, KVH_D, HD_D), F32),
            jax.ShapeDtypeStruct((bsz * t, KVH_D, HD_D), F32),
        ],
        grid_spec=grid_spec,
        compiler_params=_cp(("parallel", "arbitrary", "arbitrary")),
        name="sb_prompt",
    )(bias, z1, z1, z1, gq.reshape(1, HD_D), gk.reshape(1, HD_D))


def _sb_decode_kernel(n_pages, pt_ref, bias_ref, q_ref, k_ref, v_ref, gq_ref, gk_ref, ck_hbm, cv_hbm,
                      o_ref, ko_ref, kbuf, vbuf, sem):
    b = pl.program_id(0)
    nb = pl.num_programs(0)
    npg = SB_PAGES_PER_STEP
    n_chunks = n_pages // npg
    prow = PAGE * KVH_D
    grp = H_D // KVH_D
    scale = HD_D ** -0.5

    def page_copies(bb, chunk, slot):
        cps = []
        for p in range(npg):
            page = pt_ref[bb * n_pages + chunk * npg + p]
            dst = pl.ds(p * prow, prow)
            cps.append(pltpu.make_async_copy(ck_hbm.at[page], kbuf.at[slot, dst], sem.at[0, slot]))
            cps.append(pltpu.make_async_copy(cv_hbm.at[page], vbuf.at[slot, dst], sem.at[1, slot]))
        return cps

    @pl.when(b == 0)
    def _():
        for cp in page_copies(0, n_chunks - 1, 0):
            cp.start()

    bias = jnp.concatenate([jnp.full((1, 1), bias_ref[h], F32) for h in range(H_D)], axis=0)
    qn = _rms_rows(q_ref[0], gq_ref[...])
    qb = qn.astype(BF16)
    kn = _rms_rows(k_ref[0], gk_ref[...])
    ko_ref[0] = kn
    kn_sel = jnp.concatenate([kn[h // grp:h // grp + 1] for h in range(H_D)], axis=0)
    v_sel = jnp.concatenate([v_ref[0, h // grp:h // grp + 1] for h in range(H_D)], axis=0)
    z0 = jnp.sum(qn * kn_sel, axis=-1, keepdims=True) * scale + bias
    visible = jnp.zeros((H_D, 1), jnp.int32) < jnp.zeros((H_D, 1), jnp.int32)
    sp0 = _softplus(z0)
    acc = jnp.where(visible, jnp.exp(z0 - sp0), 0.0) * v_sel
    surv = jnp.where(visible, -sp0, 0.0)

    later = _strict_upper_ones(prow)
    row = lax.broadcasted_iota(jnp.int32, (npg * H_D, prow), 0)
    col = lax.broadcasted_iota(jnp.int32, (npg * H_D, prow), 1)
    valid = (col & (KVH_D - 1)) == ((row & (H_D - 1)) // grp)
    bias_r = jnp.concatenate([bias] * npg, axis=0)

    for i in range(n_chunks):
        slot = i % 2
        if i + 1 < n_chunks:
            for cp in page_copies(b, n_chunks - 2 - i, 1 - slot):
                cp.start()
        else:
            @pl.when(b + 1 < nb)
            def _():
                for cp in page_copies(b + 1, n_chunks - 1, 1 - slot):
                    cp.start()
        for cp in page_copies(b, n_chunks - 1 - i, slot):
            cp.wait()

        z = _dot_nt(qb, kbuf[slot].astype(BF16))
        z = jnp.concatenate([z[:, p * prow:(p + 1) * prow] for p in range(npg)], axis=0) * scale + bias_r
        sp = _softplus(z)
        l1m = jnp.where(valid, -sp, 0.0)
        hi, lo = _split_bf16(l1m)
        suffix = _dot(hi, later) + _dot(lo, later)
        tot = jnp.sum(l1m, axis=-1, keepdims=True)
        survs = [None] * npg
        for p in reversed(range(npg)):
            survs[p] = surv
            surv = surv + tot[p * H_D:(p + 1) * H_D]
        w = jnp.where(valid, jnp.exp((z - sp) + suffix + jnp.concatenate(survs, axis=0)), 0.0)
        w = jnp.concatenate([w[p * H_D:(p + 1) * H_D] for p in range(npg)], axis=1).astype(BF16)
        acc = acc + _dot(w, vbuf[slot].astype(BF16))

    o_ref[0] = acc.astype(BF16)


def sb_decode(q, k, v, gq, gk, bias, cache_k, cache_v, page_table):
    bsz = q.shape[0]
    n_pages = page_table.shape[1]
    npg = SB_PAGES_PER_STEP
    assert n_pages % (2 * npg) == 0
    prow = PAGE * KVH_D
    grid_spec = pltpu.PrefetchScalarGridSpec(
        num_scalar_prefetch=2,
        grid=(bsz,),
        in_specs=[
            pl.BlockSpec((1, H_D, HD_D), lambda b, pt, bias_: (b, 0, 0)),
            pl.BlockSpec((1, KVH_D, HD_D), lambda b, pt, bias_: (b, 0, 0)),
            pl.BlockSpec((1, KVH_D, HD_D), lambda b, pt, bias_: (b, 0, 0)),
            pl.BlockSpec((1, HD_D), lambda b, pt, bias_: (0, 0)),
            pl.BlockSpec((1, HD_D), lambda b, pt, bias_: (0, 0)),
            pl.BlockSpec(memory_space=pl.ANY),
            pl.BlockSpec(memory_space=pl.ANY),
        ],
        out_specs=[
            pl.BlockSpec((1, H_D, HD_D), lambda b, pt, bias_: (b, 0, 0)),
            pl.BlockSpec((1, KVH_D, HD_D), lambda b, pt, bias_: (b, 0, 0)),
        ],
        scratch_shapes=[
            pltpu.VMEM((2, npg * prow, HD_D), F32),
            pltpu.VMEM((2, npg * prow, HD_D), F32),
            pltpu.SemaphoreType.DMA((2, 2)),
        ],
    )
    return pl.pallas_call(
        functools.partial(_sb_decode_kernel, n_pages),
        out_shape=[
            jax.ShapeDtypeStruct((bsz, H_D, HD_D), BF16),
            jax.ShapeDtypeStruct((bsz, KVH_D, HD_D), F32),
        ],
        grid_spec=grid_spec,
        compiler_params=_cp(("arbitrary",)),
        name="sb_decode",
    )(page_table.reshape(-1), bias, q, k, v, gq.reshape(1, HD_D), gk.reshape(1, HD_D), cache_k, cache_v)


def kernel(x_prompt, x_sample, state_swa_k, state_swa_v, state_gla, state_conv, state_lru, cache_sb_k, cache_sb_v, page_table, p_prompt, p_sample, g_mix, g_ffn, w_ffn_gate, w_ffn_up, w_ffn_down, g_ple, w_ple_gate, w_ple_proj, w_in_even, g_qnorm_a, g_knorm_a, sinks_a, w_gla_gate2, b_gla_gate2, g_gla_out, w_out_even, w_in_odd, conv_w, conv_b, w_rg_a, b_rg_a, w_rg_x, b_rg_x, lru_lambda, g_qnorm_d, g_knorm_d, sb_bias, w_out_odd):
    bsz, t, d = x_prompt.shape
    dbs = x_sample.shape[0]
    n_p = bsz * t
    depth = g_mix.shape[0]
    ple = p_prompt.shape[-1]

    x_p = x_prompt.reshape(n_p, d)
    x_s = x_sample.reshape(dbs, d)
    p_all = jnp.concatenate([p_prompt.reshape(depth, n_p, ple), p_sample.reshape(depth, dbs, ple)], axis=1)

    ffn_w = (w_ffn_gate.astype(BF16), w_ffn_up.astype(BF16), w_ffn_down.astype(BF16))
    ple_w = (w_ple_gate.astype(BF16), w_ple_proj.astype(BF16))

    def dense_tail(h, i, n_tail=0):
        h = ffn_residual(h, g_ffn[i], *ffn_w, i, 512, f"ffn_{i}")
        return ple_residual(h, g_ple[i], p_all, *ple_w, i, f"ple_{i}", n_tail)

    w = w_in_even[0]
    glr_col = Z0_GOUT
    w0 = jnp.concatenate([w[:, :glr_col].astype(BF16), w[:, glr_col + GLA_RANK:].astype(BF16)], axis=1)
    w0_glr = jnp.pad(w[:, glr_col:glr_col + GLA_RANK], ((0, 0), (0, 128 - GLA_RANK))).astype(BF16)
    z0, glr = norm_matmul(x_p, g_mix[0], w0, 1536, "in_proj_even", w_side=w0_glr, x_tail=x_s)
    wg2 = jnp.pad(w_gla_gate2[0], ((0, 128 - GLA_RANK), (0, 0))).astype(BF16)

    oa_p, swa_k_p, swa_v_p = swa_prompt(z0, g_qnorm_a[0], g_knorm_a[0], sinks_a[0], bsz, t)
    ob_p, gla_p = gla_prompt(z0, glr, wg2, b_gla_gate2[0], g_gla_out[0], bsz, t)

    zd = z0[n_p:]
    kw_a = KVH_A * HD_A
    to_keys_last = lambda s: jnp.transpose(s, (0, 2, 3, 1)).reshape(dbs, kw_a, -1)
    from_keys_last = lambda s: jnp.transpose(s.reshape(dbs, KVH_A, HD_A, -1), (0, 3, 1, 2))[None]
    oa_d, swa_kt_d, swa_vt_d = swa_decode(
        zd[:, Z0_QA:Z0_QA + 1024].reshape(dbs, H_A, HD_A),
        zd[:, Z0_KA:Z0_KA + kw_a], zd[:, Z0_VA:Z0_VA + kw_a],
        to_keys_last(state_swa_k[0]), to_keys_last(state_swa_v[0]),
        g_qnorm_a[0], g_knorm_a[0], sinks_a[0])
    ob_d, gla_d = gla_decode(
        zd[:, Z0_QB:Z0_QB + 512], zd[:, Z0_KB:Z0_KB + 512], zd[:, Z0_VB:Z0_VB + 1024],
        zd[:, Z0_GOUT:Z0_GOUT + 1024], glr[n_p:], wg2, b_gla_gate2[0], g_gla_out[0], state_gla[0])

    wo = w_out_even[0].astype(BF16)
    h = proj_residual(x_p, oa_p, ob_p, oa_d.reshape(dbs, 1024), ob_d, wo, "out_proj_even",
                      h_tail=x_s)
    h = dense_tail(h, 0)

    z1 = norm_matmul(h, g_mix[1], w_in_odd[0].astype(BF16), 1024, "in_proj_odd")
    wa = w_rg_a[0].astype(BF16)
    wx = w_rg_x[0].astype(BF16)
    oc_p, conv_tail, lru_p = rglru_prompt(z1, conv_w[0], conv_b[0], wa, b_rg_a[0], wx, b_rg_x[0], lru_lambda[0], bsz, t)
    od_p, sb_k_p, sb_v_p = sb_prompt(z1, g_qnorm_d[0], g_knorm_d[0], sb_bias[0], bsz, t)

    zd = z1[n_p:]
    oc_d, conv_d, lru_d = rglru_decode(
        zd[:, Z1_CX:Z1_CX + 1024], zd[:, Z1_CY:Z1_CY + 1024],
        jnp.swapaxes(state_conv[0], 0, 1), state_lru[0],
        conv_w[0], conv_b[0], wa, b_rg_a[0], wx, b_rg_x[0], lru_lambda[0])
    kvw = KVH_D * HD_D
    n_phys = cache_sb_k.shape[1]
    od_d, sb_k_d = sb_decode(
        zd[:, Z1_QD:Z1_QD + 1024].reshape(dbs, H_D, HD_D),
        zd[:, Z1_KD:Z1_KD + kvw].reshape(dbs, KVH_D, HD_D),
        zd[:, Z1_VD:Z1_VD + kvw].reshape(dbs, KVH_D, HD_D),
        g_qnorm_d[0], g_knorm_d[0], sb_bias[0],
        cache_sb_k[0].reshape(n_phys, PAGE * KVH_D, HD_D), cache_sb_v[0].reshape(n_phys, PAGE * KVH_D, HD_D),
        page_table)

    wo = w_out_odd[0].astype(BF16)
    h = proj_residual(h, oc_p, od_p, oc_d, od_d.reshape(dbs, 1024), wo, "out_proj_odd")
    y_prompt, y_sample = dense_tail(h, 1, n_tail=dbs)

    n_keep = min(WINDOW, t)
    return (
        y_prompt.reshape(bsz, t, d), y_sample.reshape(dbs, 1, d),
        swa_k_p.reshape(1, bsz, n_keep, KVH_A, HD_A), swa_v_p.reshape(1, bsz, n_keep, KVH_A, HD_A),
        gla_p[None],
        conv_tail[:, 8 - (CONV_W - 1):][None], lru_p.reshape(1, bsz, C_WIDTH),
        sb_k_p.reshape(1, bsz, t, KVH_D, HD_D), sb_v_p.reshape(1, bsz, t, KVH_D, HD_D),
        from_keys_last(swa_kt_d), from_keys_last(swa_vt_d),
        gla_d[None],
        jnp.swapaxes(conv_d, 0, 1)[None], lru_d[None],
        sb_k_d.reshape(1, dbs, 1, KVH_D, HD_D), zd[:, Z1_VD:].reshape(1, dbs, 1, KVH_D, HD_D),
    )
```
